```python
import jax, jax.numpy as jnp
from jax import lax
import numpy as np

D_MODEL = 1024
BATCH = 8
SEQ = 2048
DEPTH = 4
DEC_BATCH = 128
DEC_SEQ = 1
PAST_LEN = 16384
PAGE_SIZE = 128

N_META = 16
CHUNK = 64
N_MIXERS = 2
N_GLA_LAYERS = (DEPTH + 1) // 2
N_SSD_LAYERS = DEPTH // 2
NORM_EPS = 1e-6

GLA_HEADS = 4
GLA_DK = D_MODEL // 2
GLA_DV = D_MODEL
GLA_HK = GLA_DK // GLA_HEADS
GLA_HV = GLA_DV // GLA_HEADS
GLA_RANK = 16
GLA_TAU = 16.0
GLA_IN = 2 * GLA_DK + 2 * GLA_DV + GLA_RANK

SSD_EXPAND = 2
SSD_DINNER = SSD_EXPAND * D_MODEL
SSD_HEADDIM = 64
SSD_HEADS = SSD_DINNER // SSD_HEADDIM
SSD_GROUPS = 4
SSD_HPG = SSD_HEADS // SSD_GROUPS
SSD_STATE = 128
SSD_CONV = 4
SSD_CONV_DIM = SSD_DINNER + 2 * SSD_GROUPS * SSD_STATE
SSD_IN = SSD_DINNER + SSD_CONV_DIM + SSD_HEADS

kernel_name = "hybrid_gla_ssd_meta_decoder_step"


def rmsnorm(x, w):
    xf = x.astype(jnp.float32)
    y = xf * lax.rsqrt(jnp.mean(xf * xf, axis=-1, keepdims=True) + NORM_EPS)
    return (y * w.astype(jnp.float32)).astype(x.dtype)


def chunked_scan(step, state, xs, chunk):
    b, l = xs[0].shape[:2]
    n = l // chunk
    xs_c = tuple(jnp.moveaxis(a.reshape((b, n, chunk) + a.shape[2:]), 1, 0) for a in xs)
    state, ys = lax.scan(step, state, xs_c)
    ys = jnp.moveaxis(ys, 0, 1).reshape((b, l) + ys.shape[3:])
    return state, ys


def segmented_scan(step, state, xs, seg_lens):
    outs = []
    start = 0
    for seg in seg_lens:
        part = tuple(a[:, start:start + seg] for a in xs)
        chunk = CHUNK if seg % CHUNK == 0 else seg
        state, y = chunked_scan(step, state, part, chunk)
        outs.append(y)
        start += seg
    return state, jnp.concatenate(outs, axis=1)


def gla_chunk_step(S, inp):
    q, k, v, g = inp
    c = q.shape[1]
    bcum = jnp.cumsum(g, axis=1)
    causal = jnp.tril(jnp.ones((c, c), dtype=bool))[None, :, :, None, None]
    decay = jnp.exp(jnp.where(causal, bcum[:, :, None] - bcum[:, None], -jnp.inf))
    attn = jnp.einsum('bihk,bjhk,bijhk->bhij', q, k, decay)
    o = jnp.einsum('bhij,bjhv->bihv', attn, v) + jnp.einsum('bihk,bhkv->bihv', q * jnp.exp(bcum), S)
    b_last = bcum[:, -1]
    k_dec = k * jnp.exp(b_last[:, None] - bcum)
    S = jnp.exp(b_last)[..., None] * S + jnp.einsum('bjhk,bjhv->bhkv', k_dec, v)
    return S, o


def ssd_chunk_step(S, inp):
    x, dt, a, Bm, Cm = inp
    c = x.shape[1]
    acum = jnp.cumsum(a, axis=1)
    causal = jnp.tril(jnp.ones((c, c), dtype=bool))[None, :, :, None, None]
    lmat = jnp.exp(jnp.where(causal, acum[:, :, None] - acum[:, None], -jnp.inf))
    cb = jnp.einsum('bign,bjgn->bijg', Cm, Bm)
    y = jnp.einsum('bijg,bijgr,bjgrp->bigrp', cb, lmat * dt[:, None], x)
    y = y + jnp.einsum('bign,bgrpn->bigrp', Cm, S) * jnp.exp(acum)[..., None]
    a_last = acum[:, -1]
    w = jnp.exp(a_last[:, None] - acum) * dt
    S = jnp.exp(a_last)[..., None, None] * S + jnp.einsum('bjgn,bjgr,bjgrp->bgrpn', Bm, w, x)
    return S, y


def gla_mixer(h, S0, seg_lens, w_in, w_gate_up, b_gate, w_norm, w_out):
    bsz, l, _ = h.shape
    f32 = jnp.float32
    proj = (h @ w_in).astype(f32)
    q, k, v, r, g_low = jnp.split(
        proj, [GLA_DK, 2 * GLA_DK, 2 * GLA_DK + GLA_DV, 2 * GLA_DK + 2 * GLA_DV], axis=-1)
    log_alpha = jax.nn.log_sigmoid(g_low @ w_gate_up.astype(f32) + b_gate.astype(f32)) / GLA_TAU
    q = q.reshape(bsz, l, GLA_HEADS, GLA_HK) * (GLA_HK ** -0.5)
    k = k.reshape(bsz, l, GLA_HEADS, GLA_HK)
    v = v.reshape(bsz, l, GLA_HEADS, GLA_HV)
    log_alpha = log_alpha.reshape(bsz, l, GLA_HEADS, GLA_HK)
    S, o = segmented_scan(gla_chunk_step, S0.astype(f32), (q, k, v, log_alpha), seg_lens)
    o = o * lax.rsqrt(jnp.mean(o * o, axis=-1, keepdims=True) + NORM_EPS)
    o = o * w_norm.astype(f32).reshape(GLA_HEADS, GLA_HV)
    o = o.reshape(bsz, l, GLA_DV) * jax.nn.silu(r)
    y = o.astype(h.dtype) @ w_out
    return y, S.astype(h.dtype)


def ssd_mixer(h, S0, conv_buf, seg_lens, w_in, conv_w, conv_b, dt_bias, a_log, d_skip, w_norm, w_out):
    bsz, l, _ = h.shape
    f32 = jnp.float32
    proj = (h @ w_in).astype(f32)
    z, xbc, dt_raw = jnp.split(proj, [SSD_DINNER, SSD_DINNER + SSD_CONV_DIM], axis=-1)
    xpad = jnp.concatenate([conv_buf.astype(f32), xbc], axis=1)
    new_buf = xpad[:, xpad.shape[1] - (SSD_CONV - 1):]
    cw = conv_w.astype(f32)
    conv = conv_b.astype(f32) + xpad[:, 0:l] * cw[0]
    for i in range(1, SSD_CONV):
        conv = conv + xpad[:, i:i + l] * cw[i]
    xbc = jax.nn.silu(conv)
    x, Bm, Cm = jnp.split(xbc, [SSD_DINNER, SSD_DINNER + SSD_GROUPS * SSD_STATE], axis=-1)
    x = x.reshape(bsz, l, SSD_GROUPS, SSD_HPG, SSD_HEADDIM)
    Bm = Bm.reshape(bsz, l, SSD_GROUPS, SSD_STATE)
    Cm = Cm.reshape(bsz, l, SSD_GROUPS, SSD_STATE)
    dt = jax.nn.softplus(dt_raw + dt_bias.astype(f32)).reshape(bsz, l, SSD_GROUPS, SSD_HPG)
    A = -jnp.exp(a_log.astype(f32)).reshape(SSD_GROUPS, SSD_HPG)
    S0g = S0.astype(f32).reshape(bsz, SSD_GROUPS, SSD_HPG, SSD_HEADDIM, SSD_STATE)
    S, y = segmented_scan(ssd_chunk_step, S0g, (x, dt, dt * A, Bm, Cm), seg_lens)
    y = y + d_skip.astype(f32).reshape(SSD_GROUPS, SSD_HPG)[..., None] * x
    y = y.reshape(bsz, l, SSD_DINNER) * jax.nn.silu(z)
    yg = y.reshape(bsz, l, SSD_GROUPS, SSD_DINNER // SSD_GROUPS)
    yg = yg * lax.rsqrt(jnp.mean(yg * yg, axis=-1, keepdims=True) + NORM_EPS)
    y = yg.reshape(bsz, l, SSD_DINNER) * w_norm.astype(f32)
    out = y.astype(h.dtype) @ w_out
    S = S.reshape(bsz, SSD_HEADS, SSD_HEADDIM, SSD_STATE)
    return out, S.astype(h.dtype), new_buf.astype(h.dtype)


def trunk(x, gla_states, ssm_states, conv_bufs, seg_lens, pre_norm, post_norm,
          gla_w_in, gla_w_gate_up, gla_b_gate, gla_w_norm, gla_w_out,
          ssd_w_in, ssd_conv_w, ssd_conv_b, ssd_dt_bias, ssd_a_log, ssd_d_skip, ssd_w_norm, ssd_w_out):
    new_gla, new_ssm, new_conv = [], [], []
    for i in range(DEPTH):
        h = rmsnorm(x, pre_norm[i])
        j = i // N_MIXERS
        if i % N_MIXERS == 0:
            y, S = gla_mixer(h, gla_states[j], seg_lens, gla_w_in[j], gla_w_gate_up[j],
                             gla_b_gate[j], gla_w_norm[j], gla_w_out[j])
            new_gla.append(S)
        else:
            y, S, buf = ssd_mixer(h, ssm_states[j], conv_bufs[j], seg_lens, ssd_w_in[j],
                                  ssd_conv_w[j], ssd_conv_b[j], ssd_dt_bias[j], ssd_a_log[j],
                                  ssd_d_skip[j], ssd_w_norm[j], ssd_w_out[j])
            new_ssm.append(S)
            new_conv.append(buf)
        x = x + rmsnorm(y, post_norm[i]).astype(x.dtype)
    return x, jnp.stack(new_gla), jnp.stack(new_ssm), jnp.stack(new_conv)


def setup_inputs(seed: int = 0) -> dict:
    key = jax.random.key(seed)
    ks = jax.random.split(key, 24)
    f32 = jnp.float32
    nrm = lambda k, shape, s: jax.random.normal(k, shape, f32) * s
    dt0 = jnp.exp(jax.random.uniform(ks[17], (N_SSD_LAYERS, SSD_HEADS), f32,
                                     np.log(1e-3).astype(np.float32), np.log(1e-1).astype(np.float32)))
    return {
        "x_prompt": nrm(ks[0], (BATCH, SEQ, D_MODEL), 1.0),
        "x_sample": nrm(ks[1], (DEC_BATCH, DEC_SEQ, D_MODEL), 1.0),
        "state_gla": nrm(ks[2], (N_GLA_LAYERS, DEC_BATCH, GLA_HEADS, GLA_HK, GLA_HV), 0.1),
        "state_ssm": nrm(ks[3], (N_SSD_LAYERS, DEC_BATCH, SSD_HEADS, SSD_HEADDIM, SSD_STATE), 0.1),
        "state_conv": nrm(ks[4], (N_SSD_LAYERS, DEC_BATCH, SSD_CONV - 1, SSD_CONV_DIM), 1.0),
        "meta_tokens": nrm(ks[5], (N_META, D_MODEL), 1.0),
        "pre_norm": 1.0 + nrm(ks[6], (DEPTH, D_MODEL), 0.02),
        "post_norm": 1.0 + nrm(ks[7], (DEPTH, D_MODEL), 0.02),
        "gla_w_in": nrm(ks[8], (N_GLA_LAYERS, D_MODEL, GLA_IN), D_MODEL ** -0.5),
        "gla_w_gate_up": nrm(ks[9], (N_GLA_LAYERS, GLA_RANK, GLA_DK), GLA_RANK ** -0.5),
        "gla_b_gate": nrm(ks[10], (N_GLA_LAYERS, GLA_DK), 0.1),
        "gla_w_norm": 1.0 + nrm(ks[11], (N_GLA_LAYERS, GLA_DV), 0.02),
        "gla_w_out": nrm(ks[12], (N_GLA_LAYERS, GLA_DV, D_MODEL), GLA_DV ** -0.5),
        "ssd_w_in": nrm(ks[13], (N_SSD_LAYERS, D_MODEL, SSD_IN), D_MODEL ** -0.5),
        "ssd_conv_w": nrm(ks[14], (N_SSD_LAYERS, SSD_CONV, SSD_CONV_DIM), SSD_CONV ** -0.5),
        "ssd_conv_b": nrm(ks[15], (N_SSD_LAYERS, SSD_CONV_DIM), 0.02),
        "ssd_dt_bias": dt0 + jnp.log(-jnp.expm1(-dt0)),
        "ssd_a_log": jnp.log(jax.random.uniform(ks[18], (N_SSD_LAYERS, SSD_HEADS), f32, 1.0, 16.0)),
        "ssd_d_skip": 1.0 + nrm(ks[19], (N_SSD_LAYERS, SSD_HEADS), 0.1),
        "ssd_w_norm": 1.0 + nrm(ks[20], (N_SSD_LAYERS, SSD_DINNER), 0.02),
        "ssd_w_out": nrm(ks[21], (N_SSD_LAYERS, SSD_DINNER, D_MODEL), SSD_DINNER ** -0.5),
    }


def reference(x_prompt, x_sample, state_gla, state_ssm, state_conv, meta_tokens, pre_norm, post_norm,
              gla_w_in, gla_w_gate_up, gla_b_gate, gla_w_norm, gla_w_out,
              ssd_w_in, ssd_conv_w, ssd_conv_b, ssd_dt_bias, ssd_a_log, ssd_d_skip, ssd_w_norm, ssd_w_out):
    weights = (pre_norm, post_norm, gla_w_in, gla_w_gate_up, gla_b_gate, gla_w_norm, gla_w_out,
               ssd_w_in, ssd_conv_w, ssd_conv_b, ssd_dt_bias, ssd_a_log, ssd_d_skip, ssd_w_norm, ssd_w_out)
    dt = x_prompt.dtype
    meta = jnp.broadcast_to(meta_tokens.astype(dt)[None], (BATCH, N_META, D_MODEL))
    xp = jnp.concatenate([meta, x_prompt], axis=1)
    gla0 = jnp.zeros((N_GLA_LAYERS, BATCH, GLA_HEADS, GLA_HK, GLA_HV), dt)
    ssm0 = jnp.zeros((N_SSD_LAYERS, BATCH, SSD_HEADS, SSD_HEADDIM, SSD_STATE), dt)
    conv0 = jnp.zeros((N_SSD_LAYERS, BATCH, SSD_CONV - 1, SSD_CONV_DIM), dt)
    yp, gla_p, ssm_p, conv_p = trunk(xp, gla0, ssm0, conv0, (N_META, SEQ), *weights)
    y_prompt = yp[:, N_META:]
    y_sample, gla_s, ssm_s, conv_s = trunk(x_sample, state_gla, state_ssm, state_conv,
                                           (DEC_SEQ,), *weights)
    return (y_prompt, y_sample, gla_p, ssm_p, conv_p, gla_s, ssm_s, conv_s)
```

```python
import functools

import jax
import jax.numpy as jnp
from jax import lax
from jax.experimental import pallas as pl
from jax.experimental.pallas import tpu as pltpu

F32 = jnp.float32
BF16 = jnp.bfloat16

D_MODEL = 1024
N_META = 16
NORM_EPS = 1e-6

GLA_HEADS = 4
GLA_DK = 512
GLA_DV = 1024
GLA_HK = 128
GLA_HV = 256
GLA_RANK = 16
GLA_TAU = 16.0

SSD_DINNER = 2048
SSD_HEADDIM = 64
SSD_HEADS = 32
SSD_GROUPS = 4
SSD_HPG = 8
SSD_STATE = 128
SSD_CONV = 4
SSD_CONV_DIM = 3072
SSD_GW = SSD_HPG * SSD_HEADDIM

LANES = 128
SUBLANES = 8
CHUNK = 128
VMEM_LIMIT = 56 * 1024 * 1024

HIGHEST = lax.Precision.HIGHEST


def _dot(a, b, precision=None):
    return jnp.dot(a, b, preferred_element_type=F32, precision=precision)


def _dot_nt(a, b):
    return lax.dot_general(a, b, (((1,), (1,)), ((), ())), preferred_element_type=F32)


def _dot_tn(a, b):
    return lax.dot_general(a, b, (((0,), (0,)), ((), ())), preferred_element_type=F32)


def _rms(x, w):
    return x * lax.rsqrt(jnp.mean(x * x, axis=-1, keepdims=True) + NORM_EPS) * w


def _silu(x):
    return x * (1.0 / (1.0 + jnp.exp(-x)))


def _softplus(x):
    return jnp.maximum(x, 0.0) + jnp.log1p(jnp.exp(-jnp.abs(x)))


def _log_sigmoid(x):
    return -_softplus(-x)


def _tri(n):
    r = lax.broadcasted_iota(jnp.int32, (n, n), 0)
    c = lax.broadcasted_iota(jnp.int32, (n, n), 1)
    return r >= c


def _gla_scan_chunk(q, k, v, g, st_ref, causal, tril_f):
    outs = []
    for h in range(GLA_HEADS):
        ks = slice(h * GLA_HK, (h + 1) * GLA_HK)
        vs = slice(h * GLA_HV, (h + 1) * GLA_HV)
        bc = _dot(tril_f, g[:, ks], precision=HIGHEST)
        b_last = bc[-1:, :]
        e_last = jnp.exp(b_last)
        qh = (q[:, ks] * jnp.exp(bc)).astype(BF16)
        kh = k[:, ks] * jnp.exp(-bc)
        kd = (kh * e_last).astype(BF16)
        vh = v[:, vs].astype(BF16)
        attn = jnp.where(causal, _dot_nt(qh, kh.astype(BF16)), 0.0)
        st = st_ref[h]
        o = _dot(attn.astype(BF16), vh) + _dot_nt(qh, st.astype(BF16))
        st_ref[h] = st * e_last + _dot_tn(vh, kd)
        outs.append(o)
    return jnp.concatenate(outs, axis=1)


def _gla_tail(o, r, wn, wo, postw, x):
    parts = []
    for h in range(GLA_HEADS):
        vs = slice(h * GLA_HV, (h + 1) * GLA_HV)
        parts.append(_rms(o[:, vs], wn[:, vs]))
    on = jnp.concatenate(parts, axis=1) * _silu(r)
    y = _dot(on.astype(BF16), wo)
    return x + _rms(y, postw)


def _gla_prompt_kernel(x_ref, s0_ref, prew_ref, postw_ref, wq_ref, wk_ref, wv_ref, wr_ref,
                       wg_ref, wgu_ref, bg_ref, wn_ref, wo_ref,
                       y_ref, sout_ref, st_ref, *, n_pad, tile):
    t = pl.program_id(1)

    @pl.when(t == 0)
    def _():
        for h in range(GLA_HEADS):
            st_ref[h] = s0_ref[0, h].T

    x = x_ref[0]
    hn = _rms(x, prew_ref[...])
    if n_pad:
        row = lax.broadcasted_iota(jnp.int32, (tile, 1), 0) + t * tile
        hn = jnp.where(row >= n_pad, hn, 0.0)
    hb = hn.astype(BF16)
    q = _dot(hb, wq_ref[...]) * (GLA_HK ** -0.5)
    k = _dot(hb, wk_ref[...])
    v = _dot(hb, wv_ref[...])
    r = _dot(hb, wr_ref[...])
    gl = _dot(hb, wg_ref[...])
    g = _log_sigmoid(_dot(gl.astype(BF16), wgu_ref[...]) + bg_ref[...]) * (1.0 / GLA_TAU)

    causal = _tri(CHUNK)
    tril_f = causal.astype(F32)
    outs = []
    for c in range(tile // CHUNK):
        rows = slice(c * CHUNK, (c + 1) * CHUNK)
        outs.append(_gla_scan_chunk(q[rows], k[rows], v[rows], g[rows], st_ref, causal, tril_f))
    o = jnp.concatenate(outs, axis=0) if len(outs) > 1 else outs[0]
    y_ref[0] = _gla_tail(o, r, wn_ref[...], wo_ref[...], postw_ref[...], x)

    @pl.when(t == pl.num_programs(1) - 1)
    def _():
        for h in range(GLA_HEADS):
            sout_ref[0, h] = st_ref[h].T


def _full(shape):
    nd = len(shape)
    return pl.BlockSpec(shape, lambda *_: (0,) * nd)


def _gla_prompt_layer(x, s0, w, *, n_pad, tile):
    bsz, seq, _ = x.shape
    b0 = s0.shape[0]
    s_idx = (lambda b, t: (b, 0, 0, 0)) if b0 == bsz else (lambda b, t: (0, 0, 0, 0))
    weights = (w["pre"], w["post"], w["wq"], w["wk"], w["wv"], w["wr"], w["wg"], w["wgu"],
               w["bg"], w["wn"], w["wo"])
    return pl.pallas_call(
        functools.partial(_gla_prompt_kernel, n_pad=n_pad, tile=tile),
        grid=(bsz, seq // tile),
        in_specs=[pl.BlockSpec((1, tile, D_MODEL), lambda b, t: (b, t, 0)),
                  pl.BlockSpec((1, GLA_HEADS, GLA_HK, GLA_HV), s_idx)]
                 + [_full(a.shape) for a in weights],
        out_specs=[pl.BlockSpec((1, tile, D_MODEL), lambda b, t: (b, t, 0)),
                   pl.BlockSpec((1, GLA_HEADS, GLA_HK, GLA_HV), lambda b, t: (b, 0, 0, 0))],
        out_shape=[jax.ShapeDtypeStruct(x.shape, F32),
                   jax.ShapeDtypeStruct((bsz, GLA_HEADS, GLA_HK, GLA_HV), F32)],
        scratch_shapes=[pltpu.VMEM((GLA_HEADS, GLA_HV, GLA_HK), F32)],
        compiler_params=pltpu.CompilerParams(
            dimension_semantics=("arbitrary", "arbitrary"), vmem_limit_bytes=VMEM_LIMIT),
        name="gla_prompt",
    )(x, s0, *weights)


def _ssd_conv(xbc, cbuf_ref, cw, cb, tile):
    cbuf_ref[SUBLANES:SUBLANES + tile, :] = xbc
    conv = cb + xbc * cw[3:4, :]
    for i in range(SSD_CONV - 1):
        off = SUBLANES - (SSD_CONV - 1) + i
        conv = conv + cbuf_ref[off:off + tile, :] * cw[i:i + 1, :]
    return _silu(conv)


def _ssd_scan_chunk(xs, bm, cm, dt, a, dskip, expand, st_ref, y_ref, row0, causal, tril_f):
    ac = _dot(tril_f, a, precision=HIGHEST)
    ac_t = ac.T
    dt_t = dt.T
    ea = jnp.exp(ac)
    a_last = ac[-1:, :]
    wgt = jnp.exp(a_last - ac) * dt
    e_last = _dot(jnp.exp(a_last), expand, precision=HIGHEST)
    for gi in range(SSD_GROUPS):
        ns = slice(gi * SSD_STATE, (gi + 1) * SSD_STATE)
        gs = slice(gi * SSD_GW, (gi + 1) * SSD_GW)
        bg = bm[:, ns].astype(BF16)
        cg = cm[:, ns].astype(BF16)
        cb = jnp.where(causal, _dot_nt(cg, bg), 0.0)
        st = st_ref[gi]
        y_int = _dot(cg, st.astype(BF16))
        xw = []
        for r in range(SSD_HPG):
            h = gi * SSD_HPG + r
            ps = slice(h * SSD_HEADDIM, (h + 1) * SSD_HEADDIM)
            seg = jnp.minimum(ac[:, h:h + 1] - ac_t[h:h + 1, :], 0.0)
            m = cb * jnp.exp(seg) * dt_t[h:h + 1, :]
            xh = xs[:, ps]
            yh = (_dot(m.astype(BF16), xh.astype(BF16))
                  + y_int[:, r * SSD_HEADDIM:(r + 1) * SSD_HEADDIM] * ea[:, h:h + 1]
                  + xh * dskip[:, ps])
            y_ref[row0:row0 + CHUNK, ps] = yh
            xw.append(xh * wgt[:, h:h + 1])
        xw = jnp.concatenate(xw, axis=1).astype(BF16)
        st_ref[gi] = st * e_last[:, gs] + _dot_tn(bg, xw)


def _ssd_tail(y, z, wn, wo, postw, x):
    y = y * _silu(z)
    parts = []
    for gi in range(SSD_GROUPS):
        gs = slice(gi * SSD_GW, (gi + 1) * SSD_GW)
        parts.append(_rms(y[:, gs], wn[:, gs]))
    yn = jnp.concatenate(parts, axis=1)
    out = _dot(yn.astype(BF16), wo)
    return x + _rms(out, postw)


def _ssd_prompt_kernel(x_ref, s0_ref, c0_ref, prew_ref, postw_ref, wz_ref, wx_ref, wdt_ref,
                       cw_ref, cb_ref, dtb_ref, alog_ref, dskip_ref, expand_ref, wn_ref, wo_ref,
                       y_ref, sout_ref, cout_ref, st_ref, cbuf_ref, ys_ref, *, n_pad, tile):
    t = pl.program_id(1)

    @pl.when(t == 0)
    def _():
        for gi in range(SSD_GROUPS):
            s0 = s0_ref[0, gi * SSD_HPG:(gi + 1) * SSD_HPG].reshape(SSD_GW, SSD_STATE)
            st_ref[gi] = s0.T
        cbuf_ref[0:SUBLANES, :] = c0_ref[0]

    x = x_ref[0]
    hn = _rms(x, prew_ref[...])
    row = lax.broadcasted_iota(jnp.int32, (tile, 1), 0) + t * tile
    if n_pad:
        hn = jnp.where(row >= n_pad, hn, 0.0)
    hb = hn.astype(BF16)
    z = _dot(hb, wz_ref[...])
    xbc = _dot(hb, wx_ref[...])
    dtr = _dot(hb, wdt_ref[...])
    xc = _ssd_conv(xbc, cbuf_ref, cw_ref[...], cb_ref[...], tile)
    cbuf_ref[0:SUBLANES, :] = cbuf_ref[tile:tile + SUBLANES, :]
    xs = xc[:, :SSD_DINNER]
    bm = xc[:, SSD_DINNER:SSD_DINNER + SSD_GROUPS * SSD_STATE]
    cm = xc[:, SSD_DINNER + SSD_GROUPS * SSD_STATE:]
    dt = _softplus(dtr + dtb_ref[...])
    if n_pad:
        dt = jnp.where(row >= n_pad, dt, 0.0)
    a = dt * (-jnp.exp(alog_ref[...]))

    causal = _tri(CHUNK)
    tril_f = causal.astype(F32)
    dskip = dskip_ref[...]
    expand = expand_ref[...]
    for c in range(tile // CHUNK):
        rows = slice(c * CHUNK, (c + 1) * CHUNK)
        _ssd_scan_chunk(xs[rows], bm[rows], cm[rows], dt[rows], a[rows], dskip, expand,
                        st_ref, ys_ref, c * CHUNK, causal, tril_f)
    y_ref[0] = _ssd_tail(ys_ref[...], z, wn_ref[...], wo_ref[...], postw_ref[...], x)

    @pl.when(t == pl.num_programs(1) - 1)
    def _():
        for gi in range(SSD_GROUPS):
            sout_ref[0, gi * SSD_HPG:(gi + 1) * SSD_HPG] = (
                st_ref[gi].T.reshape(SSD_HPG, SSD_HEADDIM, SSD_STATE))
        cout_ref[0] = cbuf_ref[0:SUBLANES, :]


def _ssd_prompt_layer(x, s0, c0, w, *, n_pad, tile):
    bsz, seq, _ = x.shape
    b0 = s0.shape[0]
    if b0 == bsz:
        s_idx = lambda b, t: (b, 0, 0, 0)
        c_idx = lambda b, t: (b, 0, 0)
    else:
        s_idx = lambda b, t: (0, 0, 0, 0)
        c_idx = lambda b, t: (0, 0, 0)
    weights = (w["pre"], w["post"], w["wz"], w["wx"], w["wdt"], w["cw"], w["cb"], w["dtb"],
               w["alog"], w["dskip"], w["expand"], w["wn"], w["wo"])
    return pl.pallas_call(
        functools.partial(_ssd_prompt_kernel, n_pad=n_pad, tile=tile),
        grid=(bsz, seq // tile),
        in_specs=[pl.BlockSpec((1, tile, D_MODEL), lambda b, t: (b, t, 0)),
                  pl.BlockSpec((1, SSD_HEADS, SSD_HEADDIM, SSD_STATE), s_idx),
                  pl.BlockSpec((1, SUBLANES, SSD_CONV_DIM), c_idx)]
                 + [_full(a.shape) for a in weights],
        out_specs=[pl.BlockSpec((1, tile, D_MODEL), lambda b, t: (b, t, 0)),
                   pl.BlockSpec((1, SSD_HEADS, SSD_HEADDIM, SSD_STATE), lambda b, t: (b, 0, 0, 0)),
                   pl.BlockSpec((1, SUBLANES, SSD_CONV_DIM), lambda b, t: (b, 0, 0))],
        out_shape=[jax.ShapeDtypeStruct(x.shape, F32),
                   jax.ShapeDtypeStruct((bsz, SSD_HEADS, SSD_HEADDIM, SSD_STATE), F32),
                   jax.ShapeDtypeStruct((bsz, SUBLANES, SSD_CONV_DIM), F32)],
        scratch_shapes=[pltpu.VMEM((SSD_GROUPS, SSD_STATE, SSD_GW), F32),
                        pltpu.VMEM((tile + SUBLANES, SSD_CONV_DIM), F32),
                        pltpu.VMEM((tile, SSD_DINNER), F32)],
        compiler_params=pltpu.CompilerParams(
            dimension_semantics=("arbitrary", "arbitrary"), vmem_limit_bytes=VMEM_LIMIT),
        name="ssd_prompt",
    )(x, s0, c0, *weights)


def _column(row, width, lane0=0):
    seg = row[:, lane0:lane0 + width]
    r = lax.broadcasted_iota(jnp.int32, (width, width), 0)
    c = lax.broadcasted_iota(jnp.int32, (width, width), 1)
    return jnp.sum(jnp.where(r == c, seg, 0.0), axis=1, keepdims=True)


def _gla_step_kernel(x_ref, s_ref, prew_ref, postw_ref, wq_ref, wk_ref, wv_ref, wr_ref,
                     wg_ref, wgu_ref, bg_ref, wn_ref, wo_ref, *rest, bb, has_prev):
    y_ref, sout_ref, q_s, k_s, v_s, e_s, o_s = rest[1:] if has_prev else rest
    i = pl.program_id(0)

    @pl.when(i == 0)
    def _():
        hb = _rms(x_ref[...], prew_ref[...]).astype(BF16)
        q_s[...] = _dot(hb, wq_ref[...]) * (GLA_HK ** -0.5)
        k_s[...] = _dot(hb, wk_ref[...])
        v_s[...] = _dot(hb, wv_ref[...])
        gl = _dot(hb, wg_ref[...])
        g = _log_sigmoid(_dot(gl.astype(BF16), wgu_ref[...]) + bg_ref[...]) * (1.0 / GLA_TAU)
        e_s[...] = jnp.exp(g)

    def body(j, carry):
        row = i * bb + j
        qr = q_s[pl.ds(row, 1), :]
        kr = k_s[pl.ds(row, 1), :]
        vr = v_s[pl.ds(row, 1), :]
        er = e_s[pl.ds(row, 1), :]
        for h in range(GLA_HEADS):
            vs = slice(h * GLA_HV, (h + 1) * GLA_HV)
            ecol = _column(er, GLA_HK, h * GLA_HK)
            kcol = _column(kr, GLA_HK, h * GLA_HK)
            qcol = _column(qr, GLA_HK, h * GLA_HK)
            s_new = s_ref[j, h] * ecol + kcol * vr[:, vs]
            sout_ref[j, h] = s_new
            o_s[pl.ds(row, 1), vs] = jnp.sum(qcol * s_new, axis=0, keepdims=True)
        return carry

    lax.fori_loop(0, bb, body, 0)

    @pl.when(i == pl.num_programs(0) - 1)
    def _():
        x = x_ref[...]
        hb = _rms(x, prew_ref[...]).astype(BF16)
        r = _dot(hb, wr_ref[...])
        y_ref[...] = _gla_tail(o_s[...], r, wn_ref[...], wo_ref[...], postw_ref[...], x)


def _layer_block(layer, bb, tail):
    nz = len(tail)
    return pl.BlockSpec((None, bb) + tail, lambda i, *_: (layer, i) + (0,) * nz)


def _gla_step_layer(x, states, prev_out, layer, w, *, bb=8):
    bsz = x.shape[0]
    weights = (w["pre"], w["post"], w["wq"], w["wk"], w["wv"], w["wr"], w["wg"], w["wgu"],
               w["bg"], w["wn"], w["wo"])
    sblk = _layer_block(layer, bb, (GLA_HEADS, GLA_HK, GLA_HV))
    carried = [] if prev_out is None else [prev_out]
    return pl.pallas_call(
        functools.partial(_gla_step_kernel, bb=bb, has_prev=prev_out is not None),
        grid=(bsz // bb,),
        in_specs=[_full(x.shape), sblk] + [_full(a.shape) for a in weights]
                 + [pl.BlockSpec(memory_space=pl.ANY)] * len(carried),
        out_specs=[_full(x.shape), sblk],
        out_shape=[jax.ShapeDtypeStruct(x.shape, F32), jax.ShapeDtypeStruct(states.shape, F32)],
        scratch_shapes=[pltpu.VMEM((bsz, GLA_DK), F32), pltpu.VMEM((bsz, GLA_DK), F32),
                        pltpu.VMEM((bsz, GLA_DV), F32), pltpu.VMEM((bsz, GLA_DK), F32),
                        pltpu.VMEM((bsz, GLA_DV), F32)],
        input_output_aliases={2 + len(weights): 1} if carried else {},
        compiler_params=pltpu.CompilerParams(
            dimension_semantics=("arbitrary",), vmem_limit_bytes=VMEM_LIMIT),
        name="gla_step",
    )(x, states, *weights, *carried)


def _ssd_step_kernel(x_ref, s_ref, cv_ref, prew_ref, postw_ref, wz_ref, wx_ref, wdt_ref,
                     cw_ref, cb_ref, dtb_ref, alog_ref, dskip_ref, expand_ref, expand_n_ref,
                     wn_ref, wo_ref, *rest, bb, has_prev):
    (y_ref, sout_ref, cvout_ref, xd_s, xs_s, b_s, c_s, e_s, yt_s) = rest[2:] if has_prev else rest
    i = pl.program_id(0)
    bsz = x_ref.shape[0]
    cd = SSD_CONV_DIM

    @pl.when(i == 0)
    def _():
        hb = _rms(x_ref[...], prew_ref[...]).astype(BF16)
        xbc = _dot(hb, wx_ref[...])
        dtr = _dot(hb, wdt_ref[...])
        cw = cw_ref[...]
        conv = cb_ref[...] + xbc * cw[3:4, :]
        for t in range(SSD_CONV - 1):
            conv = conv + cv_ref[:, t * cd:(t + 1) * cd] * cw[t:t + 1, :]
        cvout_ref[:, 0:cd] = cv_ref[:, cd:2 * cd]
        cvout_ref[:, cd:2 * cd] = cv_ref[:, 2 * cd:3 * cd]
        cvout_ref[:, 2 * cd:3 * cd] = xbc
        xc = _silu(conv)
        xs = xc[:, :SSD_DINNER]
        dt = _softplus(dtr + dtb_ref[...])
        ea = jnp.exp(dt * (-jnp.exp(alog_ref[...])))
        xs_s[...] = xs
        xd_s[...] = xs * _dot(dt, expand_ref[...], precision=HIGHEST)
        b_s[...] = xc[:, SSD_DINNER:SSD_DINNER + SSD_GROUPS * SSD_STATE]
        c_s[...] = xc[:, SSD_DINNER + SSD_GROUPS * SSD_STATE:]
        e_s[...] = _dot(ea, expand_n_ref[...], precision=HIGHEST)

    lane = lax.broadcasted_iota(jnp.int32, (SSD_HEADDIM, bsz), 1)

    def body(j, carry):
        row = i * bb + j
        xr = xd_s[pl.ds(row, 1), :]
        br = b_s[pl.ds(row, 1), :]
        cr = c_s[pl.ds(row, 1), :]
        er = e_s[pl.ds(row, 1), :]
        hit = lane == row
        for h in range(SSD_HEADS):
            gi = h // SSD_HPG
            ns = slice(gi * SSD_STATE, (gi + 1) * SSD_STATE)
            xcol = _column(xr, SSD_HEADDIM, h * SSD_HEADDIM)
            s_new = (s_ref[j, h] * er[:, h * SSD_STATE:(h + 1) * SSD_STATE] + xcol * br[:, ns])
            sout_ref[j, h] = s_new
            ycol = jnp.sum(s_new * cr[:, ns], axis=1, keepdims=True)
            hs = slice(h * SSD_HEADDIM, (h + 1) * SSD_HEADDIM)
            yt_s[hs, :] = jnp.where(hit, ycol, yt_s[hs, :])
        return carry

    lax.fori_loop(0, bb, body, 0)

    @pl.when(i == pl.num_programs(0) - 1)
    def _():
        x = x_ref[...]
        hb = _rms(x, prew_ref[...]).astype(BF16)
        z = _dot(hb, wz_ref[...])
        y = yt_s[...].T + xs_s[...] * dskip_ref[...]
        y_ref[...] = _ssd_tail(y, z, wn_ref[...], wo_ref[...], postw_ref[...], x)


def _ssd_step_layer(x, states, convs, prev_out, layer, w, *, bb=4):
    bsz = x.shape[0]
    weights = (w["pre"], w["post"], w["wz"], w["wx"], w["wdt"], w["cw"], w["cb"], w["dtb"],
               w["alog"], w["dskip"], w["expand"], w["expand_n"], w["wn"], w["wo"])
    sblk = _layer_block(layer, bb, (SSD_HEADS, SSD_HEADDIM, SSD_STATE))
    cblk = pl.BlockSpec((None,) + convs.shape[1:], lambda i: (layer, 0, 0))
    carried = [] if prev_out is None else list(prev_out)
    n_in = 3 + len(weights)
    return pl.pallas_call(
        functools.partial(_ssd_step_kernel, bb=bb, has_prev=prev_out is not None),
        grid=(bsz // bb,),
        in_specs=[_full(x.shape), sblk, cblk] + [_full(a.shape) for a in weights]
                 + [pl.BlockSpec(memory_space=pl.ANY)] * len(carried),
        out_specs=[_full(x.shape), sblk, cblk],
        out_shape=[jax.ShapeDtypeStruct(x.shape, F32), jax.ShapeDtypeStruct(states.shape, F32),
                   jax.ShapeDtypeStruct(convs.shape, F32)],
        input_output_aliases={n_in: 1, n_in + 1: 2} if carried else {},
        scratch_shapes=[pltpu.VMEM((bsz, SSD_DINNER), F32), pltpu.VMEM((bsz, SSD_DINNER), F32),
                        pltpu.VMEM((bsz, SSD_GROUPS * SSD_STATE), F32),
                        pltpu.VMEM((bsz, SSD_GROUPS * SSD_STATE), F32),
                        pltpu.VMEM((bsz, SSD_HEADS * SSD_STATE), F32),
                        pltpu.VMEM((SSD_DINNER, bsz), F32)],
        compiler_params=pltpu.CompilerParams(
            dimension_semantics=("arbitrary",), vmem_limit_bytes=VMEM_LIMIT),
        name="ssd_step",
    )(x, states, convs, *weights, *carried)


def _pad_lanes(a, n=LANES):
    return jnp.pad(a, ((0, 0), (0, n - a.shape[1])))


def _gla_weights(j, pre, post, w_in, w_gate_up, b_gate, w_norm, w_out):
    wi = w_in[j]
    o1, o2, o3, o4 = GLA_DK, 2 * GLA_DK, 2 * GLA_DK + GLA_DV, 2 * GLA_DK + 2 * GLA_DV
    return {
        "pre": pre[None, :], "post": post[None, :],
        "wq": wi[:, :o1].astype(BF16), "wk": wi[:, o1:o2].astype(BF16),
        "wv": wi[:, o2:o3].astype(BF16), "wr": wi[:, o3:o4].astype(BF16),
        "wg": _pad_lanes(wi[:, o4:]).astype(BF16),
        "wgu": jnp.pad(w_gate_up[j], ((0, LANES - GLA_RANK), (0, 0))).astype(BF16),
        "bg": b_gate[j][None, :], "wn": w_norm[j][None, :], "wo": w_out[j].astype(BF16),
    }


def _ssd_weights(j, pre, post, w_in, conv_w, conv_b, dt_bias, a_log, d_skip, w_norm, w_out):
    wi = w_in[j]
    o1, o2 = SSD_DINNER, SSD_DINNER + SSD_CONV_DIM
    head_of_lane = jnp.arange(SSD_DINNER) // SSD_HEADDIM
    expand = (jnp.arange(LANES)[:, None] == head_of_lane[None, :]).astype(F32)
    head_of_lane_n = jnp.arange(SSD_HEADS * SSD_STATE) // SSD_STATE
    expand_n = (jnp.arange(LANES)[:, None] == head_of_lane_n[None, :]).astype(F32)
    return {
        "pre": pre[None, :], "post": post[None, :],
        "wz": wi[:, :o1].astype(BF16), "wx": wi[:, o1:o2].astype(BF16),
        "wdt": _pad_lanes(wi[:, o2:]).astype(BF16),
        "cw": jnp.pad(conv_w[j], ((0, SUBLANES - SSD_CONV), (0, 0))), "cb": conv_b[j][None, :],
        "dtb": _pad_lanes(dt_bias[j][None, :]), "alog": _pad_lanes(a_log[j][None, :]),
        "dskip": jnp.repeat(d_skip[j], SSD_HEADDIM)[None, :],
        "expand": expand, "expand_n": expand_n,
        "wn": w_norm[j][None, :], "wo": w_out[j].astype(BF16),
    }


def _prompt_trunk(x, gla_s, ssm_s, conv_s, gw, sw, *, n_pad, tile):
    new_gla, new_ssm, new_conv = [], [], []
    for j in range(2):
        x, s = _gla_prompt_layer(x, gla_s[j], gw[j], n_pad=n_pad, tile=tile)
        new_gla.append(s)
        x, s, c = _ssd_prompt_layer(x, ssm_s[j], conv_s[j], sw[j], n_pad=n_pad, tile=tile)
        new_ssm.append(s)
        new_conv.append(c)
    return x, new_gla, new_ssm, new_conv


def kernel(x_prompt, x_sample, state_gla, state_ssm, state_conv, meta_tokens, pre_norm, post_norm,
           gla_w_in, gla_w_gate_up, gla_b_gate, gla_w_norm, gla_w_out,
           ssd_w_in, ssd_conv_w, ssd_conv_b, ssd_dt_bias, ssd_a_log, ssd_d_skip, ssd_w_norm, ssd_w_out):
    gw = [_gla_weights(j, pre_norm[2 * j], post_norm[2 * j], gla_w_in, gla_w_gate_up, gla_b_gate,
                       gla_w_norm, gla_w_out) for j in range(2)]
    sw = [_ssd_weights(j, pre_norm[2 * j + 1], post_norm[2 * j + 1], ssd_w_in, ssd_conv_w,
                       ssd_conv_b, ssd_dt_bias, ssd_a_log, ssd_d_skip, ssd_w_norm, ssd_w_out)
          for j in range(2)]
    bsz = x_prompt.shape[0]

    n_pad = CHUNK - N_META
    x_meta = jnp.pad(meta_tokens.astype(F32), ((n_pad, 0), (0, 0)))[None]
    zg = jnp.zeros((1, GLA_HEADS, GLA_HK, GLA_HV), F32)
    zs = jnp.zeros((1, SSD_HEADS, SSD_HEADDIM, SSD_STATE), F32)
    zc = jnp.zeros((1, SUBLANES, SSD_CONV_DIM), F32)
    _, mg, ms, mc = _prompt_trunk(x_meta, [zg, zg], [zs, zs], [zc, zc], gw, sw,
                                  n_pad=n_pad, tile=CHUNK)
    y_prompt, pg, ps, pc = _prompt_trunk(x_prompt, mg, ms, mc, gw, sw, n_pad=0, tile=256)
    gla_p = jnp.stack(pg)
    ssm_p = jnp.stack(ps)
    conv_p = jnp.stack(pc)[:, :, SUBLANES - (SSD_CONV - 1):, :]

    xs = x_sample[:, 0, :]
    sbsz = xs.shape[0]
    convs = state_conv.reshape(2, sbsz, (SSD_CONV - 1) * SSD_CONV_DIM)
    gla_s = ssm_conv_s = None
    for j in range(2):
        xs, gla_s = _gla_step_layer(xs, state_gla, gla_s, j, gw[j])
        xs, *ssm_conv_s = _ssd_step_layer(xs, state_ssm, convs, ssm_conv_s, j, sw[j])
    ssm_s, conv_s = ssm_conv_s
    y_sample = xs[:, None, :]
    return (y_prompt, y_sample, gla_p, ssm_p, conv_p,
            gla_s, ssm_s, conv_s.reshape(2, sbsz, SSD_CONV - 1, SSD_CONV_DIM))
```

```python
import functools

import jax
import jax.numpy as jnp
from jax import lax
from jax.experimental import pallas as pl
from jax.experimental.pallas import tpu as pltpu

F32 = jnp.float32
BF16 = jnp.bfloat16

D_MODEL = 1024
N_META = 16
NORM_EPS = 1e-6

GLA_HEADS = 4
GLA_DK = 512
GLA_DV = 1024
GLA_HK = 128
GLA_HV = 256
GLA_RANK = 16
GLA_TAU = 16.0

SSD_DINNER = 2048
SSD_HEADDIM = 64
SSD_HEADS = 32
SSD_GROUPS = 4
SSD_HPG = 8
SSD_STATE = 128
SSD_CONV = 4
SSD_CONV_DIM = 3072
SSD_GW = SSD_HPG * SSD_HEADDIM

LANES = 128
SUBLANES = 8
CHUNK = 128
VMEM_LIMIT = 56 * 1024 * 1024

HIGHEST = lax.Precision.HIGHEST


def _dot(a, b, precision=None):
    return jnp.dot(a, b, preferred_element_type=F32, precision=precision)


def _dot_nt(a, b):
    return lax.dot_general(a, b, (((1,), (1,)), ((), ())), preferred_element_type=F32)


def _dot_tn(a, b):
    return lax.dot_general(a, b, (((0,), (0,)), ((), ())), preferred_element_type=F32)


def _rms(x, w):
    return x * lax.rsqrt(jnp.mean(x * x, axis=-1, keepdims=True) + NORM_EPS) * w


def _silu(x):
    return x * (1.0 / (1.0 + jnp.exp(-x)))


def _softplus(x):
    return jnp.maximum(x, 0.0) + jnp.log1p(jnp.exp(-jnp.abs(x)))


def _log_sigmoid(x):
    return -_softplus(-x)


def _tri(n):
    r = lax.broadcasted_iota(jnp.int32, (n, n), 0)
    c = lax.broadcasted_iota(jnp.int32, (n, n), 1)
    return r >= c


def _gla_scan_chunk(q, k, v, g, st_ref, causal, tril_f):
    outs = []
    for h in range(GLA_HEADS):
        ks = slice(h * GLA_HK, (h + 1) * GLA_HK)
        vs = slice(h * GLA_HV, (h + 1) * GLA_HV)
        bc = _dot(tril_f, g[:, ks], precision=HIGHEST)
        b_last = bc[-1:, :]
        e_last = jnp.exp(b_last)
        qh = (q[:, ks] * jnp.exp(bc)).astype(BF16)
        kh = k[:, ks] * jnp.exp(-bc)
        kd = (kh * e_last).astype(BF16)
        vh = v[:, vs].astype(BF16)
        attn = jnp.where(causal, _dot_nt(qh, kh.astype(BF16)), 0.0)
        st = st_ref[h]
        o = _dot(attn.astype(BF16), vh) + _dot_nt(qh, st.astype(BF16))
        st_ref[h] = st * e_last + _dot_tn(vh, kd)
        outs.append(o)
    return jnp.concatenate(outs, axis=1)


def _gla_tail(o, r, wn, wo, postw, x):
    parts = []
    for h in range(GLA_HEADS):
        vs = slice(h * GLA_HV, (h + 1) * GLA_HV)
        parts.append(_rms(o[:, vs], wn[:, vs]))
    on = jnp.concatenate(parts, axis=1) * _silu(r)
    y = _dot(on.astype(BF16), wo)
    return x + _rms(y, postw)


def _gla_prompt_kernel(x_ref, s0_ref, prew_ref, postw_ref, wq_ref, wk_ref, wv_ref, wr_ref,
                       wg_ref, wgu_ref, bg_ref, wn_ref, wo_ref,
                       y_ref, sout_ref, st_ref, *, n_pad, tile):
    t = pl.program_id(1)

    @pl.when(t == 0)
    def _():
        for h in range(GLA_HEADS):
            st_ref[h] = s0_ref[0, h].T

    x = x_ref[0]
    hn = _rms(x, prew_ref[...])
    if n_pad:
        row = lax.broadcasted_iota(jnp.int32, (tile, 1), 0) + t * tile
        hn = jnp.where(row >= n_pad, hn, 0.0)
    hb = hn.astype(BF16)
    q = _dot(hb, wq_ref[...]) * (GLA_HK ** -0.5)
    k = _dot(hb, wk_ref[...])
    v = _dot(hb, wv_ref[...])
    r = _dot(hb, wr_ref[...])
    gl = _dot(hb, wg_ref[...])
    g = _log_sigmoid(_dot(gl.astype(BF16), wgu_ref[...]) + bg_ref[...]) * (1.0 / GLA_TAU)

    causal = _tri(CHUNK)
    tril_f = causal.astype(F32)
    outs = []
    for c in range(tile // CHUNK):
        rows = slice(c * CHUNK, (c + 1) * CHUNK)
        outs.append(_gla_scan_chunk(q[rows], k[rows], v[rows], g[rows], st_ref, causal, tril_f))
    o = jnp.concatenate(outs, axis=0) if len(outs) > 1 else outs[0]
    y_ref[0] = _gla_tail(o, r, wn_ref[...], wo_ref[...], postw_ref[...], x)

    @pl.when(t == pl.num_programs(1) - 1)
    def _():
        for h in range(GLA_HEADS):
            sout_ref[0, h] = st_ref[h].T


def _full(shape):
    nd = len(shape)
    return pl.BlockSpec(shape, lambda *_: (0,) * nd, pipeline_mode=pl.Buffered(1))


def _gla_prompt_layer(x, s0, w, *, n_pad, tile):
    bsz, seq, _ = x.shape
    b0 = s0.shape[0]
    s_idx = (lambda b, t: (b, 0, 0, 0)) if b0 == bsz else (lambda b, t: (0, 0, 0, 0))
    weights = (w["pre"], w["post"], w["wq"], w["wk"], w["wv"], w["wr"], w["wg"], w["wgu"],
               w["bg"], w["wn"], w["wo"])
    return pl.pallas_call(
        functools.partial(_gla_prompt_kernel, n_pad=n_pad, tile=tile),
        grid=(bsz, seq // tile),
        in_specs=[pl.BlockSpec((1, tile, D_MODEL), lambda b, t: (b, t, 0)),
                  pl.BlockSpec((1, GLA_HEADS, GLA_HK, GLA_HV), s_idx)]
                 + [_full(a.shape) for a in weights],
        out_specs=[pl.BlockSpec((1, tile, D_MODEL), lambda b, t: (b, t, 0)),
                   pl.BlockSpec((1, GLA_HEADS, GLA_HK, GLA_HV), lambda b, t: (b, 0, 0, 0))],
        out_shape=[jax.ShapeDtypeStruct(x.shape, F32),
                   jax.ShapeDtypeStruct((bsz, GLA_HEADS, GLA_HK, GLA_HV), F32)],
        scratch_shapes=[pltpu.VMEM((GLA_HEADS, GLA_HV, GLA_HK), F32)],
        compiler_params=pltpu.CompilerParams(
            dimension_semantics=("arbitrary", "arbitrary"), vmem_limit_bytes=VMEM_LIMIT),
        name="gla_prompt",
    )(x, s0, *weights)


def _ssd_conv(xbc, cbuf_ref, cw, cb, tile):
    cbuf_ref[SUBLANES:SUBLANES + tile, :] = xbc
    conv = cb + xbc * cw[3:4, :]
    for i in range(SSD_CONV - 1):
        off = SUBLANES - (SSD_CONV - 1) + i
        conv = conv + cbuf_ref[off:off + tile, :] * cw[i:i + 1, :]
    return _silu(conv)


def _ssd_scan_chunk(xs, bm, cm, dt, a, dskip, expand, st_ref, y_ref, row0, causal, tril_f):
    ac = _dot(tril_f, a, precision=HIGHEST)
    ac_t = ac.T
    dt_t = dt.T
    ea = jnp.exp(ac)
    a_last = ac[-1:, :]
    wgt = jnp.exp(a_last - ac) * dt
    e_last = _dot(jnp.exp(a_last), expand, precision=HIGHEST)
    for gi in range(SSD_GROUPS):
        ns = slice(gi * SSD_STATE, (gi + 1) * SSD_STATE)
        gs = slice(gi * SSD_GW, (gi + 1) * SSD_GW)
        bg = bm[:, ns].astype(BF16)
        cg = cm[:, ns].astype(BF16)
        cb = jnp.where(causal, _dot_nt(cg, bg), 0.0)
        st = st_ref[gi]
        y_int = _dot(cg, st.astype(BF16))
        xw = []
        for r in range(SSD_HPG):
            h = gi * SSD_HPG + r
            ps = slice(h * SSD_HEADDIM, (h + 1) * SSD_HEADDIM)
            seg = jnp.minimum(ac[:, h:h + 1] - ac_t[h:h + 1, :], 0.0)
            m = cb * jnp.exp(seg) * dt_t[h:h + 1, :]
            xh = xs[:, ps]
            yh = (_dot(m.astype(BF16), xh.astype(BF16))
                  + y_int[:, r * SSD_HEADDIM:(r + 1) * SSD_HEADDIM] * ea[:, h:h + 1]
                  + xh * dskip[:, ps])
            y_ref[row0:row0 + CHUNK, ps] = yh
            xw.append(xh * wgt[:, h:h + 1])
        xw = jnp.concatenate(xw, axis=1).astype(BF16)
        st_ref[gi] = st * e_last[:, gs] + _dot_tn(bg, xw)


def _ssd_tail(y, z, wn, wo, postw, x):
    y = y * _silu(z)
    parts = []
    for gi in range(SSD_GROUPS):
        gs = slice(gi * SSD_GW, (gi + 1) * SSD_GW)
        parts.append(_rms(y[:, gs], wn[:, gs]))
    yn = jnp.concatenate(parts, axis=1)
    out = _dot(yn.astype(BF16), wo)
    return x + _rms(out, postw)


def _ssd_prompt_kernel(x_ref, s0_ref, c0_ref, prew_ref, postw_ref, wz_ref, wx_ref, wdt_ref,
                       cw_ref, cb_ref, dtb_ref, alog_ref, dskip_ref, expand_ref, wn_ref, wo_ref,
                       y_ref, sout_ref, cout_ref, st_ref, cbuf_ref, ys_ref, *, n_pad, tile):
    t = pl.program_id(1)

    @pl.when(t == 0)
    def _():
        for gi in range(SSD_GROUPS):
            s0 = s0_ref[0, gi * SSD_HPG:(gi + 1) * SSD_HPG].reshape(SSD_GW, SSD_STATE)
            st_ref[gi] = s0.T
        cbuf_ref[0:SUBLANES, :] = c0_ref[0]

    x = x_ref[0]
    hn = _rms(x, prew_ref[...])
    row = lax.broadcasted_iota(jnp.int32, (tile, 1), 0) + t * tile
    if n_pad:
        hn = jnp.where(row >= n_pad, hn, 0.0)
    hb = hn.astype(BF16)
    z = _dot(hb, wz_ref[...])
    xbc = _dot(hb, wx_ref[...])
    dtr = _dot(hb, wdt_ref[...])
    xc = _ssd_conv(xbc, cbuf_ref, cw_ref[...], cb_ref[...], tile)
    cbuf_ref[0:SUBLANES, :] = cbuf_ref[tile:tile + SUBLANES, :]
    xs = xc[:, :SSD_DINNER]
    bm = xc[:, SSD_DINNER:SSD_DINNER + SSD_GROUPS * SSD_STATE]
    cm = xc[:, SSD_DINNER + SSD_GROUPS * SSD_STATE:]
    dt = _softplus(dtr + dtb_ref[...])
    if n_pad:
        dt = jnp.where(row >= n_pad, dt, 0.0)
    a = dt * (-jnp.exp(alog_ref[...]))

    causal = _tri(CHUNK)
    tril_f = causal.astype(F32)
    dskip = dskip_ref[...]
    expand = expand_ref[...]
    for c in range(tile // CHUNK):
        rows = slice(c * CHUNK, (c + 1) * CHUNK)
        _ssd_scan_chunk(xs[rows], bm[rows], cm[rows], dt[rows], a[rows], dskip, expand,
                        st_ref, ys_ref, c * CHUNK, causal, tril_f)
    y_ref[0] = _ssd_tail(ys_ref[...], z, wn_ref[...], wo_ref[...], postw_ref[...], x)

    @pl.when(t == pl.num_programs(1) - 1)
    def _():
        for gi in range(SSD_GROUPS):
            sout_ref[0, gi * SSD_HPG:(gi + 1) * SSD_HPG] = (
                st_ref[gi].T.reshape(SSD_HPG, SSD_HEADDIM, SSD_STATE))
        cout_ref[0] = cbuf_ref[0:SUBLANES, :]


def _ssd_prompt_layer(x, s0, c0, w, *, n_pad, tile):
    bsz, seq, _ = x.shape
    b0 = s0.shape[0]
    if b0 == bsz:
        s_idx = lambda b, t: (b, 0, 0, 0)
        c_idx = lambda b, t: (b, 0, 0)
    else:
        s_idx = lambda b, t: (0, 0, 0, 0)
        c_idx = lambda b, t: (0, 0, 0)
    weights = (w["pre"], w["post"], w["wz"], w["wx"], w["wdt"], w["cw"], w["cb"], w["dtb"],
               w["alog"], w["dskip"], w["expand"], w["wn"], w["wo"])
    return pl.pallas_call(
        functools.partial(_ssd_prompt_kernel, n_pad=n_pad, tile=tile),
        grid=(bsz, seq // tile),
        in_specs=[pl.BlockSpec((1, tile, D_MODEL), lambda b, t: (b, t, 0)),
                  pl.BlockSpec((1, SSD_HEADS, SSD_HEADDIM, SSD_STATE), s_idx),
                  pl.BlockSpec((1, SUBLANES, SSD_CONV_DIM), c_idx)]
                 + [_full(a.shape) for a in weights],
        out_specs=[pl.BlockSpec((1, tile, D_MODEL), lambda b, t: (b, t, 0)),
                   pl.BlockSpec((1, SSD_HEADS, SSD_HEADDIM, SSD_STATE), lambda b, t: (b, 0, 0, 0)),
                   pl.BlockSpec((1, SUBLANES, SSD_CONV_DIM), lambda b, t: (b, 0, 0))],
        out_shape=[jax.ShapeDtypeStruct(x.shape, F32),
                   jax.ShapeDtypeStruct((bsz, SSD_HEADS, SSD_HEADDIM, SSD_STATE), F32),
                   jax.ShapeDtypeStruct((bsz, SUBLANES, SSD_CONV_DIM), F32)],
        scratch_shapes=[pltpu.VMEM((SSD_GROUPS, SSD_STATE, SSD_GW), F32),
                        pltpu.VMEM((tile + SUBLANES, SSD_CONV_DIM), F32),
                        pltpu.VMEM((tile, SSD_DINNER), F32)],
        compiler_params=pltpu.CompilerParams(
            dimension_semantics=("arbitrary", "arbitrary"), vmem_limit_bytes=VMEM_LIMIT),
        name="ssd_prompt",
    )(x, s0, c0, *weights)


def _column(row, width, lane0=0):
    seg = row[:, lane0:lane0 + width]
    r = lax.broadcasted_iota(jnp.int32, (width, width), 0)
    c = lax.broadcasted_iota(jnp.int32, (width, width), 1)
    return jnp.sum(jnp.where(r == c, seg, 0.0), axis=1, keepdims=True)


def _gla_step_kernel(x_ref, s_ref, prew_ref, postw_ref, wq_ref, wk_ref, wv_ref, wr_ref,
                     wg_ref, wgu_ref, bg_ref, wn_ref, wo_ref, *rest, bb, has_prev):
    y_ref, sout_ref, q_s, k_s, v_s, e_s, o_s = rest[1:] if has_prev else rest
    i = pl.program_id(0)

    @pl.when(i == 0)
    def _():
        hb = _rms(x_ref[...], prew_ref[...]).astype(BF16)
        q_s[...] = _dot(hb, wq_ref[...]) * (GLA_HK ** -0.5)
        k_s[...] = _dot(hb, wk_ref[...])
        v_s[...] = _dot(hb, wv_ref[...])
        gl = _dot(hb, wg_ref[...])
        g = _log_sigmoid(_dot(gl.astype(BF16), wgu_ref[...]) + bg_ref[...]) * (1.0 / GLA_TAU)
        e_s[...] = jnp.exp(g)

    def body(j, carry):
        row = i * bb + j
        qr = q_s[pl.ds(row, 1), :]
        kr = k_s[pl.ds(row, 1), :]
        vr = v_s[pl.ds(row, 1), :]
        er = e_s[pl.ds(row, 1), :]
        for h in range(GLA_HEADS):
            vs = slice(h * GLA_HV, (h + 1) * GLA_HV)
            ecol = _column(er, GLA_HK, h * GLA_HK)
            kcol = _column(kr, GLA_HK, h * GLA_HK)
            qcol = _column(qr, GLA_HK, h * GLA_HK)
            s_new = s_ref[j, h] * ecol + kcol * vr[:, vs]
            sout_ref[j, h] = s_new
            o_s[pl.ds(row, 1), vs] = jnp.sum(qcol * s_new, axis=0, keepdims=True)
        return carry

    lax.fori_loop(0, bb, body, 0)

    @pl.when(i == pl.num_programs(0) - 1)
    def _():
        x = x_ref[...]
        hb = _rms(x, prew_ref[...]).astype(BF16)
        r = _dot(hb, wr_ref[...])
        y_ref[...] = _gla_tail(o_s[...], r, wn_ref[...], wo_ref[...], postw_ref[...], x)


def _layer_block(layer, bb, tail):
    nz = len(tail)
    return pl.BlockSpec((None, bb) + tail, lambda i, *_: (layer, i) + (0,) * nz)


def _gla_step_layer(x, states, prev_out, layer, w, *, bb=8):
    bsz = x.shape[0]
    weights = (w["pre"], w["post"], w["wq"], w["wk"], w["wv"], w["wr"], w["wg"], w["wgu"],
               w["bg"], w["wn"], w["wo"])
    sblk = _layer_block(layer, bb, (GLA_HEADS, GLA_HK, GLA_HV))
    carried = [] if prev_out is None else [prev_out]
    return pl.pallas_call(
        functools.partial(_gla_step_kernel, bb=bb, has_prev=prev_out is not None),
        grid=(bsz // bb,),
        in_specs=[_full(x.shape), sblk] + [_full(a.shape) for a in weights]
                 + [pl.BlockSpec(memory_space=pl.ANY)] * len(carried),
        out_specs=[pl.BlockSpec(x.shape, lambda i: (0, 0)), sblk],
        out_shape=[jax.ShapeDtypeStruct(x.shape, F32), jax.ShapeDtypeStruct(states.shape, F32)],
        scratch_shapes=[pltpu.VMEM((bsz, GLA_DK), F32), pltpu.VMEM((bsz, GLA_DK), F32),
                        pltpu.VMEM((bsz, GLA_DV), F32), pltpu.VMEM((bsz, GLA_DK), F32),
                        pltpu.VMEM((bsz, GLA_DV), F32)],
        input_output_aliases={2 + len(weights): 1} if carried else {},
        compiler_params=pltpu.CompilerParams(
            dimension_semantics=("arbitrary",), vmem_limit_bytes=VMEM_LIMIT),
        name="gla_step",
    )(x, states, *weights, *carried)


def _ssd_step_kernel(x_ref, s_ref, cv_ref, prew_ref, postw_ref, wz_ref, wx_ref, wdt_ref,
                     cw_ref, cb_ref, dtb_ref, alog_ref, dskip_ref, expand_ref, expand_n_ref,
                     wn_ref, wo_ref, *rest, bb, has_prev):
    (y_ref, sout_ref, cvout_ref, xs_s, xl_s, ct_s, b_s, e_s, yt_s) = rest[2:] if has_prev else rest
    i = pl.program_id(0)
    bsz = x_ref.shape[0]
    cd = SSD_CONV_DIM

    @pl.when(i == 0)
    def _():
        hb = _rms(x_ref[...], prew_ref[...]).astype(BF16)
        xbc = _dot(hb, wx_ref[...])
        dtr = _dot(hb, wdt_ref[...])
        cw = cw_ref[...]
        conv = cb_ref[...] + xbc * cw[3:4, :]
        for t in range(SSD_CONV - 1):
            conv = conv + cv_ref[:, t * cd:(t + 1) * cd] * cw[t:t + 1, :]
        cvout_ref[:, 0:cd] = cv_ref[:, cd:2 * cd]
        cvout_ref[:, cd:2 * cd] = cv_ref[:, 2 * cd:3 * cd]
        cvout_ref[:, 2 * cd:3 * cd] = xbc
        xc = _silu(conv)
        xs = xc[:, :SSD_DINNER]
        dt = _softplus(dtr + dtb_ref[...])
        ea = jnp.exp(dt * (-jnp.exp(alog_ref[...])))
        xs_s[...] = xs
        xd_t = (xs * _dot(dt, expand_ref[...], precision=HIGHEST)).T
        c_t = xc[:, SSD_DINNER + SSD_GROUPS * SSD_STATE:].T
        for p in range(SSD_GROUPS // 2):
            for e in range(2):
                gi = 2 * p + e
                xl_s[p, :, e * bsz:(e + 1) * bsz] = (
                    xd_t[gi * SSD_GW:(gi + 1) * SSD_GW].astype(BF16))
        ct_s[...] = c_t
        b_s[...] = xc[:, SSD_DINNER:SSD_DINNER + SSD_GROUPS * SSD_STATE]
        e_s[...] = _dot(ea, expand_n_ref[...], precision=HIGHEST)
        yt_s[...] = jnp.zeros(yt_s.shape, F32)

    sub = lax.broadcasted_iota(jnp.int32, (bsz, SSD_STATE), 0)
    lane = lax.broadcasted_iota(jnp.int32, (SSD_STATE, bsz), 1)
    zero_bn = jnp.zeros((bsz, SSD_STATE), BF16)
    zero_nb = jnp.zeros((SSD_STATE, bsz), BF16)

    for p in range(SSD_GROUPS // 2):
        y_acc = None
        for j in range(bb):
            row = i * bb + j
            er = e_s[pl.ds(row, 1), :]
            bsel, csel = [], []
            for e in range(2):
                ns = slice((2 * p + e) * SSD_STATE, (2 * p + e + 1) * SSD_STATE)
                bsel.append(jnp.where(sub == row, b_s[:, ns], 0.0).astype(BF16))
                csel.append(jnp.where(lane == row, ct_s[ns, :], 0.0).astype(BF16))
            rhs_b = jnp.concatenate([jnp.concatenate([bsel[0], zero_bn], axis=1),
                                     jnp.concatenate([zero_bn, bsel[1]], axis=1)], axis=0)
            rhs_c = jnp.concatenate([jnp.concatenate([csel[0], zero_nb], axis=1),
                                     jnp.concatenate([zero_nb, csel[1]], axis=1)], axis=0)
            outer = _dot(xl_s[p], rhs_b)
            halves = []
            for e in range(2):
                pieces = []
                for r in range(SSD_HPG):
                    h = (2 * p + e) * SSD_HPG + r
                    s_new = (s_ref[j, h] * er[:, h * SSD_STATE:(h + 1) * SSD_STATE]
                             + outer[r * SSD_HEADDIM:(r + 1) * SSD_HEADDIM,
                                     e * SSD_STATE:(e + 1) * SSD_STATE])
                    sout_ref[j, h] = s_new
                    pieces.append(s_new.astype(BF16))
                halves.append(jnp.concatenate(pieces, axis=0))
            y_j = _dot(jnp.concatenate(halves, axis=1), rhs_c)
            y_acc = y_j if y_acc is None else y_acc + y_j
        yt_s[p] += y_acc

    @pl.when(i == pl.num_programs(0) - 1)
    def _():
        x = x_ref[...]
        hb = _rms(x, prew_ref[...]).astype(BF16)
        z = _dot(hb, wz_ref[...])
        y = jnp.concatenate([yt_s[p, :, e * bsz:(e + 1) * bsz].T
                             for p in range(SSD_GROUPS // 2) for e in range(2)], axis=1)
        y = y + xs_s[...] * dskip_ref[...]
        y_ref[...] = _ssd_tail(y, z, wn_ref[...], wo_ref[...], postw_ref[...], x)


def _ssd_step_layer(x, states, convs, prev_out, layer, w, *, bb=4):
    bsz = x.shape[0]
    weights = (w["pre"], w["post"], w["wz"], w["wx"], w["wdt"], w["cw"], w["cb"], w["dtb"],
               w["alog"], w["dskip"], w["expand"], w["expand_n"], w["wn"], w["wo"])
    sblk = _layer_block(layer, bb, (SSD_HEADS, SSD_HEADDIM, SSD_STATE))
    cblk = pl.BlockSpec((None,) + convs.shape[1:], lambda i: (layer, 0, 0))
    carried = [] if prev_out is None else list(prev_out)
    n_in = 3 + len(weights)
    return pl.pallas_call(
        functools.partial(_ssd_step_kernel, bb=bb, has_prev=prev_out is not None),
        grid=(bsz // bb,),
        in_specs=[_full(x.shape), sblk,
                  pl.BlockSpec(cblk.block_shape, cblk.index_map, pipeline_mode=pl.Buffered(1))]
                 + [_full(a.shape) for a in weights]
                 + [pl.BlockSpec(memory_space=pl.ANY)] * len(carried),
        out_specs=[pl.BlockSpec(x.shape, lambda i: (0, 0)), sblk, cblk],
        out_shape=[jax.ShapeDtypeStruct(x.shape, F32), jax.ShapeDtypeStruct(states.shape, F32),
                   jax.ShapeDtypeStruct(convs.shape, F32)],
        input_output_aliases={n_in: 1, n_in + 1: 2} if carried else {},
        scratch_shapes=[pltpu.VMEM((bsz, SSD_DINNER), F32),
                        pltpu.VMEM((SSD_GROUPS // 2, SSD_GW, 2 * bsz), BF16),
                        pltpu.VMEM((SSD_GROUPS * SSD_STATE, bsz), F32),
                        pltpu.VMEM((bsz, SSD_GROUPS * SSD_STATE), F32),
                        pltpu.VMEM((bsz, SSD_HEADS * SSD_STATE), F32),
                        pltpu.VMEM((SSD_GROUPS // 2, SSD_GW, 2 * bsz), F32)],
        compiler_params=pltpu.CompilerParams(
            dimension_semantics=("arbitrary",), vmem_limit_bytes=VMEM_LIMIT),
        name="ssd_step",
    )(x, states, convs, *weights, *carried)


def _pad_lanes(a, n=LANES):
    return jnp.pad(a, ((0, 0), (0, n - a.shape[1])))


def _gla_weights(j, pre, post, w_in, w_gate_up, b_gate, w_norm, w_out):
    wi = w_in[j]
    o1, o2, o3, o4 = GLA_DK, 2 * GLA_DK, 2 * GLA_DK + GLA_DV, 2 * GLA_DK + 2 * GLA_DV
    return {
        "pre": pre[None, :], "post": post[None, :],
        "wq": wi[:, :o1].astype(BF16), "wk": wi[:, o1:o2].astype(BF16),
        "wv": wi[:, o2:o3].astype(BF16), "wr": wi[:, o3:o4].astype(BF16),
        "wg": _pad_lanes(wi[:, o4:]).astype(BF16),
        "wgu": jnp.pad(w_gate_up[j], ((0, LANES - GLA_RANK), (0, 0))).astype(BF16),
        "bg": b_gate[j][None, :], "wn": w_norm[j][None, :], "wo": w_out[j].astype(BF16),
    }


def _ssd_weights(j, pre, post, w_in, conv_w, conv_b, dt_bias, a_log, d_skip, w_norm, w_out):
    wi = w_in[j]
    o1, o2 = SSD_DINNER, SSD_DINNER + SSD_CONV_DIM
    head_of_lane = jnp.arange(SSD_DINNER) // SSD_HEADDIM
    expand = (jnp.arange(LANES)[:, None] == head_of_lane[None, :]).astype(F32)
    head_of_lane_n = jnp.arange(SSD_HEADS * SSD_STATE) // SSD_STATE
    expand_n = (jnp.arange(LANES)[:, None] == head_of_lane_n[None, :]).astype(F32)
    return {
        "pre": pre[None, :], "post": post[None, :],
        "wz": wi[:, :o1].astype(BF16), "wx": wi[:, o1:o2].astype(BF16),
        "wdt": _pad_lanes(wi[:, o2:]).astype(BF16),
        "cw": jnp.pad(conv_w[j], ((0, SUBLANES - SSD_CONV), (0, 0))), "cb": conv_b[j][None, :],
        "dtb": _pad_lanes(dt_bias[j][None, :]), "alog": _pad_lanes(a_log[j][None, :]),
        "dskip": jnp.repeat(d_skip[j], SSD_HEADDIM)[None, :],
        "expand": expand, "expand_n": expand_n,
        "wn": w_norm[j][None, :], "wo": w_out[j].astype(BF16),
    }


def _prompt_trunk(x, gla_s, ssm_s, conv_s, gw, sw, *, n_pad, tile):
    new_gla, new_ssm, new_conv = [], [], []
    for j in range(2):
        x, s = _gla_prompt_layer(x, gla_s[j], gw[j], n_pad=n_pad, tile=tile)
        new_gla.append(s)
        x, s, c = _ssd_prompt_layer(x, ssm_s[j], conv_s[j], sw[j], n_pad=n_pad, tile=tile)
        new_ssm.append(s)
        new_conv.append(c)
    return x, new_gla, new_ssm, new_conv


def kernel(x_prompt, x_sample, state_gla, state_ssm, state_conv, meta_tokens, pre_norm, post_norm,
           gla_w_in, gla_w_gate_up, gla_b_gate, gla_w_norm, gla_w_out,
           ssd_w_in, ssd_conv_w, ssd_conv_b, ssd_dt_bias, ssd_a_log, ssd_d_skip, ssd_w_norm, ssd_w_out):
    gw = [_gla_weights(j, pre_norm[2 * j], post_norm[2 * j], gla_w_in, gla_w_gate_up, gla_b_gate,
                       gla_w_norm, gla_w_out) for j in range(2)]
    sw = [_ssd_weights(j, pre_norm[2 * j + 1], post_norm[2 * j + 1], ssd_w_in, ssd_conv_w,
                       ssd_conv_b, ssd_dt_bias, ssd_a_log, ssd_d_skip, ssd_w_norm, ssd_w_out)
          for j in range(2)]
    bsz = x_prompt.shape[0]

    n_pad = CHUNK - N_META
    x_meta = jnp.pad(meta_tokens.astype(F32), ((n_pad, 0), (0, 0)))[None]
    zg = jnp.zeros((1, GLA_HEADS, GLA_HK, GLA_HV), F32)
    zs = jnp.zeros((1, SSD_HEADS, SSD_HEADDIM, SSD_STATE), F32)
    zc = jnp.zeros((1, SUBLANES, SSD_CONV_DIM), F32)
    _, mg, ms, mc = _prompt_trunk(x_meta, [zg, zg], [zs, zs], [zc, zc], gw, sw,
                                  n_pad=n_pad, tile=CHUNK)
    y_prompt, pg, ps, pc = _prompt_trunk(x_prompt, mg, ms, mc, gw, sw, n_pad=0, tile=256)
    gla_p = jnp.stack(pg)
    ssm_p = jnp.stack(ps)
    conv_p = jnp.stack(pc)[:, :, SUBLANES - (SSD_CONV - 1):, :]

    xs = x_sample[:, 0, :]
    sbsz = xs.shape[0]
    convs = state_conv.reshape(2, sbsz, (SSD_CONV - 1) * SSD_CONV_DIM)
    gla_s = ssm_conv_s = None
    for j in range(2):
        xs, gla_s = _gla_step_layer(xs, state_gla, gla_s, j, gw[j])
        xs, *ssm_conv_s = _ssd_step_layer(xs, state_ssm, convs, ssm_conv_s, j, sw[j])
    ssm_s, conv_s = ssm_conv_s
    y_sample = xs[:, None, :]
    return (y_prompt, y_sample, gla_p, ssm_p, conv_p,
            gla_s, ssm_s, conv_s.reshape(2, sbsz, SSD_CONV - 1, SSD_CONV_DIM))
```

```python
import functools

import jax
import jax.numpy as jnp
from jax import lax
from jax.experimental import pallas as pl
from jax.experimental.pallas import tpu as pltpu

F32 = jnp.float32
BF16 = jnp.bfloat16

D_MODEL = 1024
N_META = 16
NORM_EPS = 1e-6

GLA_HEADS = 4
GLA_DK = 512
GLA_DV = 1024
GLA_HK = 128
GLA_HV = 256
GLA_RANK = 16
GLA_TAU = 16.0
GLA_SAFE_DROP = 40.0

SSD_DINNER = 2048
SSD_HEADDIM = 64
SSD_HEADS = 32
SSD_GROUPS = 4
SSD_HPG = 8
SSD_STATE = 128
SSD_CONV = 4
SSD_CONV_DIM = 3072
SSD_GW = SSD_HPG * SSD_HEADDIM

LANES = 128
SUBLANES = 8
CHUNK = 128
GLA_TILE = 512
SSD_TILE = 256
VMEM_LIMIT = 56 * 1024 * 1024

HIGHEST = lax.Precision.HIGHEST


def _dot(a, b, precision=None):
    return jnp.dot(a, b, preferred_element_type=F32, precision=precision)


def _dot_nt(a, b):
    return lax.dot_general(a, b, (((1,), (1,)), ((), ())), preferred_element_type=F32)


def _dot_tn(a, b):
    return lax.dot_general(a, b, (((0,), (0,)), ((), ())), preferred_element_type=F32)


def _rms(x, w):
    return x * lax.rsqrt(jnp.mean(x * x, axis=-1, keepdims=True) + NORM_EPS) * w


def _silu(x):
    h = 0.5 * x
    return h + h * jnp.tanh(h)


def _softplus(x):
    return jnp.maximum(x, 0.0) + jnp.log1p(jnp.exp(-jnp.abs(x)))


def _log_sigmoid(x):
    return -_softplus(-x)


def _tri(n):
    r = lax.broadcasted_iota(jnp.int32, (n, n), 0)
    c = lax.broadcasted_iota(jnp.int32, (n, n), 1)
    return r >= c


def _split3(x):
    hi = x.astype(BF16)
    r1 = x - hi.astype(F32)
    mid = r1.astype(BF16)
    lo = (r1 - mid.astype(F32)).astype(BF16)
    return hi, mid, lo


def _dot_exact_rhs(x, m_bf16):
    hi, mid, lo = _split3(x)
    return _dot(hi, m_bf16) + _dot(mid, m_bf16) + _dot(lo, m_bf16)


def _cumsum_rows(tril_bf16, x):
    hi, mid, lo = _split3(x)
    return _dot(tril_bf16, hi) + _dot(tril_bf16, mid) + _dot(tril_bf16, lo)


def _gla_attn_pairwise(qf, kf, bc, causal, kbuf_ref, bbuf_ref):
    n = qf.shape[0]
    kbuf_ref[...] = kf
    bbuf_ref[...] = bc
    lane = lax.broadcasted_iota(jnp.int32, (n, n), 1)

    def body(j, acc):
        kj = kbuf_ref[pl.ds(j, 1), :]
        bj = bbuf_ref[pl.ds(j, 1), :]
        col = jnp.sum(qf * jnp.exp(jnp.minimum(bc - bj, 0.0)) * kj, axis=1, keepdims=True)
        return jnp.where(lane == j, col, acc)

    acc = lax.fori_loop(0, n, body, jnp.zeros((n, n), F32))
    return jnp.where(causal, acc, 0.0)


def _gla_scan_tile(q, k, v, g, states, pairwise, kbuf_ref, bbuf_ref):
    causal = _tri(CHUNK)
    tril = causal.astype(BF16)
    states = list(states)
    outs = []
    for c in range(q.shape[0] // CHUNK):
        rows = slice(c * CHUNK, (c + 1) * CHUNK)
        bc = _cumsum_rows(tril, g[rows])
        b_last = bc[-1:, :]
        e_last = jnp.exp(b_last)
        qh = (q[rows] * jnp.exp(bc)).astype(BF16)
        if pairwise:
            kd = (k[rows] * jnp.exp(b_last - bc)).astype(BF16)
        else:
            kh = k[rows] * jnp.exp(-bc)
            kd = (kh * e_last).astype(BF16)
            kh = kh.astype(BF16)
        vb = v[rows].astype(BF16)
        parts = []
        for h in range(GLA_HEADS):
            ks = slice(h * GLA_HK, (h + 1) * GLA_HK)
            vs = slice(h * GLA_HV, (h + 1) * GLA_HV)
            if pairwise:
                attn = _gla_attn_pairwise(q[rows, ks], k[rows, ks], bc[:, ks], causal,
                                          kbuf_ref, bbuf_ref)
            else:
                attn = jnp.where(causal, _dot_nt(qh[:, ks], kh[:, ks]), 0.0)
            st = states[h]
            parts.append(_dot(attn.astype(BF16), vb[:, vs]) + _dot_nt(qh[:, ks], st.astype(BF16)))
            states[h] = st * e_last[:, ks] + _dot_tn(vb[:, vs], kd[:, ks])
        outs.append(jnp.concatenate(parts, axis=1))
    o = jnp.concatenate(outs, axis=0) if len(outs) > 1 else outs[0]
    return o, states


def _gla_tail(o, r, wn, wo, postw, x):
    parts = []
    for h in range(GLA_HEADS):
        vs = slice(h * GLA_HV, (h + 1) * GLA_HV)
        parts.append(_rms(o[:, vs], wn[:, vs]))
    on = jnp.concatenate(parts, axis=1) * _silu(r)
    y = _dot(on.astype(BF16), wo)
    return x + _rms(y, postw)


def _gla_prompt_kernel(x_ref, s0_ref, prew_ref, postw_ref, wq_ref, wk_ref, wv_ref, wr_ref,
                       wg_ref, wgu_ref, bg_ref, wn_ref, wo_ref,
                       y_ref, sout_ref, st_ref, o_ref, kbuf_ref, bbuf_ref, *, n_pad, tile):
    t = pl.program_id(1)

    @pl.when(t == 0)
    def _():
        for h in range(GLA_HEADS):
            st_ref[h] = s0_ref[0, h].T

    x = x_ref[0]
    hn = _rms(x, prew_ref[...])
    if n_pad:
        row = lax.broadcasted_iota(jnp.int32, (tile, 1), 0) + t * tile
        hn = jnp.where(row >= n_pad, hn, 0.0)
    hb = hn.astype(BF16)
    q = _dot(hb, wq_ref[...]) * (GLA_HK ** -0.5)
    k = _dot(hb, wk_ref[...])
    v = _dot(hb, wv_ref[...])
    r = _dot(hb, wr_ref[...])
    gl = _dot(hb, wg_ref[...])
    g = _log_sigmoid(_dot(gl.astype(BF16), wgu_ref[...]) + bg_ref[...]) * (1.0 / GLA_TAU)

    drop = None
    for c in range(tile // CHUNK):
        tot = jnp.sum(g[c * CHUNK:(c + 1) * CHUNK], axis=0, keepdims=True)
        drop = tot if drop is None else jnp.minimum(drop, tot)
    safe = jnp.min(drop) >= -GLA_SAFE_DROP

    def scan(pairwise):
        o, states = _gla_scan_tile(q, k, v, g, [st_ref[h] for h in range(GLA_HEADS)], pairwise,
                                   kbuf_ref, bbuf_ref)
        o_ref[...] = o
        for h in range(GLA_HEADS):
            st_ref[h] = states[h]

    pl.when(safe)(functools.partial(scan, False))
    pl.when(jnp.logical_not(safe))(functools.partial(scan, True))
    y_ref[0] = _gla_tail(o_ref[...], r, wn_ref[...], wo_ref[...], postw_ref[...], x)

    @pl.when(t == pl.num_programs(1) - 1)
    def _():
        for h in range(GLA_HEADS):
            sout_ref[0, h] = st_ref[h].T


def _full(shape):
    nd = len(shape)
    return pl.BlockSpec(shape, lambda *_: (0,) * nd, pipeline_mode=pl.Buffered(1))


def _gla_prompt_layer(x, s0, w, *, n_pad, tile):
    bsz, seq, _ = x.shape
    b0 = s0.shape[0]
    s_idx = (lambda b, t: (b, 0, 0, 0)) if b0 == bsz else (lambda b, t: (0, 0, 0, 0))
    weights = (w["pre"], w["post"], w["wq"], w["wk"], w["wv"], w["wr"], w["wg"], w["wgu"],
               w["bg"], w["wn"], w["wo"])
    return pl.pallas_call(
        functools.partial(_gla_prompt_kernel, n_pad=n_pad, tile=tile),
        grid=(bsz, seq // tile),
        in_specs=[pl.BlockSpec((1, tile, D_MODEL), lambda b, t: (b, t, 0)),
                  pl.BlockSpec((1, GLA_HEADS, GLA_HK, GLA_HV), s_idx)]
                 + [_full(a.shape) for a in weights],
        out_specs=[pl.BlockSpec((1, tile, D_MODEL), lambda b, t: (b, t, 0)),
                   pl.BlockSpec((1, GLA_HEADS, GLA_HK, GLA_HV), lambda b, t: (b, 0, 0, 0))],
        out_shape=[jax.ShapeDtypeStruct(x.shape, F32),
                   jax.ShapeDtypeStruct((bsz, GLA_HEADS, GLA_HK, GLA_HV), F32)],
        scratch_shapes=[pltpu.VMEM((GLA_HEADS, GLA_HV, GLA_HK), F32),
                        pltpu.VMEM((tile, GLA_DV), F32),
                        pltpu.VMEM((CHUNK, GLA_HK), F32),
                        pltpu.VMEM((CHUNK, GLA_HK), F32)],
        compiler_params=pltpu.CompilerParams(
            dimension_semantics=("arbitrary", "arbitrary"), vmem_limit_bytes=VMEM_LIMIT),
        name="gla_prompt",
    )(x, s0, *weights)


def _ssd_conv(xbc, cbuf_ref, cw, cb, tile):
    cbuf_ref[SUBLANES:SUBLANES + tile, :] = xbc
    conv = cb + xbc * cw[3:4, :]
    for i in range(SSD_CONV - 1):
        off = SUBLANES - (SSD_CONV - 1) + i
        conv = conv + cbuf_ref[off:off + tile, :] * cw[i:i + 1, :]
    return _silu(conv)


def _ssd_scan_chunk(xs, bm, cm, dt, a, expand, st_ref, y_ref, row0, causal, tril, lo_half):
    ac = _cumsum_rows(tril, a)
    ac_t = ac.T
    dt_t = dt.T
    a_last = ac[-1:, :]
    wgt = jnp.exp(a_last - ac) * dt
    xw = (xs * _dot(wgt.astype(BF16), expand)).astype(BF16)
    e_last = _dot_exact_rhs(jnp.exp(a_last), expand)
    pair = 2 * SSD_HEADDIM
    for gi in range(SSD_GROUPS):
        ns = slice(gi * SSD_STATE, (gi + 1) * SSD_STATE)
        gs = slice(gi * SSD_GW, (gi + 1) * SSD_GW)
        bg = bm[:, ns].astype(BF16)
        cg = cm[:, ns].astype(BF16)
        cb = _dot_nt(cg, bg)
        st = st_ref[gi]
        y_int = _dot(cg, st.astype(BF16))
        for qi in range(SSD_HPG // 2):
            h0 = gi * SSD_HPG + 2 * qi
            ps = slice(h0 * SSD_HEADDIM, h0 * SSD_HEADDIM + pair)
            ms, eas = [], []
            for e in range(2):
                h = h0 + e
                col = jnp.broadcast_to(ac[:, h:h + 1], (CHUNK, CHUNK))
                seg = col - ac_t[h:h + 1, :]
                ms.append(jnp.where(causal, cb * jnp.exp(seg) * dt_t[h:h + 1, :], 0.0).astype(BF16))
                eas.append(jnp.exp(col))
            xp = xs[:, ps]
            rhs = jnp.concatenate([jnp.where(lo_half, xp, 0.0).astype(BF16),
                                   jnp.where(lo_half, 0.0, xp).astype(BF16)], axis=0)
            ea = jnp.where(lo_half, eas[0], eas[1])
            y_ref[row0:row0 + CHUNK, ps] = (_dot(jnp.concatenate(ms, axis=1), rhs)
                                            + y_int[:, qi * pair:(qi + 1) * pair] * ea)
        st_ref[gi] = st * e_last[:, gs] + _dot_tn(bg, xw[:, gs])


def _ssd_tail(y, z, wn, wo, postw, x):
    y = y * _silu(z)
    parts = []
    for gi in range(SSD_GROUPS):
        gs = slice(gi * SSD_GW, (gi + 1) * SSD_GW)
        parts.append(_rms(y[:, gs], wn[:, gs]))
    yn = jnp.concatenate(parts, axis=1)
    out = _dot(yn.astype(BF16), wo)
    return x + _rms(out, postw)


def _ssd_prompt_kernel(x_ref, s0_ref, c0_ref, prew_ref, postw_ref, wz_ref, wx_ref, wdt_ref,
                       cw_ref, cb_ref, dtb_ref, alog_ref, dskip_ref, expand_ref, wn_ref, wo_ref,
                       y_ref, sout_ref, cout_ref, st_ref, cbuf_ref, ys_ref, *, n_pad, tile):
    t = pl.program_id(1)

    @pl.when(t == 0)
    def _():
        for gi in range(SSD_GROUPS):
            s0 = s0_ref[0, gi * SSD_HPG:(gi + 1) * SSD_HPG].reshape(SSD_GW, SSD_STATE)
            st_ref[gi] = s0.T
        cbuf_ref[0:SUBLANES, :] = c0_ref[0]

    x = x_ref[0]
    hn = _rms(x, prew_ref[...])
    row = lax.broadcasted_iota(jnp.int32, (tile, 1), 0) + t * tile
    if n_pad:
        hn = jnp.where(row >= n_pad, hn, 0.0)
    hb = hn.astype(BF16)
    z = _dot(hb, wz_ref[...])
    xbc = _dot(hb, wx_ref[...])
    dtr = _dot(hb, wdt_ref[...])
    xc = _ssd_conv(xbc, cbuf_ref, cw_ref[...], cb_ref[...], tile)
    cbuf_ref[0:SUBLANES, :] = cbuf_ref[tile:tile + SUBLANES, :]
    xs = xc[:, :SSD_DINNER]
    bm = xc[:, SSD_DINNER:SSD_DINNER + SSD_GROUPS * SSD_STATE]
    cm = xc[:, SSD_DINNER + SSD_GROUPS * SSD_STATE:]
    dt = _softplus(dtr + dtb_ref[...])
    if n_pad:
        dt = jnp.where(row >= n_pad, dt, 0.0)
    a = dt * (-jnp.exp(alog_ref[...]))

    causal = _tri(CHUNK)
    tril = causal.astype(BF16)
    lo_half = lax.broadcasted_iota(jnp.int32, (CHUNK, 2 * SSD_HEADDIM), 1) < SSD_HEADDIM
    expand = expand_ref[...]
    for c in range(tile // CHUNK):
        rows = slice(c * CHUNK, (c + 1) * CHUNK)
        _ssd_scan_chunk(xs[rows], bm[rows], cm[rows], dt[rows], a[rows], expand,
                        st_ref, ys_ref, c * CHUNK, causal, tril, lo_half)
    y = ys_ref[...] + xs * dskip_ref[...]
    y_ref[0] = _ssd_tail(y, z, wn_ref[...], wo_ref[...], postw_ref[...], x)

    @pl.when(t == pl.num_programs(1) - 1)
    def _():
        for gi in range(SSD_GROUPS):
            sout_ref[0, gi * SSD_HPG:(gi + 1) * SSD_HPG] = (
                st_ref[gi].T.reshape(SSD_HPG, SSD_HEADDIM, SSD_STATE))
        cout_ref[0] = cbuf_ref[0:SUBLANES, :]


def _ssd_prompt_layer(x, s0, c0, w, *, n_pad, tile):
    bsz, seq, _ = x.shape
    b0 = s0.shape[0]
    if b0 == bsz:
        s_idx = lambda b, t: (b, 0, 0, 0)
        c_idx = lambda b, t: (b, 0, 0)
    else:
        s_idx = lambda b, t: (0, 0, 0, 0)
        c_idx = lambda b, t: (0, 0, 0)
    weights = (w["pre"], w["post"], w["wz"], w["wx"], w["wdt"], w["cw"], w["cb"], w["dtb"],
               w["alog"], w["dskip"], w["expand"], w["wn"], w["wo"])
    return pl.pallas_call(
        functools.partial(_ssd_prompt_kernel, n_pad=n_pad, tile=tile),
        grid=(bsz, seq // tile),
        in_specs=[pl.BlockSpec((1, tile, D_MODEL), lambda b, t: (b, t, 0)),
                  pl.BlockSpec((1, SSD_HEADS, SSD_HEADDIM, SSD_STATE), s_idx),
                  pl.BlockSpec((1, SUBLANES, SSD_CONV_DIM), c_idx)]
                 + [_full(a.shape) for a in weights],
        out_specs=[pl.BlockSpec((1, tile, D_MODEL), lambda b, t: (b, t, 0)),
                   pl.BlockSpec((1, SSD_HEADS, SSD_HEADDIM, SSD_STATE), lambda b, t: (b, 0, 0, 0)),
                   pl.BlockSpec((1, SUBLANES, SSD_CONV_DIM), lambda b, t: (b, 0, 0))],
        out_shape=[jax.ShapeDtypeStruct(x.shape, F32),
                   jax.ShapeDtypeStruct((bsz, SSD_HEADS, SSD_HEADDIM, SSD_STATE), F32),
                   jax.ShapeDtypeStruct((bsz, SUBLANES, SSD_CONV_DIM), F32)],
        scratch_shapes=[pltpu.VMEM((SSD_GROUPS, SSD_STATE, SSD_GW), F32),
                        pltpu.VMEM((tile + SUBLANES, SSD_CONV_DIM), F32),
                        pltpu.VMEM((tile, SSD_DINNER), F32)],
        compiler_params=pltpu.CompilerParams(
            dimension_semantics=("arbitrary", "arbitrary"), vmem_limit_bytes=VMEM_LIMIT),
        name="ssd_prompt",
    )(x, s0, c0, *weights)


def _column(row, width, lane0=0):
    seg = row[:, lane0:lane0 + width]
    r = lax.broadcasted_iota(jnp.int32, (width, width), 0)
    c = lax.broadcasted_iota(jnp.int32, (width, width), 1)
    return jnp.sum(jnp.where(r == c, seg, 0.0), axis=1, keepdims=True)


def _gla_step_kernel(x_ref, s_ref, prew_ref, postw_ref, wq_ref, wk_ref, wv_ref, wr_ref,
                     wg_ref, wgu_ref, bg_ref, wn_ref, wo_ref, *rest, bb, has_prev):
    y_ref, sout_ref, q_s, k_s, v_s, e_s, o_s = rest[1:] if has_prev else rest
    i = pl.program_id(0)

    @pl.when(i == 0)
    def _():
        hb = _rms(x_ref[...], prew_ref[...]).astype(BF16)
        q_s[...] = _dot(hb, wq_ref[...]) * (GLA_HK ** -0.5)
        k_s[...] = _dot(hb, wk_ref[...])
        v_s[...] = _dot(hb, wv_ref[...])
        gl = _dot(hb, wg_ref[...])
        g = _log_sigmoid(_dot(gl.astype(BF16), wgu_ref[...]) + bg_ref[...]) * (1.0 / GLA_TAU)
        e_s[...] = jnp.exp(g)

    def body(j, carry):
        row = i * bb + j
        qr = q_s[pl.ds(row, 1), :]
        kr = k_s[pl.ds(row, 1), :]
        vr = v_s[pl.ds(row, 1), :]
        er = e_s[pl.ds(row, 1), :]
        for h in range(GLA_HEADS):
            vs = slice(h * GLA_HV, (h + 1) * GLA_HV)
            ecol = _column(er, GLA_HK, h * GLA_HK)
            kcol = _column(kr, GLA_HK, h * GLA_HK)
            qcol = _column(qr, GLA_HK, h * GLA_HK)
            s_new = s_ref[j, h] * ecol + kcol * vr[:, vs]
            sout_ref[j, h] = s_new
            o_s[pl.ds(row, 1), vs] = jnp.sum(qcol * s_new, axis=0, keepdims=True)
        return carry

    lax.fori_loop(0, bb, body, 0)

    @pl.when(i == pl.num_programs(0) - 1)
    def _():
        x = x_ref[...]
        hb = _rms(x, prew_ref[...]).astype(BF16)
        r = _dot(hb, wr_ref[...])
        y_ref[...] = _gla_tail(o_s[...], r, wn_ref[...], wo_ref[...], postw_ref[...], x)


def _layer_block(layer, bb, tail):
    nz = len(tail)
    return pl.BlockSpec((None, bb) + tail, lambda i, *_: (layer, i) + (0,) * nz)


def _gla_step_layer(x, states, prev_out, layer, w, *, bb=8):
    bsz = x.shape[0]
    weights = (w["pre"], w["post"], w["wq"], w["wk"], w["wv"], w["wr"], w["wg"], w["wgu"],
               w["bg"], w["wn"], w["wo"])
    sblk = _layer_block(layer, bb, (GLA_HEADS, GLA_HK, GLA_HV))
    carried = [] if prev_out is None else [prev_out]
    return pl.pallas_call(
        functools.partial(_gla_step_kernel, bb=bb, has_prev=prev_out is not None),
        grid=(bsz // bb,),
        in_specs=[_full(x.shape), sblk] + [_full(a.shape) for a in weights]
                 + [pl.BlockSpec(memory_space=pl.ANY)] * len(carried),
        out_specs=[pl.BlockSpec(x.shape, lambda i: (0, 0)), sblk],
        out_shape=[jax.ShapeDtypeStruct(x.shape, F32), jax.ShapeDtypeStruct(states.shape, F32)],
        scratch_shapes=[pltpu.VMEM((bsz, GLA_DK), F32), pltpu.VMEM((bsz, GLA_DK), F32),
                        pltpu.VMEM((bsz, GLA_DV), F32), pltpu.VMEM((bsz, GLA_DK), F32),
                        pltpu.VMEM((bsz, GLA_DV), F32)],
        input_output_aliases={2 + len(weights): 1} if carried else {},
        compiler_params=pltpu.CompilerParams(
            dimension_semantics=("arbitrary",), vmem_limit_bytes=VMEM_LIMIT),
        name="gla_step",
    )(x, states, *weights, *carried)


def _ssd_step_kernel(x_ref, s_ref, cv_ref, prew_ref, postw_ref, wz_ref, wx_ref, wdt_ref,
                     cw_ref, cb_ref, dtb_ref, alog_ref, dskip_ref, expand_ref, expand_n_ref,
                     wn_ref, wo_ref, *rest, bb, has_prev):
    (y_ref, sout_ref, cvout_ref, xs_s, xl_s, ct_s, b_s, e_s, yt_s) = rest[2:] if has_prev else rest
    i = pl.program_id(0)
    bsz = x_ref.shape[0]
    cd = SSD_CONV_DIM

    @pl.when(i == 0)
    def _():
        hb = _rms(x_ref[...], prew_ref[...]).astype(BF16)
        xbc = _dot(hb, wx_ref[...])
        dtr = _dot(hb, wdt_ref[...])
        cw = cw_ref[...]
        conv = cb_ref[...] + xbc * cw[3:4, :]
        for t in range(SSD_CONV - 1):
            conv = conv + cv_ref[:, t * cd:(t + 1) * cd] * cw[t:t + 1, :]
        cvout_ref[:, 0:cd] = cv_ref[:, cd:2 * cd]
        cvout_ref[:, cd:2 * cd] = cv_ref[:, 2 * cd:3 * cd]
        cvout_ref[:, 2 * cd:3 * cd] = xbc
        xc = _silu(conv)
        xs = xc[:, :SSD_DINNER]
        dt = _softplus(dtr + dtb_ref[...])
        ea = jnp.exp(dt * (-jnp.exp(alog_ref[...])))
        xs_s[...] = xs
        xd_t = (xs * _dot_exact_rhs(dt, expand_ref[...])).T
        c_t = xc[:, SSD_DINNER + SSD_GROUPS * SSD_STATE:].T
        for p in range(SSD_GROUPS // 2):
            for e in range(2):
                gi = 2 * p + e
                xl_s[p, :, e * bsz:(e + 1) * bsz] = (
                    xd_t[gi * SSD_GW:(gi + 1) * SSD_GW].astype(BF16))
        ct_s[...] = c_t
        b_s[...] = xc[:, SSD_DINNER:SSD_DINNER + SSD_GROUPS * SSD_STATE]
        e_s[...] = _dot_exact_rhs(ea, expand_n_ref[...])
        yt_s[...] = jnp.zeros(yt_s.shape, F32)

    sub = lax.broadcasted_iota(jnp.int32, (bsz, SSD_STATE), 0)
    lane = lax.broadcasted_iota(jnp.int32, (SSD_STATE, bsz), 1)
    zero_bn = jnp.zeros((bsz, SSD_STATE), BF16)
    zero_nb = jnp.zeros((SSD_STATE, bsz), BF16)

    for p in range(SSD_GROUPS // 2):
        y_acc = None
        for j in range(bb):
            row = i * bb + j
            er = e_s[pl.ds(row, 1), :]
            bsel, csel = [], []
            for e in range(2):
                ns = slice((2 * p + e) * SSD_STATE, (2 * p + e + 1) * SSD_STATE)
                bsel.append(jnp.where(sub == row, b_s[:, ns], 0.0).astype(BF16))
                csel.append(jnp.where(lane == row, ct_s[ns, :], 0.0).astype(BF16))
            rhs_b = jnp.concatenate([jnp.concatenate([bsel[0], zero_bn], axis=1),
                                     jnp.concatenate([zero_bn, bsel[1]], axis=1)], axis=0)
            rhs_c = jnp.concatenate([jnp.concatenate([csel[0], zero_nb], axis=1),
                                     jnp.concatenate([zero_nb, csel[1]], axis=1)], axis=0)
            outer = _dot(xl_s[p], rhs_b)
            halves = []
            for e in range(2):
                pieces = []
                for r in range(SSD_HPG):
                    h = (2 * p + e) * SSD_HPG + r
                    s_new = (s_ref[j, h] * er[:, h * SSD_STATE:(h + 1) * SSD_STATE]
                             + outer[r * SSD_HEADDIM:(r + 1) * SSD_HEADDIM,
                                     e * SSD_STATE:(e + 1) * SSD_STATE])
                    sout_ref[j, h] = s_new
                    pieces.append(s_new.astype(BF16))
                halves.append(jnp.concatenate(pieces, axis=0))
            y_j = _dot(jnp.concatenate(halves, axis=1), rhs_c)
            y_acc = y_j if y_acc is None else y_acc + y_j
        yt_s[p] += y_acc

    @pl.when(i == pl.num_programs(0) - 1)
    def _():
        x = x_ref[...]
        hb = _rms(x, prew_ref[...]).astype(BF16)
        z = _dot(hb, wz_ref[...])
        y = jnp.concatenate([yt_s[p, :, e * bsz:(e + 1) * bsz].T
                             for p in range(SSD_GROUPS // 2) for e in range(2)], axis=1)
        y = y + xs_s[...] * dskip_ref[...]
        y_ref[...] = _ssd_tail(y, z, wn_ref[...], wo_ref[...], postw_ref[...], x)


def _ssd_step_layer(x, states, convs, prev_out, layer, w, *, bb=4):
    bsz = x.shape[0]
    weights = (w["pre"], w["post"], w["wz"], w["wx"], w["wdt"], w["cw"], w["cb"], w["dtb"],
               w["alog"], w["dskip"], w["expand"], w["expand_n"], w["wn"], w["wo"])
    sblk = _layer_block(layer, bb, (SSD_HEADS, SSD_HEADDIM, SSD_STATE))
    cblk = pl.BlockSpec((None,) + convs.shape[1:], lambda i: (layer, 0, 0))
    carried = [] if prev_out is None else list(prev_out)
    n_in = 3 + len(weights)
    return pl.pallas_call(
        functools.partial(_ssd_step_kernel, bb=bb, has_prev=prev_out is not None),
        grid=(bsz // bb,),
        in_specs=[_full(x.shape), sblk,
                  pl.BlockSpec(cblk.block_shape, cblk.index_map, pipeline_mode=pl.Buffered(1))]
                 + [_full(a.shape) for a in weights]
                 + [pl.BlockSpec(memory_space=pl.ANY)] * len(carried),
        out_specs=[pl.BlockSpec(x.shape, lambda i: (0, 0)), sblk, cblk],
        out_shape=[jax.ShapeDtypeStruct(x.shape, F32), jax.ShapeDtypeStruct(states.shape, F32),
                   jax.ShapeDtypeStruct(convs.shape, F32)],
        input_output_aliases={n_in: 1, n_in + 1: 2} if carried else {},
        scratch_shapes=[pltpu.VMEM((bsz, SSD_DINNER), F32),
                        pltpu.VMEM((SSD_GROUPS // 2, SSD_GW, 2 * bsz), BF16),
                        pltpu.VMEM((SSD_GROUPS * SSD_STATE, bsz), F32),
                        pltpu.VMEM((bsz, SSD_GROUPS * SSD_STATE), F32),
                        pltpu.VMEM((bsz, SSD_HEADS * SSD_STATE), F32),
                        pltpu.VMEM((SSD_GROUPS // 2, SSD_GW, 2 * bsz), F32)],
        compiler_params=pltpu.CompilerParams(
            dimension_semantics=("arbitrary",), vmem_limit_bytes=VMEM_LIMIT),
        name="ssd_step",
    )(x, states, convs, *weights, *carried)


def _pad_lanes(a, n=LANES):
    return jnp.pad(a, ((0, 0), (0, n - a.shape[1])))


def _gla_weights(j, pre, post, w_in, w_gate_up, b_gate, w_norm, w_out):
    wi = w_in[j]
    o1, o2, o3, o4 = GLA_DK, 2 * GLA_DK, 2 * GLA_DK + GLA_DV, 2 * GLA_DK + 2 * GLA_DV
    return {
        "pre": pre[None, :], "post": post[None, :],
        "wq": wi[:, :o1].astype(BF16), "wk": wi[:, o1:o2].astype(BF16),
        "wv": wi[:, o2:o3].astype(BF16), "wr": wi[:, o3:o4].astype(BF16),
        "wg": _pad_lanes(wi[:, o4:]).astype(BF16),
        "wgu": jnp.pad(w_gate_up[j], ((0, LANES - GLA_RANK), (0, 0))).astype(BF16),
        "bg": b_gate[j][None, :], "wn": w_norm[j][None, :], "wo": w_out[j].astype(BF16),
    }


def _ssd_weights(j, pre, post, w_in, conv_w, conv_b, dt_bias, a_log, d_skip, w_norm, w_out):
    wi = w_in[j]
    o1, o2 = SSD_DINNER, SSD_DINNER + SSD_CONV_DIM
    head_of_lane = jnp.arange(SSD_DINNER) // SSD_HEADDIM
    expand = (jnp.arange(LANES)[:, None] == head_of_lane[None, :]).astype(BF16)
    head_of_lane_n = jnp.arange(SSD_HEADS * SSD_STATE) // SSD_STATE
    expand_n = (jnp.arange(LANES)[:, None] == head_of_lane_n[None, :]).astype(BF16)
    return {
        "pre": pre[None, :], "post": post[None, :],
        "wz": wi[:, :o1].astype(BF16), "wx": wi[:, o1:o2].astype(BF16),
        "wdt": _pad_lanes(wi[:, o2:]).astype(BF16),
        "cw": jnp.pad(conv_w[j], ((0, SUBLANES - SSD_CONV), (0, 0))), "cb": conv_b[j][None, :],
        "dtb": _pad_lanes(dt_bias[j][None, :]), "alog": _pad_lanes(a_log[j][None, :]),
        "dskip": jnp.repeat(d_skip[j], SSD_HEADDIM)[None, :],
        "expand": expand, "expand_n": expand_n,
        "wn": w_norm[j][None, :], "wo": w_out[j].astype(BF16),
    }


def _prompt_trunk(x, gla_s, ssm_s, conv_s, gw, sw, *, n_pad, gla_tile, ssd_tile):
    new_gla, new_ssm, new_conv = [], [], []
    for j in range(2):
        x, s = _gla_prompt_layer(x, gla_s[j], gw[j], n_pad=n_pad, tile=gla_tile)
        new_gla.append(s)
        x, s, c = _ssd_prompt_layer(x, ssm_s[j], conv_s[j], sw[j], n_pad=n_pad, tile=ssd_tile)
        new_ssm.append(s)
        new_conv.append(c)
    return x, new_gla, new_ssm, new_conv


def kernel(x_prompt, x_sample, state_gla, state_ssm, state_conv, meta_tokens, pre_norm, post_norm,
           gla_w_in, gla_w_gate_up, gla_b_gate, gla_w_norm, gla_w_out,
           ssd_w_in, ssd_conv_w, ssd_conv_b, ssd_dt_bias, ssd_a_log, ssd_d_skip, ssd_w_norm, ssd_w_out):
    gw = [_gla_weights(j, pre_norm[2 * j], post_norm[2 * j], gla_w_in, gla_w_gate_up, gla_b_gate,
                       gla_w_norm, gla_w_out) for j in range(2)]
    sw = [_ssd_weights(j, pre_norm[2 * j + 1], post_norm[2 * j + 1], ssd_w_in, ssd_conv_w,
                       ssd_conv_b, ssd_dt_bias, ssd_a_log, ssd_d_skip, ssd_w_norm, ssd_w_out)
          for j in range(2)]
    bsz = x_prompt.shape[0]

    n_pad = CHUNK - N_META
    x_meta = jnp.pad(meta_tokens.astype(F32), ((n_pad, 0), (0, 0)))[None]
    zg = jnp.zeros((1, GLA_HEADS, GLA_HK, GLA_HV), F32)
    zs = jnp.zeros((1, SSD_HEADS, SSD_HEADDIM, SSD_STATE), F32)
    zc = jnp.zeros((1, SUBLANES, SSD_CONV_DIM), F32)
    _, mg, ms, mc = _prompt_trunk(x_meta, [zg, zg], [zs, zs], [zc, zc], gw, sw,
                                  n_pad=n_pad, gla_tile=CHUNK, ssd_tile=CHUNK)
    y_prompt, pg, ps, pc = _prompt_trunk(x_prompt, mg, ms, mc, gw, sw, n_pad=0,
                                         gla_tile=GLA_TILE, ssd_tile=SSD_TILE)
    gla_p = jnp.stack(pg)
    ssm_p = jnp.stack(ps)
    conv_p = jnp.stack(pc)[:, :, SUBLANES - (SSD_CONV - 1):, :]

    xs = x_sample[:, 0, :]
    sbsz = xs.shape[0]
    convs = state_conv.reshape(2, sbsz, (SSD_CONV - 1) * SSD_CONV_DIM)
    gla_s = ssm_conv_s = None
    for j in range(2):
        xs, gla_s = _gla_step_layer(xs, state_gla, gla_s, j, gw[j])
        xs, *ssm_conv_s = _ssd_step_layer(xs, state_ssm, convs, ssm_conv_s, j, sw[j])
    ssm_s, conv_s = ssm_conv_s
    y_sample = xs[:, None, :]
    return (y_prompt, y_sample, gla_p, ssm_p, conv_p,
            gla_s, ssm_s, conv_s.reshape(2, sbsz, SSD_CONV - 1, SSD_CONV_DIM))
```

```python
import functools

import jax
import jax.numpy as jnp
from jax import lax
from jax.experimental import pallas as pl
from jax.experimental.pallas import tpu as pltpu

F32 = jnp.float32
BF16 = jnp.bfloat16

D_MODEL = 1024
N_META = 16
NORM_EPS = 1e-6

GLA_HEADS = 4
GLA_DK = 512
GLA_DV = 1024
GLA_HK = 128
GLA_HV = 256
GLA_RANK = 16
GLA_TAU = 16.0
GLA_SAFE_DROP = 40.0

SSD_DINNER = 2048
SSD_HEADDIM = 64
SSD_HEADS = 32
SSD_GROUPS = 4
SSD_HPG = 8
SSD_STATE = 128
SSD_CONV = 4
SSD_CONV_DIM = 3072
SSD_GW = SSD_HPG * SSD_HEADDIM

LANES = 128
SUBLANES = 8
CHUNK = 128
GLA_TILE = 512
SSD_TILE = 256
VMEM_LIMIT = 56 * 1024 * 1024

HIGHEST = lax.Precision.HIGHEST


def _dot(a, b, precision=None):
    return jnp.dot(a, b, preferred_element_type=F32, precision=precision)


def _dot_nt(a, b):
    return lax.dot_general(a, b, (((1,), (1,)), ((), ())), preferred_element_type=F32)


def _dot_tn(a, b):
    return lax.dot_general(a, b, (((0,), (0,)), ((), ())), preferred_element_type=F32)


def _rms(x, w):
    return x * lax.rsqrt(jnp.mean(x * x, axis=-1, keepdims=True) + NORM_EPS) * w


def _silu(x):
    h = 0.5 * x
    return h + h * jnp.tanh(h)


def _softplus(x):
    return jnp.maximum(x, 0.0) + jnp.log1p(jnp.exp(-jnp.abs(x)))


def _log_sigmoid(x):
    return -_softplus(-x)


def _tri(n):
    r = lax.broadcasted_iota(jnp.int32, (n, n), 0)
    c = lax.broadcasted_iota(jnp.int32, (n, n), 1)
    return r >= c


def _split3(x):
    hi = x.astype(BF16)
    r1 = x - hi.astype(F32)
    mid = r1.astype(BF16)
    lo = (r1 - mid.astype(F32)).astype(BF16)
    return hi, mid, lo


def _dot_exact_rhs(x, m_bf16):
    hi, mid, lo = _split3(x)
    return _dot(hi, m_bf16) + _dot(mid, m_bf16) + _dot(lo, m_bf16)


def _cumsum_rows(tril_bf16, x):
    hi, mid, lo = _split3(x)
    return _dot(tril_bf16, hi) + _dot(tril_bf16, mid) + _dot(tril_bf16, lo)


def _gla_attn_pairwise(qf, kf, bc, causal, kbuf_ref, bbuf_ref):
    n = qf.shape[0]
    kbuf_ref[...] = kf
    bbuf_ref[...] = bc
    lane = lax.broadcasted_iota(jnp.int32, (n, n), 1)

    def body(j, acc):
        kj = kbuf_ref[pl.ds(j, 1), :]
        bj = bbuf_ref[pl.ds(j, 1), :]
        col = jnp.sum(qf * jnp.exp(jnp.minimum(bc - bj, 0.0)) * kj, axis=1, keepdims=True)
        return jnp.where(lane == j, col, acc)

    acc = lax.fori_loop(0, n, body, jnp.zeros((n, n), F32))
    return jnp.where(causal, acc, 0.0)


def _gla_scan_tile(q, k, v, g, states, pairwise, kbuf_ref, bbuf_ref):
    causal = _tri(CHUNK)
    tril = causal.astype(BF16)
    states = list(states)
    outs = []
    for c in range(q.shape[0] // CHUNK):
        rows = slice(c * CHUNK, (c + 1) * CHUNK)
        bc = _cumsum_rows(tril, g[rows])
        b_last = bc[-1:, :]
        e_last = jnp.exp(b_last)
        qh = (q[rows] * jnp.exp(bc)).astype(BF16)
        if pairwise:
            kd = (k[rows] * jnp.exp(b_last - bc)).astype(BF16)
        else:
            kh = k[rows] * jnp.exp(-bc)
            kd = (kh * e_last).astype(BF16)
            kh = kh.astype(BF16)
        vb = v[rows].astype(BF16)
        parts = []
        for h in range(GLA_HEADS):
            ks = slice(h * GLA_HK, (h + 1) * GLA_HK)
            vs = slice(h * GLA_HV, (h + 1) * GLA_HV)
            if pairwise:
                attn = _gla_attn_pairwise(q[rows, ks], k[rows, ks], bc[:, ks], causal,
                                          kbuf_ref, bbuf_ref)
            else:
                attn = jnp.where(causal, _dot_nt(qh[:, ks], kh[:, ks]), 0.0)
            st = states[h]
            parts.append(_dot(attn.astype(BF16), vb[:, vs]) + _dot_nt(qh[:, ks], st.astype(BF16)))
            states[h] = st * e_last[:, ks] + _dot_tn(vb[:, vs], kd[:, ks])
        outs.append(jnp.concatenate(parts, axis=1))
    o = jnp.concatenate(outs, axis=0) if len(outs) > 1 else outs[0]
    return o, states


def _gla_tail(o, r, wn, wo, postw, x):
    parts = []
    for h in range(GLA_HEADS):
        vs = slice(h * GLA_HV, (h + 1) * GLA_HV)
        parts.append(_rms(o[:, vs], wn[:, vs]))
    on = jnp.concatenate(parts, axis=1) * _silu(r)
    y = _dot(on.astype(BF16), wo)
    return x + _rms(y, postw)


def _gla_prompt_kernel(x_ref, s0_ref, prew_ref, postw_ref, wq_ref, wk_ref, wv_ref, wr_ref,
                       wg_ref, wgu_ref, bg_ref, wn_ref, wo_ref,
                       y_ref, sout_ref, st_ref, o_ref, kbuf_ref, bbuf_ref, *, n_pad, tile):
    t = pl.program_id(1)

    @pl.when(t == 0)
    def _():
        for h in range(GLA_HEADS):
            st_ref[h] = s0_ref[0, h].T

    x = x_ref[0]
    hn = _rms(x, prew_ref[...])
    if n_pad:
        row = lax.broadcasted_iota(jnp.int32, (tile, 1), 0) + t * tile
        hn = jnp.where(row >= n_pad, hn, 0.0)
    hb = hn.astype(BF16)
    q = _dot(hb, wq_ref[...]) * (GLA_HK ** -0.5)
    k = _dot(hb, wk_ref[...])
    v = _dot(hb, wv_ref[...])
    r = _dot(hb, wr_ref[...])
    gl = _dot(hb, wg_ref[...])
    g = _log_sigmoid(_dot(gl.astype(BF16), wgu_ref[...]) + bg_ref[...]) * (1.0 / GLA_TAU)

    drop = None
    for c in range(tile // CHUNK):
        tot = jnp.sum(g[c * CHUNK:(c + 1) * CHUNK], axis=0, keepdims=True)
        drop = tot if drop is None else jnp.minimum(drop, tot)
    safe = jnp.min(drop) >= -GLA_SAFE_DROP

    def scan(pairwise):
        o, states = _gla_scan_tile(q, k, v, g, [st_ref[h] for h in range(GLA_HEADS)], pairwise,
                                   kbuf_ref, bbuf_ref)
        o_ref[...] = o
        for h in range(GLA_HEADS):
            st_ref[h] = states[h]

    pl.when(safe)(functools.partial(scan, False))
    pl.when(jnp.logical_not(safe))(functools.partial(scan, True))
    y_ref[0] = _gla_tail(o_ref[...], r, wn_ref[...], wo_ref[...], postw_ref[...], x)

    @pl.when(t == pl.num_programs(1) - 1)
    def _():
        for h in range(GLA_HEADS):
            sout_ref[0, h] = st_ref[h].T


def _full(shape):
    nd = len(shape)
    return pl.BlockSpec(shape, lambda *_: (0,) * nd, pipeline_mode=pl.Buffered(1))


def _gla_prompt_layer(x, s0, w, *, n_pad, tile):
    bsz, seq, _ = x.shape
    b0 = s0.shape[0]
    s_idx = (lambda b, t: (b, 0, 0, 0)) if b0 == bsz else (lambda b, t: (0, 0, 0, 0))
    weights = (w["pre"], w["post"], w["wq"], w["wk"], w["wv"], w["wr"], w["wg"], w["wgu"],
               w["bg"], w["wn"], w["wo"])
    return pl.pallas_call(
        functools.partial(_gla_prompt_kernel, n_pad=n_pad, tile=tile),
        grid=(bsz, seq // tile),
        in_specs=[pl.BlockSpec((1, tile, D_MODEL), lambda b, t: (b, t, 0)),
                  pl.BlockSpec((1, GLA_HEADS, GLA_HK, GLA_HV), s_idx)]
                 + [_full(a.shape) for a in weights],
        out_specs=[pl.BlockSpec((1, tile, D_MODEL), lambda b, t: (b, t, 0)),
                   pl.BlockSpec((1, GLA_HEADS, GLA_HK, GLA_HV), lambda b, t: (b, 0, 0, 0))],
        out_shape=[jax.ShapeDtypeStruct(x.shape, F32),
                   jax.ShapeDtypeStruct((bsz, GLA_HEADS, GLA_HK, GLA_HV), F32)],
        scratch_shapes=[pltpu.VMEM((GLA_HEADS, GLA_HV, GLA_HK), F32),
                        pltpu.VMEM((tile, GLA_DV), F32),
                        pltpu.VMEM((CHUNK, GLA_HK), F32),
                        pltpu.VMEM((CHUNK, GLA_HK), F32)],
        compiler_params=pltpu.CompilerParams(
            dimension_semantics=("arbitrary", "arbitrary"), vmem_limit_bytes=VMEM_LIMIT),
        name="gla_prompt",
    )(x, s0, *weights)


MXU_COLS = 256


def _interleave(a, b):
    ia = ib = 0
    while ia < len(a) or ib < len(b):
        if ib >= len(b) or (ia < len(a) and ia * len(b) <= ib * len(a)):
            a[ia]()
            ia += 1
        else:
            b[ib]()
            ib += 1


def _ssd_chunk_stages(c, env, expand_ref, st_ref, xc_ref, ys_ref):
    rows = slice(c * CHUNK, (c + 1) * CHUNK)
    pair = 2 * SSD_HEADDIM
    loc = {}

    def prep():
        causal = _tri(CHUNK)
        dt = env["dt"][rows]
        ac = _cumsum_rows(causal.astype(BF16), env["a"][rows])
        a_last = ac[-1:, :]
        wgt = jnp.exp(a_last - ac) * dt
        expand = expand_ref[...]
        xs = xc_ref[rows, 0:SSD_DINNER]
        loc.update(
            causal=causal, ac=ac, ac_t=ac.T, dt_t=dt.T,
            xw=(xs * _dot(wgt.astype(BF16), expand)).astype(BF16),
            e_last=_dot_exact_rhs(jnp.exp(a_last), expand),
            lo_half=lax.broadcasted_iota(jnp.int32, (CHUNK, pair), 1) < SSD_HEADDIM)

    def group(gi):
        ns_b = slice(SSD_DINNER + gi * SSD_STATE, SSD_DINNER + (gi + 1) * SSD_STATE)
        ns_c = slice(ns_b.start + SSD_GROUPS * SSD_STATE, ns_b.stop + SSD_GROUPS * SSD_STATE)
        gs = slice(gi * SSD_GW, (gi + 1) * SSD_GW)
        causal, ac, ac_t, dt_t, lo_half = (loc[k] for k in ("causal", "ac", "ac_t", "dt_t", "lo_half"))
        bg = xc_ref[rows, ns_b].astype(BF16)
        cg = xc_ref[rows, ns_c].astype(BF16)
        cb = _dot_nt(cg, bg)
        st = st_ref[gi]
        y_int = _dot(cg, st.astype(BF16))
        for qi in range(SSD_HPG // 2):
            h0 = gi * SSD_HPG + 2 * qi
            ps = slice(h0 * SSD_HEADDIM, h0 * SSD_HEADDIM + pair)
            ms, eas = [], []
            for e in range(2):
                h = h0 + e
                col = jnp.broadcast_to(ac[:, h:h + 1], (CHUNK, CHUNK))
                seg = col - ac_t[h:h + 1, :]
                ms.append(jnp.where(causal, cb * jnp.exp(seg) * dt_t[h:h + 1, :], 0.0).astype(BF16))
                eas.append(jnp.exp(col))
            xp = xc_ref[rows, ps]
            rhs = jnp.concatenate([jnp.where(lo_half, xp, 0.0).astype(BF16),
                                   jnp.where(lo_half, 0.0, xp).astype(BF16)], axis=0)
            ea = jnp.where(lo_half, eas[0], eas[1])
            ys_ref[rows, ps] = (_dot(jnp.concatenate(ms, axis=1), rhs)
                                + y_int[:, qi * pair:(qi + 1) * pair] * ea)
        st_ref[gi] = st * loc["e_last"][:, gs] + _dot_tn(bg, loc["xw"][:, gs])

    return [prep] + [functools.partial(group, gi) for gi in range(SSD_GROUPS)]


def _ssd_tail(y, z, wn, wo, postw, x):
    y = y * _silu(z)
    parts = []
    for gi in range(SSD_GROUPS):
        gs = slice(gi * SSD_GW, (gi + 1) * SSD_GW)
        parts.append(_rms(y[:, gs], wn[:, gs]))
    yn = jnp.concatenate(parts, axis=1)
    out = _dot(yn.astype(BF16), wo)
    return x + _rms(out, postw)


def _ssd_prompt_kernel(xa_ref, xb_ref, s0_ref, c0_ref, prew_ref, postw_ref, wz_ref, wx_ref,
                       wdt_ref, cw_ref, cb_ref, dtb_ref, alog_ref, dskip_ref, expand_ref, wn_ref,
                       wo_ref, y_ref, sout_ref, cout_ref, st_ref, tail_ref, ys_ref, xc_ref, z_ref,
                       xbc_ref, dt_ref, *, n_pad, tile, nt, n):
    i = pl.program_id(0)
    tj = (i - 1) % nt

    @pl.when(jnp.logical_and(i > 0, tj == 0))
    def _():
        for gi in range(SSD_GROUPS):
            s0 = s0_ref[0, gi * SSD_HPG:(gi + 1) * SSD_HPG].reshape(SSD_GW, SSD_STATE)
            st_ref[gi] = s0.T
        tail_ref[...] = c0_ref[0]

    def project_stages(slot):
        env = {}

        def norm():
            hn = _rms(xa_ref[0], prew_ref[...])
            if n_pad:
                row = (lax.broadcasted_iota(jnp.int32, (tile, 1), 0)
                       + (jnp.minimum(i, n - 1) % nt) * tile)
                hn = jnp.where(row >= n_pad, hn, 0.0)
            env["hb"] = hn.astype(BF16)

        def lhs():
            return env["hb"]

        def z_block(j):
            ls = slice(j * MXU_COLS, (j + 1) * MXU_COLS)
            z_ref[slot, :, ls] = _dot(lhs(), wz_ref[:, ls])

        def xbc_block(j):
            ls = slice(j * MXU_COLS, (j + 1) * MXU_COLS)
            xbc_ref[slot, SUBLANES:SUBLANES + tile, ls] = _dot(lhs(), wx_ref[:, ls])

        def dt_block():
            dt_ref[slot] = _dot(lhs(), wdt_ref[...])

        return ([norm] + [functools.partial(xbc_block, j) for j in range(SSD_CONV_DIM // MXU_COLS)]
                + [dt_block] + [functools.partial(z_block, j) for j in range(SSD_DINNER // MXU_COLS)])

    def mix_stages(slot):
        env = {}

        def conv_block(j):
            ls = slice(j * MXU_COLS, (j + 1) * MXU_COLS)
            if j == 0:
                xbc_ref[slot, 0:SUBLANES, :] = tail_ref[...]
            cw = cw_ref[:, ls]
            conv = cb_ref[:, ls] + xbc_ref[slot, SUBLANES:SUBLANES + tile, ls] * cw[3:4, :]
            for k in range(SSD_CONV - 1):
                off = SUBLANES - (SSD_CONV - 1) + k
                conv = conv + xbc_ref[slot, off:off + tile, ls] * cw[k:k + 1, :]
            xc_ref[:, ls] = _silu(conv)

        def gates():
            tail_ref[...] = xbc_ref[slot, tile:tile + SUBLANES, :]
            dt = _softplus(dt_ref[slot] + dtb_ref[...])
            if n_pad:
                row = lax.broadcasted_iota(jnp.int32, (tile, 1), 0) + tj * tile
                dt = jnp.where(row >= n_pad, dt, 0.0)
            env["dt"] = dt
            env["a"] = dt * (-jnp.exp(alog_ref[...]))

        def gate_norm():
            y = ys_ref[...] + xc_ref[:, 0:SSD_DINNER] * dskip_ref[...]
            y = y * _silu(z_ref[slot])
            wn = wn_ref[...]
            parts = []
            for gi in range(SSD_GROUPS):
                gs = slice(gi * SSD_GW, (gi + 1) * SSD_GW)
                parts.append(_rms(y[:, gs], wn[:, gs]))
            env["yn"] = jnp.concatenate(parts, axis=1).astype(BF16)

        def out_proj():
            out = _dot(env["yn"], wo_ref[...])
            y_ref[0] = xb_ref[0] + _rms(out, postw_ref[...])

        stages = [functools.partial(conv_block, j) for j in range(SSD_CONV_DIM // MXU_COLS)]
        stages.append(gates)
        for c in range(tile // CHUNK):
            stages += _ssd_chunk_stages(c, env, expand_ref, st_ref, xc_ref, ys_ref)
        return stages + [gate_norm, out_proj]

    def run(stages):
        for s in stages:
            s()

    def both_stages(par):
        _interleave(mix_stages(1 - par), project_stages(par))

    pl.when(i == 0)(lambda: run(project_stages(0)))
    for par in range(2):
        both = jnp.logical_and(jnp.logical_and(i > 0, i < n), i % 2 == par)
        pl.when(both)(functools.partial(both_stages, par))
    pl.when(i == n)(lambda: run(mix_stages((n - 1) % 2)))

    @pl.when(jnp.logical_and(i > 0, tj == nt - 1))
    def _():
        for gi in range(SSD_GROUPS):
            sout_ref[0, gi * SSD_HPG:(gi + 1) * SSD_HPG] = (
                st_ref[gi].T.reshape(SSD_HPG, SSD_HEADDIM, SSD_STATE))
        cout_ref[0] = tail_ref[...]


def _ssd_prompt_layer(x, s0, c0, w, *, n_pad, tile):
    bsz, seq, _ = x.shape
    nt = seq // tile
    n = bsz * nt
    own_state = s0.shape[0] == bsz

    def tile_p(i):
        k = jnp.minimum(i, n - 1)
        return k // nt, k % nt

    def tile_m(i):
        k = jnp.maximum(i - 1, 0)
        return k // nt, k % nt

    def state_in(i, *zeros):
        return ((tile_m(i)[0] if own_state else 0),) + zeros

    weights = (w["pre"], w["post"], w["wz"], w["wx"], w["wdt"], w["cw"], w["cb"], w["dtb"],
               w["alog"], w["dskip"], w["expand"], w["wn"], w["wo"])
    return pl.pallas_call(
        functools.partial(_ssd_prompt_kernel, n_pad=n_pad, tile=tile, nt=nt, n=n),
        grid=(n + 1,),
        in_specs=[pl.BlockSpec((1, tile, D_MODEL), lambda i: tile_p(i) + (0,)),
                  pl.BlockSpec((1, tile, D_MODEL), lambda i: tile_m(i) + (0,)),
                  pl.BlockSpec((1, SSD_HEADS, SSD_HEADDIM, SSD_STATE), lambda i: state_in(i, 0, 0, 0)),
                  pl.BlockSpec((1, SUBLANES, SSD_CONV_DIM), lambda i: state_in(i, 0, 0))]
                 + [_full(a.shape) for a in weights],
        out_specs=[pl.BlockSpec((1, tile, D_MODEL), lambda i: tile_m(i) + (0,)),
                   pl.BlockSpec((1, SSD_HEADS, SSD_HEADDIM, SSD_STATE),
                                lambda i: (tile_m(i)[0], 0, 0, 0)),
                   pl.BlockSpec((1, SUBLANES, SSD_CONV_DIM), lambda i: (tile_m(i)[0], 0, 0))],
        out_shape=[jax.ShapeDtypeStruct(x.shape, F32),
                   jax.ShapeDtypeStruct((bsz, SSD_HEADS, SSD_HEADDIM, SSD_STATE), F32),
                   jax.ShapeDtypeStruct((bsz, SUBLANES, SSD_CONV_DIM), F32)],
        scratch_shapes=[pltpu.VMEM((SSD_GROUPS, SSD_STATE, SSD_GW), F32),
                        pltpu.VMEM((SUBLANES, SSD_CONV_DIM), F32),
                        pltpu.VMEM((tile, SSD_DINNER), F32),
                        pltpu.VMEM((tile, SSD_CONV_DIM), F32),
                        pltpu.VMEM((2, tile, SSD_DINNER), F32),
                        pltpu.VMEM((2, tile + SUBLANES, SSD_CONV_DIM), F32),
                        pltpu.VMEM((2, tile, LANES), F32)],
        compiler_params=pltpu.CompilerParams(
            dimension_semantics=("arbitrary",), vmem_limit_bytes=VMEM_LIMIT),
        name="ssd_prompt",
    )(x, x, s0, c0, *weights)


def _column(row, width, lane0=0):
    seg = row[:, lane0:lane0 + width]
    r = lax.broadcasted_iota(jnp.int32, (width, width), 0)
    c = lax.broadcasted_iota(jnp.int32, (width, width), 1)
    return jnp.sum(jnp.where(r == c, seg, 0.0), axis=1, keepdims=True)


def _gla_step_kernel(x_ref, s_ref, prew_ref, postw_ref, wq_ref, wk_ref, wv_ref, wr_ref,
                     wg_ref, wgu_ref, bg_ref, wn_ref, wo_ref, *rest, bb, has_prev):
    y_ref, sout_ref, q_s, k_s, v_s, e_s, o_s = rest[1:] if has_prev else rest
    i = pl.program_id(0)

    @pl.when(i == 0)
    def _():
        hb = _rms(x_ref[...], prew_ref[...]).astype(BF16)
        q_s[...] = _dot(hb, wq_ref[...]) * (GLA_HK ** -0.5)
        k_s[...] = _dot(hb, wk_ref[...])
        v_s[...] = _dot(hb, wv_ref[...])
        gl = _dot(hb, wg_ref[...])
        g = _log_sigmoid(_dot(gl.astype(BF16), wgu_ref[...]) + bg_ref[...]) * (1.0 / GLA_TAU)
        e_s[...] = jnp.exp(g)

    def body(j, carry):
        row = i * bb + j
        qr = q_s[pl.ds(row, 1), :]
        kr = k_s[pl.ds(row, 1), :]
        vr = v_s[pl.ds(row, 1), :]
        er = e_s[pl.ds(row, 1), :]
        for h in range(GLA_HEADS):
            vs = slice(h * GLA_HV, (h + 1) * GLA_HV)
            ecol = _column(er, GLA_HK, h * GLA_HK)
            kcol = _column(kr, GLA_HK, h * GLA_HK)
            qcol = _column(qr, GLA_HK, h * GLA_HK)
            s_new = s_ref[j, h] * ecol + kcol * vr[:, vs]
            sout_ref[j, h] = s_new
            o_s[pl.ds(row, 1), vs] = jnp.sum(qcol * s_new, axis=0, keepdims=True)
        return carry

    lax.fori_loop(0, bb, body, 0)

    @pl.when(i == pl.num_programs(0) - 1)
    def _():
        x = x_ref[...]
        hb = _rms(x, prew_ref[...]).astype(BF16)
        r = _dot(hb, wr_ref[...])
        y_ref[...] = _gla_tail(o_s[...], r, wn_ref[...], wo_ref[...], postw_ref[...], x)


def _layer_block(layer, bb, tail):
    nz = len(tail)
    return pl.BlockSpec((None, bb) + tail, lambda i, *_: (layer, i) + (0,) * nz)


def _gla_step_layer(x, states, prev_out, layer, w, *, bb=8):
    bsz = x.shape[0]
    weights = (w["pre"], w["post"], w["wq"], w["wk"], w["wv"], w["wr"], w["wg"], w["wgu"],
               w["bg"], w["wn"], w["wo"])
    sblk = _layer_block(layer, bb, (GLA_HEADS, GLA_HK, GLA_HV))
    carried = [] if prev_out is None else [prev_out]
    return pl.pallas_call(
        functools.partial(_gla_step_kernel, bb=bb, has_prev=prev_out is not None),
        grid=(bsz // bb,),
        in_specs=[_full(x.shape), sblk] + [_full(a.shape) for a in weights]
                 + [pl.BlockSpec(memory_space=pl.ANY)] * len(carried),
        out_specs=[pl.BlockSpec(x.shape, lambda i: (0, 0)), sblk],
        out_shape=[jax.ShapeDtypeStruct(x.shape, F32), jax.ShapeDtypeStruct(states.shape, F32)],
        scratch_shapes=[pltpu.VMEM((bsz, GLA_DK), F32), pltpu.VMEM((bsz, GLA_DK), F32),
                        pltpu.VMEM((bsz, GLA_DV), F32), pltpu.VMEM((bsz, GLA_DK), F32),
                        pltpu.VMEM((bsz, GLA_DV), F32)],
        input_output_aliases={2 + len(weights): 1} if carried else {},
        compiler_params=pltpu.CompilerParams(
            dimension_semantics=("arbitrary",), vmem_limit_bytes=VMEM_LIMIT),
        name="gla_step",
    )(x, states, *weights, *carried)


def _ssd_step_kernel(x_ref, s_ref, cv_ref, prew_ref, postw_ref, wz_ref, wx_ref, wdt_ref,
                     cw_ref, cb_ref, dtb_ref, alog_ref, dskip_ref, expand_ref, expand_n_ref,
                     wn_ref, wo_ref, *rest, bb, has_prev):
    (y_ref, sout_ref, cvout_ref, xs_s, xl_s, ct_s, b_s, e_s, yt_s) = rest[2:] if has_prev else rest
    i = pl.program_id(0)
    bsz = x_ref.shape[0]
    cd = SSD_CONV_DIM

    @pl.when(i == 0)
    def _():
        hb = _rms(x_ref[...], prew_ref[...]).astype(BF16)
        xbc = _dot(hb, wx_ref[...])
        dtr = _dot(hb, wdt_ref[...])
        cw = cw_ref[...]
        conv = cb_ref[...] + xbc * cw[3:4, :]
        for t in range(SSD_CONV - 1):
            conv = conv + cv_ref[:, t * cd:(t + 1) * cd] * cw[t:t + 1, :]
        cvout_ref[:, 0:cd] = cv_ref[:, cd:2 * cd]
        cvout_ref[:, cd:2 * cd] = cv_ref[:, 2 * cd:3 * cd]
        cvout_ref[:, 2 * cd:3 * cd] = xbc
        xc = _silu(conv)
        xs = xc[:, :SSD_DINNER]
        dt = _softplus(dtr + dtb_ref[...])
        ea = jnp.exp(dt * (-jnp.exp(alog_ref[...])))
        xs_s[...] = xs
        xd_t = (xs * _dot_exact_rhs(dt, expand_ref[...])).T
        c_t = xc[:, SSD_DINNER + SSD_GROUPS * SSD_STATE:].T
        for p in range(SSD_GROUPS // 2):
            for e in range(2):
                gi = 2 * p + e
                xl_s[p, :, e * bsz:(e + 1) * bsz] = (
                    xd_t[gi * SSD_GW:(gi + 1) * SSD_GW].astype(BF16))
        ct_s[...] = c_t
        b_s[...] = xc[:, SSD_DINNER:SSD_DINNER + SSD_GROUPS * SSD_STATE]
        e_s[...] = _dot_exact_rhs(ea, expand_n_ref[...])
        yt_s[...] = jnp.zeros(yt_s.shape, F32)

    sub = lax.broadcasted_iota(jnp.int32, (bsz, SSD_STATE), 0)
    lane = lax.broadcasted_iota(jnp.int32, (SSD_STATE, bsz), 1)
    zero_bn = jnp.zeros((bsz, SSD_STATE), BF16)
    zero_nb = jnp.zeros((SSD_STATE, bsz), BF16)

    for p in range(SSD_GROUPS // 2):
        y_acc = None
        for j in range(bb):
            row = i * bb + j
            er = e_s[pl.ds(row, 1), :]
            bsel, csel = [], []
            for e in range(2):
                ns = slice((2 * p + e) * SSD_STATE, (2 * p + e + 1) * SSD_STATE)
                bsel.append(jnp.where(sub == row, b_s[:, ns], 0.0).astype(BF16))
                csel.append(jnp.where(lane == row, ct_s[ns, :], 0.0).astype(BF16))
            rhs_b = jnp.concatenate([jnp.concatenate([bsel[0], zero_bn], axis=1),
                                     jnp.concatenate([zero_bn, bsel[1]], axis=1)], axis=0)
            rhs_c = jnp.concatenate([jnp.concatenate([csel[0], zero_nb], axis=1),
                                     jnp.concatenate([zero_nb, csel[1]], axis=1)], axis=0)
            outer = _dot(xl_s[p], rhs_b)
            halves = []
            for e in range(2):
                pieces = []
                for r in range(SSD_HPG):
                    h = (2 * p + e) * SSD_HPG + r
                    s_new = (s_ref[j, h] * er[:, h * SSD_STATE:(h + 1) * SSD_STATE]
                             + outer[r * SSD_HEADDIM:(r + 1) * SSD_HEADDIM,
                                     e * SSD_STATE:(e + 1) * SSD_STATE])
                    sout_ref[j, h] = s_new
                    pieces.append(s_new.astype(BF16))
                halves.append(jnp.concatenate(pieces, axis=0))
            y_j = _dot(jnp.concatenate(halves, axis=1), rhs_c)
            y_acc = y_j if y_acc is None else y_acc + y_j
        yt_s[p] += y_acc

    @pl.when(i == pl.num_programs(0) - 1)
    def _():
        x = x_ref[...]
        hb = _rms(x, prew_ref[...]).astype(BF16)
        z = _dot(hb, wz_ref[...])
        y = jnp.concatenate([yt_s[p, :, e * bsz:(e + 1) * bsz].T
                             for p in range(SSD_GROUPS // 2) for e in range(2)], axis=1)
        y = y + xs_s[...] * dskip_ref[...]
        y_ref[...] = _ssd_tail(y, z, wn_ref[...], wo_ref[...], postw_ref[...], x)


def _ssd_step_layer(x, states, convs, prev_out, layer, w, *, bb=4):
    bsz = x.shape[0]
    weights = (w["pre"], w["post"], w["wz"], w["wx"], w["wdt"], w["cw"], w["cb"], w["dtb"],
               w["alog"], w["dskip"], w["expand"], w["expand_n"], w["wn"], w["wo"])
    sblk = _layer_block(layer, bb, (SSD_HEADS, SSD_HEADDIM, SSD_STATE))
    cblk = pl.BlockSpec((None,) + convs.shape[1:], lambda i: (layer, 0, 0))
    carried = [] if prev_out is None else list(prev_out)
    n_in = 3 + len(weights)
    return pl.pallas_call(
        functools.partial(_ssd_step_kernel, bb=bb, has_prev=prev_out is not None),
        grid=(bsz // bb,),
        in_specs=[_full(x.shape), sblk,
                  pl.BlockSpec(cblk.block_shape, cblk.index_map, pipeline_mode=pl.Buffered(1))]
                 + [_full(a.shape) for a in weights]
                 + [pl.BlockSpec(memory_space=pl.ANY)] * len(carried),
        out_specs=[pl.BlockSpec(x.shape, lambda i: (0, 0)), sblk, cblk],
        out_shape=[jax.ShapeDtypeStruct(x.shape, F32), jax.ShapeDtypeStruct(states.shape, F32),
                   jax.ShapeDtypeStruct(convs.shape, F32)],
        input_output_aliases={n_in: 1, n_in + 1: 2} if carried else {},
        scratch_shapes=[pltpu.VMEM((bsz, SSD_DINNER), F32),
                        pltpu.VMEM((SSD_GROUPS // 2, SSD_GW, 2 * bsz), BF16),
                        pltpu.VMEM((SSD_GROUPS * SSD_STATE, bsz), F32),
                        pltpu.VMEM((bsz, SSD_GROUPS * SSD_STATE), F32),
                        pltpu.VMEM((bsz, SSD_HEADS * SSD_STATE), F32),
                        pltpu.VMEM((SSD_GROUPS // 2, SSD_GW, 2 * bsz), F32)],
        compiler_params=pltpu.CompilerParams(
            dimension_semantics=("arbitrary",), vmem_limit_bytes=VMEM_LIMIT),
        name="ssd_step",
    )(x, states, convs, *weights, *carried)


def _pad_lanes(a, n=LANES):
    return jnp.pad(a, ((0, 0), (0, n - a.shape[1])))


def _gla_weights(j, pre, post, w_in, w_gate_up, b_gate, w_norm, w_out):
    wi = w_in[j]
    o1, o2, o3, o4 = GLA_DK, 2 * GLA_DK, 2 * GLA_DK + GLA_DV, 2 * GLA_DK + 2 * GLA_DV
    return {
        "pre": pre[None, :], "post": post[None, :],
        "wq": wi[:, :o1].astype(BF16), "wk": wi[:, o1:o2].astype(BF16),
        "wv": wi[:, o2:o3].astype(BF16), "wr": wi[:, o3:o4].astype(BF16),
        "wg": _pad_lanes(wi[:, o4:]).astype(BF16),
        "wgu": jnp.pad(w_gate_up[j], ((0, LANES - GLA_RANK), (0, 0))).astype(BF16),
        "bg": b_gate[j][None, :], "wn": w_norm[j][None, :], "wo": w_out[j].astype(BF16),
    }


def _ssd_weights(j, pre, post, w_in, conv_w, conv_b, dt_bias, a_log, d_skip, w_norm, w_out):
    wi = w_in[j]
    o1, o2 = SSD_DINNER, SSD_DINNER + SSD_CONV_DIM
    head_of_lane = jnp.arange(SSD_DINNER) // SSD_HEADDIM
    expand = (jnp.arange(LANES)[:, None] == head_of_lane[None, :]).astype(BF16)
    head_of_lane_n = jnp.arange(SSD_HEADS * SSD_STATE) // SSD_STATE
    expand_n = (jnp.arange(LANES)[:, None] == head_of_lane_n[None, :]).astype(BF16)
    return {
        "pre": pre[None, :], "post": post[None, :],
        "wz": wi[:, :o1].astype(BF16), "wx": wi[:, o1:o2].astype(BF16),
        "wdt": _pad_lanes(wi[:, o2:]).astype(BF16),
        "cw": jnp.pad(conv_w[j], ((0, SUBLANES - SSD_CONV), (0, 0))), "cb": conv_b[j][None, :],
        "dtb": _pad_lanes(dt_bias[j][None, :]), "alog": _pad_lanes(a_log[j][None, :]),
        "dskip": jnp.repeat(d_skip[j], SSD_HEADDIM)[None, :],
        "expand": expand, "expand_n": expand_n,
        "wn": w_norm[j][None, :], "wo": w_out[j].astype(BF16),
    }


def _prompt_trunk(x, gla_s, ssm_s, conv_s, gw, sw, *, n_pad, gla_tile, ssd_tile):
    new_gla, new_ssm, new_conv = [], [], []
    for j in range(2):
        x, s = _gla_prompt_layer(x, gla_s[j], gw[j], n_pad=n_pad, tile=gla_tile)
        new_gla.append(s)
        x, s, c = _ssd_prompt_layer(x, ssm_s[j], conv_s[j], sw[j], n_pad=n_pad, tile=ssd_tile)
        new_ssm.append(s)
        new_conv.append(c)
    return x, new_gla, new_ssm, new_conv


def kernel(x_prompt, x_sample, state_gla, state_ssm, state_conv, meta_tokens, pre_norm, post_norm,
           gla_w_in, gla_w_gate_up, gla_b_gate, gla_w_norm, gla_w_out,
           ssd_w_in, ssd_conv_w, ssd_conv_b, ssd_dt_bias, ssd_a_log, ssd_d_skip, ssd_w_norm, ssd_w_out):
    gw = [_gla_weights(j, pre_norm[2 * j], post_norm[2 * j], gla_w_in, gla_w_gate_up, gla_b_gate,
                       gla_w_norm, gla_w_out) for j in range(2)]
    sw = [_ssd_weights(j, pre_norm[2 * j + 1], post_norm[2 * j + 1], ssd_w_in, ssd_conv_w,
                       ssd_conv_b, ssd_dt_bias, ssd_a_log, ssd_d_skip, ssd_w_norm, ssd_w_out)
          for j in range(2)]
    bsz = x_prompt.shape[0]

    n_pad = CHUNK - N_META
    x_meta = jnp.pad(meta_tokens.astype(F32), ((n_pad, 0), (0, 0)))[None]
    zg = jnp.zeros((1, GLA_HEADS, GLA_HK, GLA_HV), F32)
    zs = jnp.zeros((1, SSD_HEADS, SSD_HEADDIM, SSD_STATE), F32)
    zc = jnp.zeros((1, SUBLANES, SSD_CONV_DIM), F32)
    _, mg, ms, mc = _prompt_trunk(x_meta, [zg, zg], [zs, zs], [zc, zc], gw, sw,
                                  n_pad=n_pad, gla_tile=CHUNK, ssd_tile=CHUNK)
    y_prompt, pg, ps, pc = _prompt_trunk(x_prompt, mg, ms, mc, gw, sw, n_pad=0,
                                         gla_tile=GLA_TILE, ssd_tile=SSD_TILE)
    gla_p = jnp.stack(pg)
    ssm_p = jnp.stack(ps)
    conv_p = jnp.stack(pc)[:, :, SUBLANES - (SSD_CONV - 1):, :]

    xs = x_sample[:, 0, :]
    sbsz = xs.shape[0]
    convs = state_conv.reshape(2, sbsz, (SSD_CONV - 1) * SSD_CONV_DIM)
    gla_s = ssm_conv_s = None
    for j in range(2):
        xs, gla_s = _gla_step_layer(xs, state_gla, gla_s, j, gw[j])
        xs, *ssm_conv_s = _ssd_step_layer(xs, state_ssm, convs, ssm_conv_s, j, sw[j])
    ssm_s, conv_s = ssm_conv_s
    y_sample = xs[:, None, :]
    return (y_prompt, y_sample, gla_p, ssm_p, conv_p,
            gla_s, ssm_s, conv_s.reshape(2, sbsz, SSD_CONV - 1, SSD_CONV_DIM))
```

```python
import functools

import jax
import jax.numpy as jnp
from jax import lax
from jax.experimental import pallas as pl
from jax.experimental.pallas import tpu as pltpu

F32 = jnp.float32
BF16 = jnp.bfloat16

D_MODEL = 1024
N_META = 16
NORM_EPS = 1e-6
N_LAYERS_PER_MIXER = 2

GLA_HEADS = 4
GLA_DK = 512
GLA_DV = 1024
GLA_HK = 128
GLA_HV = 256
GLA_RANK = 16
GLA_TAU = 16.0
GLA_SAFE_DROP = 40.0

SSD_DINNER = 2048
SSD_HEADDIM = 64
SSD_HEADS = 32
SSD_GROUPS = 4
SSD_HPG = 8
SSD_STATE = 128
SSD_CONV = 4
SSD_CONV_DIM = 3072
SSD_GW = SSD_HPG * SSD_HEADDIM

LANES = 128
SUBLANES = 8
CHUNK = 128
GLA_TILE = 512
SSD_TILE = 256
VMEM_LIMIT = 56 * 1024 * 1024

GLA_Q0, GLA_K0, GLA_V0, GLA_R0, GLA_G0 = 0, GLA_DK, 2 * GLA_DK, 2 * GLA_DK + GLA_DV, 2 * GLA_DK + 2 * GLA_DV
GLA_IN_PAD = GLA_G0 + LANES
SSD_Z0, SSD_X0, SSD_DT0 = 0, SSD_DINNER, SSD_DINNER + SSD_CONV_DIM
SSD_IN_PAD = SSD_DT0 + LANES


def _dot(a, b):
    return jnp.dot(a, b, preferred_element_type=F32)


def _dot_nt(a, b):
    return lax.dot_general(a, b, (((1,), (1,)), ((), ())), preferred_element_type=F32)


def _dot_tn(a, b):
    return lax.dot_general(a, b, (((0,), (0,)), ((), ())), preferred_element_type=F32)


def _rms(x, w):
    return x * lax.rsqrt(jnp.mean(x * x, axis=-1, keepdims=True) + NORM_EPS) * w


def _silu(x):
    h = 0.5 * x
    return h + h * jnp.tanh(h)


def _softplus(x):
    return jnp.maximum(x, 0.0) + jnp.log1p(jnp.exp(-jnp.abs(x)))


def _log_sigmoid(x):
    return -_softplus(-x)


def _tri(n):
    r = lax.broadcasted_iota(jnp.int32, (n, n), 0)
    c = lax.broadcasted_iota(jnp.int32, (n, n), 1)
    return r >= c


def _split3(x):
    hi = x.astype(BF16)
    r1 = x - hi.astype(F32)
    mid = r1.astype(BF16)
    lo = (r1 - mid.astype(F32)).astype(BF16)
    return hi, mid, lo


def _dot_exact_rhs(x, m_bf16):
    hi, mid, lo = _split3(x)
    return _dot(hi, m_bf16) + _dot(mid, m_bf16) + _dot(lo, m_bf16)


def _cumsum_rows(tril_bf16, x):
    hi, mid, lo = _split3(x)
    return _dot(tril_bf16, hi) + _dot(tril_bf16, mid) + _dot(tril_bf16, lo)


def _full(shape):
    nd = len(shape)
    return pl.BlockSpec(shape, lambda *_: (0,) * nd, pipeline_mode=pl.Buffered(1))


def _of_layer(a, layer):
    nz = a.ndim - 1
    return pl.BlockSpec((None,) + a.shape[1:], lambda *_: (layer,) + (0,) * nz,
                        pipeline_mode=pl.Buffered(1))


def _weight_specs(weights, layer):
    return [_of_layer(a, layer) if stacked else _full(a.shape) for a, stacked in weights]


def _batch_block(layer, bb, tail, own=True):
    nz = len(tail)
    return pl.BlockSpec((None, bb) + tail,
                        lambda i, *_: (layer, i if own else 0) + (0,) * nz)


def _gla_attn_pairwise(qf, kf, bc, causal, kbuf_ref, bbuf_ref):
    n = qf.shape[0]
    kbuf_ref[...] = kf
    bbuf_ref[...] = bc
    lane = lax.broadcasted_iota(jnp.int32, (n, n), 1)

    def body(j, acc):
        kj = kbuf_ref[pl.ds(j, 1), :]
        bj = bbuf_ref[pl.ds(j, 1), :]
        col = jnp.sum(qf * jnp.exp(jnp.minimum(bc - bj, 0.0)) * kj, axis=1, keepdims=True)
        return jnp.where(lane == j, col, acc)

    acc = lax.fori_loop(0, n, body, jnp.zeros((n, n), F32))
    return jnp.where(causal, acc, 0.0)


def _gla_scan_tile(q, k, v, g, states, pairwise, kbuf_ref, bbuf_ref):
    causal = _tri(CHUNK)
    tril = causal.astype(BF16)
    states = list(states)
    outs = []
    for c in range(q.shape[0] // CHUNK):
        rows = slice(c * CHUNK, (c + 1) * CHUNK)
        bc = _cumsum_rows(tril, g[rows])
        b_last = bc[-1:, :]
        e_last = jnp.exp(b_last)
        qh = (q[rows] * jnp.exp(bc)).astype(BF16)
        if pairwise:
            kd = (k[rows] * jnp.exp(b_last - bc)).astype(BF16)
        else:
            kh = k[rows] * jnp.exp(-bc)
            kd = (kh * e_last).astype(BF16)
            kh = kh.astype(BF16)
        vb = v[rows].astype(BF16)
        parts = []
        for h in range(GLA_HEADS):
            ks = slice(h * GLA_HK, (h + 1) * GLA_HK)
            vs = slice(h * GLA_HV, (h + 1) * GLA_HV)
            if pairwise:
                attn = _gla_attn_pairwise(q[rows, ks], k[rows, ks], bc[:, ks], causal,
                                          kbuf_ref, bbuf_ref)
            else:
                attn = jnp.where(causal, _dot_nt(qh[:, ks], kh[:, ks]), 0.0)
            st = states[h]
            parts.append(_dot(attn.astype(BF16), vb[:, vs]) + _dot_nt(qh[:, ks], st.astype(BF16)))
            states[h] = st * e_last[:, ks] + _dot_tn(vb[:, vs], kd[:, ks])
        outs.append(jnp.concatenate(parts, axis=1))
    o = jnp.concatenate(outs, axis=0) if len(outs) > 1 else outs[0]
    return o, states


def _gla_tail(o, r, wn, wo, postw, x):
    parts = []
    for h in range(GLA_HEADS):
        vs = slice(h * GLA_HV, (h + 1) * GLA_HV)
        parts.append(_rms(o[:, vs], wn[:, vs]))
    on = jnp.concatenate(parts, axis=1) * _silu(r)
    y = _dot(on.astype(BF16), wo)
    return x + _rms(y, postw)


def _gla_gates(hb, win_ref, wgu_ref, bg_ref):
    gl = _dot(hb, win_ref[:, GLA_G0:GLA_IN_PAD])
    return _log_sigmoid(_dot(gl.astype(BF16), wgu_ref[...]) + bg_ref[...]) * (1.0 / GLA_TAU)


def _gla_prompt_kernel(x_ref, s0_ref, prew_ref, postw_ref, win_ref, wgu_ref, bg_ref, wn_ref,
                       wo_ref, *rest, n_pad, tile, has_prev):
    y_ref, sout_ref, st_ref, o_ref, kbuf_ref, bbuf_ref = rest[1:] if has_prev else rest
    t = pl.program_id(1)

    @pl.when(t == 0)
    def _():
        for h in range(GLA_HEADS):
            st_ref[h] = s0_ref[0, h].T

    x = x_ref[0]
    hn = _rms(x, prew_ref[...])
    if n_pad:
        row = lax.broadcasted_iota(jnp.int32, (tile, 1), 0) + t * tile
        hn = jnp.where(row >= n_pad, hn, 0.0)
    hb = hn.astype(BF16)
    q = _dot(hb, win_ref[:, GLA_Q0:GLA_K0]) * (GLA_HK ** -0.5)
    k = _dot(hb, win_ref[:, GLA_K0:GLA_V0])
    v = _dot(hb, win_ref[:, GLA_V0:GLA_R0])
    r = _dot(hb, win_ref[:, GLA_R0:GLA_G0])
    g = _gla_gates(hb, win_ref, wgu_ref, bg_ref)

    drop = None
    for c in range(tile // CHUNK):
        tot = jnp.sum(g[c * CHUNK:(c + 1) * CHUNK], axis=0, keepdims=True)
        drop = tot if drop is None else jnp.minimum(drop, tot)
    safe = jnp.min(drop) >= -GLA_SAFE_DROP

    def scan(pairwise):
        o, states = _gla_scan_tile(q, k, v, g, [st_ref[h] for h in range(GLA_HEADS)], pairwise,
                                   kbuf_ref, bbuf_ref)
        o_ref[...] = o
        for h in range(GLA_HEADS):
            st_ref[h] = states[h]

    pl.when(safe)(functools.partial(scan, False))
    pl.when(jnp.logical_not(safe))(functools.partial(scan, True))
    y_ref[0] = _gla_tail(o_ref[...], r, wn_ref[...], wo_ref[...], postw_ref[...], x)

    @pl.when(t == pl.num_programs(1) - 1)
    def _():
        for h in range(GLA_HEADS):
            sout_ref[0, h] = st_ref[h].T


def _gla_weight_list(w):
    return [(w[k], True) for k in ("pre", "post", "win", "wgu", "bg", "wn", "wo")]


def _gla_prompt_layer(x, s0, prev_out, layer, w, *, n_pad, tile):
    bsz, seq, _ = x.shape
    tail = (GLA_HEADS, GLA_HK, GLA_HV)
    weights = _gla_weight_list(w)
    carried = [] if prev_out is None else [prev_out]
    return pl.pallas_call(
        functools.partial(_gla_prompt_kernel, n_pad=n_pad, tile=tile, has_prev=bool(carried)),
        grid=(bsz, seq // tile),
        in_specs=[pl.BlockSpec((1, tile, D_MODEL), lambda b, t: (b, t, 0)),
                  _batch_block(layer, 1, tail, own=s0.shape[1] == bsz)]
                 + _weight_specs(weights, layer)
                 + [pl.BlockSpec(memory_space=pl.ANY)] * len(carried),
        out_specs=[pl.BlockSpec((1, tile, D_MODEL), lambda b, t: (b, t, 0)),
                   _batch_block(layer, 1, tail)],
        out_shape=[jax.ShapeDtypeStruct(x.shape, F32),
                   jax.ShapeDtypeStruct((N_LAYERS_PER_MIXER, bsz) + tail, F32)],
        scratch_shapes=[pltpu.VMEM((GLA_HEADS, GLA_HV, GLA_HK), F32),
                        pltpu.VMEM((tile, GLA_DV), F32),
                        pltpu.VMEM((CHUNK, GLA_HK), F32),
                        pltpu.VMEM((CHUNK, GLA_HK), F32)],
        input_output_aliases={2 + len(weights): 1} if carried else {},
        compiler_params=pltpu.CompilerParams(
            dimension_semantics=("arbitrary", "arbitrary"), vmem_limit_bytes=VMEM_LIMIT),
        name="gla_prompt",
    )(x, s0, *[a for a, _ in weights], *carried)


def _ssd_conv(xbc, cbuf_ref, cw, cb, tile):
    cbuf_ref[SUBLANES:SUBLANES + tile, :] = xbc
    conv = cb + xbc * cw[3:4, :]
    for i in range(SSD_CONV - 1):
        off = SUBLANES - (SSD_CONV - 1) + i
        conv = conv + cbuf_ref[off:off + tile, :] * cw[i:i + 1, :]
    return _silu(conv)


def _ssd_scan_chunk(xs, bm, cm, dt, a, expand, st_ref, y_ref, row0, causal, tril, lo_half):
    ac = _cumsum_rows(tril, a)
    ac_t = ac.T
    dt_t = dt.T
    a_last = ac[-1:, :]
    wgt = jnp.exp(a_last - ac) * dt
    xw = (xs * _dot(wgt.astype(BF16), expand)).astype(BF16)
    e_last = _dot_exact_rhs(jnp.exp(a_last), expand)
    pair = 2 * SSD_HEADDIM
    for gi in range(SSD_GROUPS):
        ns = slice(gi * SSD_STATE, (gi + 1) * SSD_STATE)
        gs = slice(gi * SSD_GW, (gi + 1) * SSD_GW)
        bg = bm[:, ns].astype(BF16)
        cg = cm[:, ns].astype(BF16)
        cb = _dot_nt(cg, bg)
        st = st_ref[gi]
        y_int = _dot(cg, st.astype(BF16))
        for qi in range(SSD_HPG // 2):
            h0 = gi * SSD_HPG + 2 * qi
            ps = slice(h0 * SSD_HEADDIM, h0 * SSD_HEADDIM + pair)
            ms, eas = [], []
            for e in range(2):
                h = h0 + e
                col = jnp.broadcast_to(ac[:, h:h + 1], (CHUNK, CHUNK))
                seg = col - ac_t[h:h + 1, :]
                ms.append(jnp.where(causal, cb * jnp.exp(seg) * dt_t[h:h + 1, :], 0.0).astype(BF16))
                eas.append(jnp.exp(col))
            xp = xs[:, ps]
            rhs = jnp.concatenate([jnp.where(lo_half, xp, 0.0).astype(BF16),
                                   jnp.where(lo_half, 0.0, xp).astype(BF16)], axis=0)
            ea = jnp.where(lo_half, eas[0], eas[1])
            y_ref[row0:row0 + CHUNK, ps] = (_dot(jnp.concatenate(ms, axis=1), rhs)
                                            + y_int[:, qi * pair:(qi + 1) * pair] * ea)
        st_ref[gi] = st * e_last[:, gs] + _dot_tn(bg, xw[:, gs])


def _ssd_tail(y, z, wn, wo, postw, x):
    y = y * _silu(z)
    parts = []
    for gi in range(SSD_GROUPS):
        gs = slice(gi * SSD_GW, (gi + 1) * SSD_GW)
        parts.append(_rms(y[:, gs], wn[:, gs]))
    yn = jnp.concatenate(parts, axis=1)
    out = _dot(yn.astype(BF16), wo)
    return x + _rms(out, postw)


def _ssd_prompt_kernel(x_ref, s0_ref, c0_ref, prew_ref, postw_ref, win_ref, cw_ref, cb_ref,
                       dtb_ref, alog_ref, dskip_ref, wn_ref, wo_ref, expand_ref, *rest,
                       n_pad, tile, has_prev):
    y_ref, sout_ref, cout_ref, st_ref, cbuf_ref, ys_ref = rest[2:] if has_prev else rest
    t = pl.program_id(1)

    @pl.when(t == 0)
    def _():
        for gi in range(SSD_GROUPS):
            s0 = s0_ref[0, gi * SSD_HPG:(gi + 1) * SSD_HPG].reshape(SSD_GW, SSD_STATE)
            st_ref[gi] = s0.T
        cbuf_ref[0:SUBLANES, :] = c0_ref[0]

    x = x_ref[0]
    hn = _rms(x, prew_ref[...])
    row = lax.broadcasted_iota(jnp.int32, (tile, 1), 0) + t * tile
    if n_pad:
        hn = jnp.where(row >= n_pad, hn, 0.0)
    hb = hn.astype(BF16)
    xbc = _dot(hb, win_ref[:, SSD_X0:SSD_DT0])
    dtr = _dot(hb, win_ref[:, SSD_DT0:SSD_IN_PAD])
    xc = _ssd_conv(xbc, cbuf_ref, cw_ref[...], cb_ref[...], tile)
    z = _dot(hb, win_ref[:, SSD_Z0:SSD_X0])
    cbuf_ref[0:SUBLANES, :] = cbuf_ref[tile:tile + SUBLANES, :]
    xs = xc[:, :SSD_DINNER]
    bm = xc[:, SSD_DINNER:SSD_DINNER + SSD_GROUPS * SSD_STATE]
    cm = xc[:, SSD_DINNER + SSD_GROUPS * SSD_STATE:]
    dt = _softplus(dtr + dtb_ref[...])
    if n_pad:
        dt = jnp.where(row >= n_pad, dt, 0.0)
    a = dt * (-jnp.exp(alog_ref[...]))

    causal = _tri(CHUNK)
    tril = causal.astype(BF16)
    lo_half = lax.broadcasted_iota(jnp.int32, (CHUNK, 2 * SSD_HEADDIM), 1) < SSD_HEADDIM
    expand = expand_ref[...]
    for c in range(tile // CHUNK):
        rows = slice(c * CHUNK, (c + 1) * CHUNK)
        _ssd_scan_chunk(xs[rows], bm[rows], cm[rows], dt[rows], a[rows], expand,
                        st_ref, ys_ref, c * CHUNK, causal, tril, lo_half)
    y = ys_ref[...] + xs * dskip_ref[...]
    y_ref[0] = _ssd_tail(y, z, wn_ref[...], wo_ref[...], postw_ref[...], x)

    @pl.when(t == pl.num_programs(1) - 1)
    def _():
        for gi in range(SSD_GROUPS):
            sout_ref[0, gi * SSD_HPG:(gi + 1) * SSD_HPG] = (
                st_ref[gi].T.reshape(SSD_HPG, SSD_HEADDIM, SSD_STATE))
        cout_ref[0] = cbuf_ref[0:SUBLANES, :]


def _ssd_weight_list(w, step):
    names = ("pre", "post", "win", "cw", "cb", "dtb", "alog", "dskip", "wn", "wo")
    consts = ("expand", "expand_n") if step else ("expand",)
    return [(w[k], True) for k in names] + [(w[k], False) for k in consts]


def _ssd_prompt_layer(x, s0, c0, prev_out, layer, w, *, n_pad, tile):
    bsz, seq, _ = x.shape
    s_tail = (SSD_HEADS, SSD_HEADDIM, SSD_STATE)
    c_tail = (SUBLANES, SSD_CONV_DIM)
    own = s0.shape[1] == bsz
    weights = _ssd_weight_list(w, step=False)
    carried = [] if prev_out is None else list(prev_out)
    n_in = 3 + len(weights)
    return pl.pallas_call(
        functools.partial(_ssd_prompt_kernel, n_pad=n_pad, tile=tile, has_prev=bool(carried)),
        grid=(bsz, seq // tile),
        in_specs=[pl.BlockSpec((1, tile, D_MODEL), lambda b, t: (b, t, 0)),
                  _batch_block(layer, 1, s_tail, own), _batch_block(layer, 1, c_tail, own)]
                 + _weight_specs(weights, layer)
                 + [pl.BlockSpec(memory_space=pl.ANY)] * len(carried),
        out_specs=[pl.BlockSpec((1, tile, D_MODEL), lambda b, t: (b, t, 0)),
                   _batch_block(layer, 1, s_tail), _batch_block(layer, 1, c_tail)],
        out_shape=[jax.ShapeDtypeStruct(x.shape, F32),
                   jax.ShapeDtypeStruct((N_LAYERS_PER_MIXER, bsz) + s_tail, F32),
                   jax.ShapeDtypeStruct((N_LAYERS_PER_MIXER, bsz) + c_tail, F32)],
        scratch_shapes=[pltpu.VMEM((SSD_GROUPS, SSD_STATE, SSD_GW), F32),
                        pltpu.VMEM((tile + SUBLANES, SSD_CONV_DIM), F32),
                        pltpu.VMEM((tile, SSD_DINNER), F32)],
        input_output_aliases={n_in: 1, n_in + 1: 2} if carried else {},
        compiler_params=pltpu.CompilerParams(
            dimension_semantics=("arbitrary", "arbitrary"), vmem_limit_bytes=VMEM_LIMIT),
        name="ssd_prompt",
    )(x, s0, c0, *[a for a, _ in weights], *carried)


def _column(row, width, lane0=0):
    seg = row[:, lane0:lane0 + width]
    r = lax.broadcasted_iota(jnp.int32, (width, width), 0)
    c = lax.broadcasted_iota(jnp.int32, (width, width), 1)
    return jnp.sum(jnp.where(r == c, seg, 0.0), axis=1, keepdims=True)


def _gla_step_kernel(x_ref, s_ref, prew_ref, postw_ref, win_ref, wgu_ref, bg_ref, wn_ref, wo_ref,
                     *rest, bb, has_prev):
    y_ref, sout_ref, q_s, k_s, v_s, e_s, o_s = rest[1:] if has_prev else rest
    i = pl.program_id(0)

    @pl.when(i == 0)
    def _():
        hb = _rms(x_ref[...], prew_ref[...]).astype(BF16)
        q_s[...] = _dot(hb, win_ref[:, GLA_Q0:GLA_K0]) * (GLA_HK ** -0.5)
        k_s[...] = _dot(hb, win_ref[:, GLA_K0:GLA_V0])
        v_s[...] = _dot(hb, win_ref[:, GLA_V0:GLA_R0])
        e_s[...] = jnp.exp(_gla_gates(hb, win_ref, wgu_ref, bg_ref))

    def body(j, carry):
        row = i * bb + j
        qr = q_s[pl.ds(row, 1), :]
        kr = k_s[pl.ds(row, 1), :]
        vr = v_s[pl.ds(row, 1), :]
        er = e_s[pl.ds(row, 1), :]
        for h in range(GLA_HEADS):
            vs = slice(h * GLA_HV, (h + 1) * GLA_HV)
            ecol = _column(er, GLA_HK, h * GLA_HK)
            kcol = _column(kr, GLA_HK, h * GLA_HK)
            qcol = _column(qr, GLA_HK, h * GLA_HK)
            s_new = s_ref[j, h] * ecol + kcol * vr[:, vs]
            sout_ref[j, h] = s_new
            o_s[pl.ds(row, 1), vs] = jnp.sum(qcol * s_new, axis=0, keepdims=True)
        return carry

    lax.fori_loop(0, bb, body, 0)

    @pl.when(i == pl.num_programs(0) - 1)
    def _():
        x = x_ref[...]
        hb = _rms(x, prew_ref[...]).astype(BF16)
        r = _dot(hb, win_ref[:, GLA_R0:GLA_G0])
        y_ref[...] = _gla_tail(o_s[...], r, wn_ref[...], wo_ref[...], postw_ref[...], x)


def _gla_step_layer(x, states, prev_out, layer, w, *, bb=8):
    bsz = x.shape[0]
    weights = _gla_weight_list(w)
    sblk = _batch_block(layer, bb, (GLA_HEADS, GLA_HK, GLA_HV))
    carried = [] if prev_out is None else [prev_out]
    return pl.pallas_call(
        functools.partial(_gla_step_kernel, bb=bb, has_prev=bool(carried)),
        grid=(bsz // bb,),
        in_specs=[_full(x.shape), sblk] + _weight_specs(weights, layer)
                 + [pl.BlockSpec(memory_space=pl.ANY)] * len(carried),
        out_specs=[pl.BlockSpec(x.shape, lambda i: (0, 0)), sblk],
        out_shape=[jax.ShapeDtypeStruct(x.shape, F32), jax.ShapeDtypeStruct(states.shape, F32)],
        scratch_shapes=[pltpu.VMEM((bsz, GLA_DK), F32), pltpu.VMEM((bsz, GLA_DK), F32),
                        pltpu.VMEM((bsz, GLA_DV), F32), pltpu.VMEM((bsz, GLA_DK), F32),
                        pltpu.VMEM((bsz, GLA_DV), F32)],
        input_output_aliases={2 + len(weights): 1} if carried else {},
        compiler_params=pltpu.CompilerParams(
            dimension_semantics=("arbitrary",), vmem_limit_bytes=VMEM_LIMIT),
        name="gla_step",
    )(x, states, *[a for a, _ in weights], *carried)


def _ssd_step_kernel(x_ref, s_ref, cv_ref, prew_ref, postw_ref, win_ref, cw_ref, cb_ref, dtb_ref,
                     alog_ref, dskip_ref, wn_ref, wo_ref, expand_ref, expand_n_ref, *rest,
                     bb, has_prev):
    (y_ref, sout_ref, cvout_ref, xs_s, xl_s, ct_s, b_s, e_s, yt_s) = rest[2:] if has_prev else rest
    i = pl.program_id(0)
    bsz = x_ref.shape[0]
    cd = SSD_CONV_DIM

    @pl.when(i == 0)
    def _():
        hb = _rms(x_ref[...], prew_ref[...]).astype(BF16)
        xbc = _dot(hb, win_ref[:, SSD_X0:SSD_DT0])
        dtr = _dot(hb, win_ref[:, SSD_DT0:SSD_IN_PAD])
        cw = cw_ref[...]
        conv = cb_ref[...] + xbc * cw[3:4, :]
        for t in range(SSD_CONV - 1):
            conv = conv + cv_ref[:, t * cd:(t + 1) * cd] * cw[t:t + 1, :]
        cvout_ref[:, 0:cd] = cv_ref[:, cd:2 * cd]
        cvout_ref[:, cd:2 * cd] = cv_ref[:, 2 * cd:3 * cd]
        cvout_ref[:, 2 * cd:3 * cd] = xbc
        xc = _silu(conv)
        xs = xc[:, :SSD_DINNER]
        dt = _softplus(dtr + dtb_ref[...])
        ea = jnp.exp(dt * (-jnp.exp(alog_ref[...])))
        xs_s[...] = xs
        xd_t = (xs * _dot_exact_rhs(dt, expand_ref[...])).T
        c_t = xc[:, SSD_DINNER + SSD_GROUPS * SSD_STATE:].T
        for p in range(SSD_GROUPS // 2):
            for e in range(2):
                gi = 2 * p + e
                xl_s[p, :, e * bsz:(e + 1) * bsz] = (
                    xd_t[gi * SSD_GW:(gi + 1) * SSD_GW].astype(BF16))
        ct_s[...] = c_t
        b_s[...] = xc[:, SSD_DINNER:SSD_DINNER + SSD_GROUPS * SSD_STATE]
        e_s[...] = _dot_exact_rhs(ea, expand_n_ref[...])
        yt_s[...] = jnp.zeros(yt_s.shape, F32)

    sub = lax.broadcasted_iota(jnp.int32, (bsz, SSD_STATE), 0)
    lane = lax.broadcasted_iota(jnp.int32, (SSD_STATE, bsz), 1)
    zero_bn = jnp.zeros((bsz, SSD_STATE), BF16)
    zero_nb = jnp.zeros((SSD_STATE, bsz), BF16)

    for p in range(SSD_GROUPS // 2):
        y_acc = None
        for j in range(bb):
            row = i * bb + j
            er = e_s[pl.ds(row, 1), :]
            bsel, csel = [], []
            for e in range(2):
                ns = slice((2 * p + e) * SSD_STATE, (2 * p + e + 1) * SSD_STATE)
                bsel.append(jnp.where(sub == row, b_s[:, ns], 0.0).astype(BF16))
                csel.append(jnp.where(lane == row, ct_s[ns, :], 0.0).astype(BF16))
            rhs_b = jnp.concatenate([jnp.concatenate([bsel[0], zero_bn], axis=1),
                                     jnp.concatenate([zero_bn, bsel[1]], axis=1)], axis=0)
            rhs_c = jnp.concatenate([jnp.concatenate([csel[0], zero_nb], axis=1),
                                     jnp.concatenate([zero_nb, csel[1]], axis=1)], axis=0)
            outer = _dot(xl_s[p], rhs_b)
            halves = []
            for e in range(2):
                pieces = []
                for r in range(SSD_HPG):
                    h = (2 * p + e) * SSD_HPG + r
                    s_new = (s_ref[j, h] * er[:, h * SSD_STATE:(h + 1) * SSD_STATE]
                             + outer[r * SSD_HEADDIM:(r + 1) * SSD_HEADDIM,
                                     e * SSD_STATE:(e + 1) * SSD_STATE])
                    sout_ref[j, h] = s_new
                    pieces.append(s_new.astype(BF16))
                halves.append(jnp.concatenate(pieces, axis=0))
            y_j = _dot(jnp.concatenate(halves, axis=1), rhs_c)
            y_acc = y_j if y_acc is None else y_acc + y_j
        yt_s[p] += y_acc

    @pl.when(i == pl.num_programs(0) - 1)
    def _():
        x = x_ref[...]
        hb = _rms(x, prew_ref[...]).astype(BF16)
        z = _dot(hb, win_ref[:, SSD_Z0:SSD_X0])
        y = jnp.concatenate([yt_s[p, :, e * bsz:(e + 1) * bsz].T
                             for p in range(SSD_GROUPS // 2) for e in range(2)], axis=1)
        y = y + xs_s[...] * dskip_ref[...]
        y_ref[...] = _ssd_tail(y, z, wn_ref[...], wo_ref[...], postw_ref[...], x)


def _ssd_step_layer(x, states, convs, prev_out, layer, w, *, bb=4):
    bsz = x.shape[0]
    weights = _ssd_weight_list(w, step=True)
    sblk = _batch_block(layer, bb, (SSD_HEADS, SSD_HEADDIM, SSD_STATE))
    cblk = pl.BlockSpec((None,) + convs.shape[1:], lambda i: (layer, 0, 0))
    carried = [] if prev_out is None else list(prev_out)
    n_in = 3 + len(weights)
    return pl.pallas_call(
        functools.partial(_ssd_step_kernel, bb=bb, has_prev=bool(carried)),
        grid=(bsz // bb,),
        in_specs=[_full(x.shape), sblk,
                  pl.BlockSpec(cblk.block_shape, cblk.index_map, pipeline_mode=pl.Buffered(1))]
                 + _weight_specs(weights, layer)
                 + [pl.BlockSpec(memory_space=pl.ANY)] * len(carried),
        out_specs=[pl.BlockSpec(x.shape, lambda i: (0, 0)), sblk, cblk],
        out_shape=[jax.ShapeDtypeStruct(x.shape, F32), jax.ShapeDtypeStruct(states.shape, F32),
                   jax.ShapeDtypeStruct(convs.shape, F32)],
        input_output_aliases={n_in: 1, n_in + 1: 2} if carried else {},
        scratch_shapes=[pltpu.VMEM((bsz, SSD_DINNER), F32),
                        pltpu.VMEM((SSD_GROUPS // 2, SSD_GW, 2 * bsz), BF16),
                        pltpu.VMEM((SSD_GROUPS * SSD_STATE, bsz), F32),
                        pltpu.VMEM((bsz, SSD_GROUPS * SSD_STATE), F32),
                        pltpu.VMEM((bsz, SSD_HEADS * SSD_STATE), F32),
                        pltpu.VMEM((SSD_GROUPS // 2, SSD_GW, 2 * bsz), F32)],
        compiler_params=pltpu.CompilerParams(
            dimension_semantics=("arbitrary",), vmem_limit_bytes=VMEM_LIMIT),
        name="ssd_step",
    )(x, states, convs, *[a for a, _ in weights], *carried)


def _pad_last(a, n):
    return jnp.pad(a, [(0, 0)] * (a.ndim - 1) + [(0, n - a.shape[-1])])


def _rows(a):
    return a[:, None, :]


def _gla_weights(pre, post, w_in, w_gate_up, b_gate, w_norm, w_out):
    return {
        "pre": _rows(pre), "post": _rows(post),
        "win": _pad_last(w_in, GLA_IN_PAD).astype(BF16),
        "wgu": jnp.pad(w_gate_up, ((0, 0), (0, LANES - GLA_RANK), (0, 0))).astype(BF16),
        "bg": _rows(b_gate), "wn": _rows(w_norm), "wo": w_out.astype(BF16),
    }


def _ssd_weights(pre, post, w_in, conv_w, conv_b, dt_bias, a_log, d_skip, w_norm, w_out):
    head_of_lane = jnp.arange(SSD_DINNER) // SSD_HEADDIM
    head_of_lane_n = jnp.arange(SSD_HEADS * SSD_STATE) // SSD_STATE
    heads = jnp.arange(LANES)[:, None]
    return {
        "pre": _rows(pre), "post": _rows(post),
        "win": _pad_last(w_in, SSD_IN_PAD).astype(BF16),
        "cw": jnp.pad(conv_w, ((0, 0), (0, SUBLANES - SSD_CONV), (0, 0))), "cb": _rows(conv_b),
        "dtb": _rows(_pad_last(dt_bias, LANES)), "alog": _rows(_pad_last(a_log, LANES)),
        "dskip": _rows(jnp.repeat(d_skip, SSD_HEADDIM, axis=1)),
        "wn": _rows(w_norm), "wo": w_out.astype(BF16),
        "expand": (heads == head_of_lane[None, :]).astype(BF16),
        "expand_n": (heads == head_of_lane_n[None, :]).astype(BF16),
    }


def _prompt_trunk(x, gla0, ssm0, conv0, gw, sw, *, n_pad, gla_tile, ssd_tile):
    gla_s = ssm_conv_s = None
    for j in range(N_LAYERS_PER_MIXER):
        x, gla_s = _gla_prompt_layer(x, gla0, gla_s, j, gw, n_pad=n_pad, tile=gla_tile)
        x, *ssm_conv_s = _ssd_prompt_layer(x, ssm0, conv0, ssm_conv_s, j, sw,
                                           n_pad=n_pad, tile=ssd_tile)
    return x, gla_s, ssm_conv_s[0], ssm_conv_s[1]


def kernel(x_prompt, x_sample, state_gla, state_ssm, state_conv, meta_tokens, pre_norm, post_norm,
           gla_w_in, gla_w_gate_up, gla_b_gate, gla_w_norm, gla_w_out,
           ssd_w_in, ssd_conv_w, ssd_conv_b, ssd_dt_bias, ssd_a_log, ssd_d_skip, ssd_w_norm, ssd_w_out):
    gw = _gla_weights(pre_norm[0::2], post_norm[0::2], gla_w_in, gla_w_gate_up, gla_b_gate,
                      gla_w_norm, gla_w_out)
    sw = _ssd_weights(pre_norm[1::2], post_norm[1::2], ssd_w_in, ssd_conv_w, ssd_conv_b,
                      ssd_dt_bias, ssd_a_log, ssd_d_skip, ssd_w_norm, ssd_w_out)

    n_pad = CHUNK - N_META
    x_meta = jnp.pad(meta_tokens.astype(F32), ((n_pad, 0), (0, 0)))[None]
    zg = jnp.zeros((N_LAYERS_PER_MIXER, 1, GLA_HEADS, GLA_HK, GLA_HV), F32)
    zs = jnp.zeros((N_LAYERS_PER_MIXER, 1, SSD_HEADS, SSD_HEADDIM, SSD_STATE), F32)
    zc = jnp.zeros((N_LAYERS_PER_MIXER, 1, SUBLANES, SSD_CONV_DIM), F32)
    _, mg, ms, mc = _prompt_trunk(x_meta, zg, zs, zc, gw, sw,
                                  n_pad=n_pad, gla_tile=CHUNK, ssd_tile=CHUNK)
    y_prompt, gla_p, ssm_p, conv_p = _prompt_trunk(x_prompt, mg, ms, mc, gw, sw, n_pad=0,
                                                   gla_tile=GLA_TILE, ssd_tile=SSD_TILE)
    conv_p = conv_p[:, :, SUBLANES - (SSD_CONV - 1):, :]

    xs = x_sample[:, 0, :]
    sbsz = xs.shape[0]
    convs = state_conv.reshape(N_LAYERS_PER_MIXER, sbsz, (SSD_CONV - 1) * SSD_CONV_DIM)
    gla_s = ssm_conv_s = None
    for j in range(N_LAYERS_PER_MIXER):
        xs, gla_s = _gla_step_layer(xs, state_gla, gla_s, j, gw)
        xs, *ssm_conv_s = _ssd_step_layer(xs, state_ssm, convs, ssm_conv_s, j, sw)
    ssm_s, conv_s = ssm_conv_s
    y_sample = xs[:, None, :]
    return (y_prompt, y_sample, gla_p, ssm_p, conv_p, gla_s, ssm_s,
            conv_s.reshape(N_LAYERS_PER_MIXER, sbsz, SSD_CONV - 1, SSD_CONV_DIM))
```

```python
import functools

import jax
import jax.numpy as jnp
from jax import lax
from jax.experimental import pallas as pl
from jax.experimental.pallas import tpu as pltpu

F32 = jnp.float32
BF16 = jnp.bfloat16

D_MODEL = 1024
N_META = 16
NORM_EPS = 1e-6
N_LAYERS_PER_MIXER = 2

GLA_HEADS = 4
GLA_DK = 512
GLA_DV = 1024
GLA_HK = 128
GLA_HV = 256
GLA_RANK = 16
GLA_TAU = 16.0
GLA_SAFE_DROP = 40.0

SSD_DINNER = 2048
SSD_HEADDIM = 64
SSD_HEADS = 32
SSD_GROUPS = 4
SSD_HPG = 8
SSD_STATE = 128
SSD_CONV = 4
SSD_CONV_DIM = 3072
SSD_GW = SSD_HPG * SSD_HEADDIM

LANES = 128
SUBLANES = 8
CHUNK = 128
GLA_TILE = 1024
SSD_TILE = 512
VMEM_LIMIT = 56 * 1024 * 1024

GLA_Q0, GLA_K0, GLA_V0, GLA_R0, GLA_G0 = 0, GLA_DK, 2 * GLA_DK, 2 * GLA_DK + GLA_DV, 2 * GLA_DK + 2 * GLA_DV
SSD_Z0, SSD_X0, SSD_DT0 = 0, SSD_DINNER, SSD_DINNER + SSD_CONV_DIM


def _narrow_proj(hb, w_ref, start, width):
    res = jnp.dot(hb, w_ref[:, start:start + width], preferred_element_type=F32)
    return jnp.concatenate([res, jnp.zeros((res.shape[0], LANES - width), F32)], axis=1)


def _dot(a, b):
    return jnp.dot(a, b, preferred_element_type=F32)


def _dot_nt(a, b):
    return lax.dot_general(a, b, (((1,), (1,)), ((), ())), preferred_element_type=F32)


def _dot_tn(a, b):
    return lax.dot_general(a, b, (((0,), (0,)), ((), ())), preferred_element_type=F32)


def _rms(x, w):
    return x * lax.rsqrt(jnp.mean(x * x, axis=-1, keepdims=True) + NORM_EPS) * w


def _silu(x):
    h = 0.5 * x
    return h + h * jnp.tanh(h)


def _softplus(x):
    return jnp.maximum(x, 0.0) + jnp.log1p(jnp.exp(-jnp.abs(x)))


def _log_sigmoid(x):
    return -_softplus(-x)


def _tri(n):
    r = lax.broadcasted_iota(jnp.int32, (n, n), 0)
    c = lax.broadcasted_iota(jnp.int32, (n, n), 1)
    return r >= c


def _split3(x):
    hi = x.astype(BF16)
    r1 = x - hi.astype(F32)
    mid = r1.astype(BF16)
    lo = (r1 - mid.astype(F32)).astype(BF16)
    return hi, mid, lo


def _dot_exact_rhs(x, m_bf16):
    hi, mid, lo = _split3(x)
    return _dot(hi, m_bf16) + _dot(mid, m_bf16) + _dot(lo, m_bf16)


def _cumsum_rows(tril_bf16, x):
    hi, mid, lo = _split3(x)
    return _dot(tril_bf16, hi) + _dot(tril_bf16, mid) + _dot(tril_bf16, lo)


def _full(shape):
    nd = len(shape)
    return pl.BlockSpec(shape, lambda *_: (0,) * nd, pipeline_mode=pl.Buffered(1))


def _of_layer(a, layer):
    nz = a.ndim - 1
    return pl.BlockSpec((None,) + a.shape[1:], lambda *_: (layer,) + (0,) * nz,
                        pipeline_mode=pl.Buffered(1))


def _weight_specs(weights, layer):
    return [_of_layer(a, layer) if stacked else _full(a.shape) for a, stacked in weights]


def _batch_block(layer, bb, tail, own=True):
    nz = len(tail)
    return pl.BlockSpec((None, bb) + tail,
                        lambda i, *_: (layer, i if own else 0) + (0,) * nz)


def _gla_attn_pairwise(qf, kf, bc, causal, kbuf_ref, bbuf_ref):
    n = qf.shape[0]
    kbuf_ref[...] = kf
    bbuf_ref[...] = bc
    lane = lax.broadcasted_iota(jnp.int32, (n, n), 1)

    def body(j, acc):
        kj = kbuf_ref[pl.ds(j, 1), :]
        bj = bbuf_ref[pl.ds(j, 1), :]
        col = jnp.sum(qf * jnp.exp(jnp.minimum(bc - bj, 0.0)) * kj, axis=1, keepdims=True)
        return jnp.where(lane == j, col, acc)

    acc = lax.fori_loop(0, n, body, jnp.zeros((n, n), F32))
    return jnp.where(causal, acc, 0.0)


def _gla_scan_tile(q, k, v, g, states, pairwise, kbuf_ref, bbuf_ref):
    causal = _tri(CHUNK)
    tril = causal.astype(BF16)
    states = list(states)
    outs = []
    for c in range(q.shape[0] // CHUNK):
        rows = slice(c * CHUNK, (c + 1) * CHUNK)
        bc = _cumsum_rows(tril, g[rows])
        b_last = bc[-1:, :]
        e_last = jnp.exp(b_last)
        qh = (q[rows] * jnp.exp(bc)).astype(BF16)
        if pairwise:
            kd = (k[rows] * jnp.exp(b_last - bc)).astype(BF16)
        else:
            kh = k[rows] * jnp.exp(-bc)
            kd = (kh * e_last).astype(BF16)
            kh = kh.astype(BF16)
        vb = v[rows].astype(BF16)
        parts = []
        for h in range(GLA_HEADS):
            ks = slice(h * GLA_HK, (h + 1) * GLA_HK)
            vs = slice(h * GLA_HV, (h + 1) * GLA_HV)
            if pairwise:
                attn = _gla_attn_pairwise(q[rows, ks], k[rows, ks], bc[:, ks], causal,
                                          kbuf_ref, bbuf_ref)
            else:
                attn = jnp.where(causal, _dot_nt(qh[:, ks], kh[:, ks]), 0.0)
            st = states[h]
            parts.append(_dot(attn.astype(BF16), vb[:, vs]) + _dot_nt(qh[:, ks], st.astype(BF16)))
            states[h] = st * e_last[:, ks] + _dot_tn(vb[:, vs], kd[:, ks])
        outs.append(jnp.concatenate(parts, axis=1))
    o = jnp.concatenate(outs, axis=0) if len(outs) > 1 else outs[0]
    return o, states


def _gla_tail(o, r, wn, wo, postw, x):
    parts = []
    for h in range(GLA_HEADS):
        vs = slice(h * GLA_HV, (h + 1) * GLA_HV)
        parts.append(_rms(o[:, vs], wn[:, vs]))
    on = jnp.concatenate(parts, axis=1) * _silu(r)
    y = _dot(on.astype(BF16), wo)
    return x + _rms(y, postw)


def _gla_gates(hb, win_ref, wgu_ref, bg_ref):
    gl = _narrow_proj(hb, win_ref, GLA_G0, GLA_RANK)
    return _log_sigmoid(_dot(gl.astype(BF16), wgu_ref[...]) + bg_ref[...]) * (1.0 / GLA_TAU)


def _gla_prompt_kernel(x_ref, s0_ref, prew_ref, postw_ref, win_ref, wgu_ref, bg_ref, wn_ref,
                       wo_ref, *rest, n_pad, tile, has_prev):
    y_ref, sout_ref, st_ref, o_ref, kbuf_ref, bbuf_ref = rest[1:] if has_prev else rest
    t = pl.program_id(1)

    @pl.when(t == 0)
    def _():
        for h in range(GLA_HEADS):
            st_ref[h] = s0_ref[0, h].T

    x = x_ref[0]
    hn = _rms(x, prew_ref[...])
    if n_pad:
        row = lax.broadcasted_iota(jnp.int32, (tile, 1), 0) + t * tile
        hn = jnp.where(row >= n_pad, hn, 0.0)
    hb = hn.astype(BF16)
    q = _dot(hb, win_ref[:, GLA_Q0:GLA_K0]) * (GLA_HK ** -0.5)
    k = _dot(hb, win_ref[:, GLA_K0:GLA_V0])
    v = _dot(hb, win_ref[:, GLA_V0:GLA_R0])
    g = _gla_gates(hb, win_ref, wgu_ref, bg_ref)

    drop = None
    for c in range(tile // CHUNK):
        tot = jnp.sum(g[c * CHUNK:(c + 1) * CHUNK], axis=0, keepdims=True)
        drop = tot if drop is None else jnp.minimum(drop, tot)
    safe = jnp.min(drop) >= -GLA_SAFE_DROP

    def scan(pairwise):
        o, states = _gla_scan_tile(q, k, v, g, [st_ref[h] for h in range(GLA_HEADS)], pairwise,
                                   kbuf_ref, bbuf_ref)
        o_ref[...] = o
        for h in range(GLA_HEADS):
            st_ref[h] = states[h]

    pl.when(safe)(functools.partial(scan, False))
    pl.when(jnp.logical_not(safe))(functools.partial(scan, True))
    r = _dot(hb, win_ref[:, GLA_R0:GLA_G0])
    y_ref[0] = _gla_tail(o_ref[...], r, wn_ref[...], wo_ref[...], postw_ref[...], x)

    @pl.when(t == pl.num_programs(1) - 1)
    def _():
        for h in range(GLA_HEADS):
            sout_ref[0, h] = st_ref[h].T


def _gla_weight_list(w):
    return [(w[k], True) for k in ("pre", "post", "win", "wgu", "bg", "wn", "wo")]


def _gla_prompt_layer(x, s0, prev_out, layer, w, *, n_pad, tile):
    bsz, seq, _ = x.shape
    tail = (GLA_HEADS, GLA_HK, GLA_HV)
    weights = _gla_weight_list(w)
    carried = [] if prev_out is None else [prev_out]
    return pl.pallas_call(
        functools.partial(_gla_prompt_kernel, n_pad=n_pad, tile=tile, has_prev=bool(carried)),
        grid=(bsz, seq // tile),
        in_specs=[pl.BlockSpec((1, tile, D_MODEL), lambda b, t: (b, t, 0)),
                  _batch_block(layer, 1, tail, own=s0.shape[1] == bsz)]
                 + _weight_specs(weights, layer)
                 + [pl.BlockSpec(memory_space=pl.ANY)] * len(carried),
        out_specs=[pl.BlockSpec((1, tile, D_MODEL), lambda b, t: (b, t, 0)),
                   _batch_block(layer, 1, tail)],
        out_shape=[jax.ShapeDtypeStruct(x.shape, F32),
                   jax.ShapeDtypeStruct((N_LAYERS_PER_MIXER, bsz) + tail, F32)],
        scratch_shapes=[pltpu.VMEM((GLA_HEADS, GLA_HV, GLA_HK), F32),
                        pltpu.VMEM((tile, GLA_DV), F32),
                        pltpu.VMEM((CHUNK, GLA_HK), F32),
                        pltpu.VMEM((CHUNK, GLA_HK), F32)],
        input_output_aliases={2 + len(weights): 1} if carried else {},
        compiler_params=pltpu.CompilerParams(
            dimension_semantics=("arbitrary", "arbitrary"), vmem_limit_bytes=VMEM_LIMIT),
        name="gla_prompt",
    )(x, s0, *[a for a, _ in weights], *carried)


def _ssd_conv(xbc, cbuf_ref, cw, cb, tile):
    cbuf_ref[SUBLANES:SUBLANES + tile, :] = xbc
    conv = cb + xbc * cw[3:4, :]
    for i in range(SSD_CONV - 1):
        off = SUBLANES - (SSD_CONV - 1) + i
        conv = conv + cbuf_ref[off:off + tile, :] * cw[i:i + 1, :]
    return _silu(conv)


def _ssd_scan_chunk(xs, bm, cm, dt, a, expand, st_ref, y_ref, row0, causal, tril, lo_half):
    ac = _cumsum_rows(tril, a)
    ac_t = ac.T
    dt_t = dt.T
    a_last = ac[-1:, :]
    wgt = jnp.exp(a_last - ac) * dt
    xw = (xs * _dot(wgt.astype(BF16), expand)).astype(BF16)
    e_last = _dot_exact_rhs(jnp.exp(a_last), expand)
    pair = 2 * SSD_HEADDIM
    for gi in range(SSD_GROUPS):
        ns = slice(gi * SSD_STATE, (gi + 1) * SSD_STATE)
        gs = slice(gi * SSD_GW, (gi + 1) * SSD_GW)
        bg = bm[:, ns].astype(BF16)
        cg = cm[:, ns].astype(BF16)
        cb = _dot_nt(cg, bg)
        st = st_ref[gi]
        y_int = _dot(cg, st.astype(BF16))
        for qi in range(SSD_HPG // 2):
            h0 = gi * SSD_HPG + 2 * qi
            ps = slice(h0 * SSD_HEADDIM, h0 * SSD_HEADDIM + pair)
            ms, eas = [], []
            for e in range(2):
                h = h0 + e
                col = jnp.broadcast_to(ac[:, h:h + 1], (CHUNK, CHUNK))
                seg = col - ac_t[h:h + 1, :]
                ms.append(jnp.where(causal, cb * jnp.exp(seg) * dt_t[h:h + 1, :], 0.0).astype(BF16))
                eas.append(jnp.exp(col))
            xp = xs[:, ps]
            rhs = jnp.concatenate([jnp.where(lo_half, xp, 0.0).astype(BF16),
                                   jnp.where(lo_half, 0.0, xp).astype(BF16)], axis=0)
            ea = jnp.where(lo_half, eas[0], eas[1])
            y_ref[row0:row0 + CHUNK, ps] = (_dot(jnp.concatenate(ms, axis=1), rhs)
                                            + y_int[:, qi * pair:(qi + 1) * pair] * ea)
        st_ref[gi] = st * e_last[:, gs] + _dot_tn(bg, xw[:, gs])


def _ssd_tail(y, z, wn, wo, postw, x):
    y = y * _silu(z)
    parts = []
    for gi in range(SSD_GROUPS):
        gs = slice(gi * SSD_GW, (gi + 1) * SSD_GW)
        parts.append(_rms(y[:, gs], wn[:, gs]))
    yn = jnp.concatenate(parts, axis=1)
    out = _dot(yn.astype(BF16), wo)
    return x + _rms(out, postw)


def _ssd_prompt_kernel(x_ref, s0_ref, c0_ref, prew_ref, postw_ref, win_ref, cw_ref, cb_ref,
                       dtb_ref, alog_ref, dskip_ref, wn_ref, wo_ref, expand_ref, *rest,
                       n_pad, tile, has_prev):
    y_ref, sout_ref, cout_ref, st_ref, cbuf_ref, ys_ref = rest[2:] if has_prev else rest
    t = pl.program_id(1)

    @pl.when(t == 0)
    def _():
        for gi in range(SSD_GROUPS):
            s0 = s0_ref[0, gi * SSD_HPG:(gi + 1) * SSD_HPG].reshape(SSD_GW, SSD_STATE)
            st_ref[gi] = s0.T
        cbuf_ref[0:SUBLANES, :] = c0_ref[0]

    x = x_ref[0]
    hn = _rms(x, prew_ref[...])
    row = lax.broadcasted_iota(jnp.int32, (tile, 1), 0) + t * tile
    if n_pad:
        hn = jnp.where(row >= n_pad, hn, 0.0)
    hb = hn.astype(BF16)
    xbc = _dot(hb, win_ref[:, SSD_X0:SSD_DT0])
    dtr = _narrow_proj(hb, win_ref, SSD_DT0, SSD_HEADS)
    xc = _ssd_conv(xbc, cbuf_ref, cw_ref[...], cb_ref[...], tile)
    z = _dot(hb, win_ref[:, SSD_Z0:SSD_X0])
    cbuf_ref[0:SUBLANES, :] = cbuf_ref[tile:tile + SUBLANES, :]
    xs = xc[:, :SSD_DINNER]
    bm = xc[:, SSD_DINNER:SSD_DINNER + SSD_GROUPS * SSD_STATE]
    cm = xc[:, SSD_DINNER + SSD_GROUPS * SSD_STATE:]
    dt = _softplus(dtr + dtb_ref[...])
    if n_pad:
        dt = jnp.where(row >= n_pad, dt, 0.0)
    a = dt * (-jnp.exp(alog_ref[...]))

    causal = _tri(CHUNK)
    tril = causal.astype(BF16)
    lo_half = lax.broadcasted_iota(jnp.int32, (CHUNK, 2 * SSD_HEADDIM), 1) < SSD_HEADDIM
    expand = expand_ref[...]
    for c in range(tile // CHUNK):
        rows = slice(c * CHUNK, (c + 1) * CHUNK)
        _ssd_scan_chunk(xs[rows], bm[rows], cm[rows], dt[rows], a[rows], expand,
                        st_ref, ys_ref, c * CHUNK, causal, tril, lo_half)
    y = ys_ref[...] + xs * dskip_ref[...]
    y_ref[0] = _ssd_tail(y, z, wn_ref[...], wo_ref[...], postw_ref[...], x)

    @pl.when(t == pl.num_programs(1) - 1)
    def _():
        for gi in range(SSD_GROUPS):
            sout_ref[0, gi * SSD_HPG:(gi + 1) * SSD_HPG] = (
                st_ref[gi].T.reshape(SSD_HPG, SSD_HEADDIM, SSD_STATE))
        cout_ref[0] = cbuf_ref[0:SUBLANES, :]


def _ssd_weight_list(w, step):
    names = ("pre", "post", "win", "cw", "cb", "dtb", "alog", "dskip", "wn", "wo")
    consts = ("expand", "expand_n") if step else ("expand",)
    return [(w[k], True) for k in names] + [(w[k], False) for k in consts]


def _ssd_prompt_layer(x, s0, c0, prev_out, layer, w, *, n_pad, tile):
    bsz, seq, _ = x.shape
    s_tail = (SSD_HEADS, SSD_HEADDIM, SSD_STATE)
    c_tail = (SUBLANES, SSD_CONV_DIM)
    own = s0.shape[1] == bsz
    weights = _ssd_weight_list(w, step=False)
    carried = [] if prev_out is None else list(prev_out)
    n_in = 3 + len(weights)
    return pl.pallas_call(
        functools.partial(_ssd_prompt_kernel, n_pad=n_pad, tile=tile, has_prev=bool(carried)),
        grid=(bsz, seq // tile),
        in_specs=[pl.BlockSpec((1, tile, D_MODEL), lambda b, t: (b, t, 0)),
                  _batch_block(layer, 1, s_tail, own), _batch_block(layer, 1, c_tail, own)]
                 + _weight_specs(weights, layer)
                 + [pl.BlockSpec(memory_space=pl.ANY)] * len(carried),
        out_specs=[pl.BlockSpec((1, tile, D_MODEL), lambda b, t: (b, t, 0)),
                   _batch_block(layer, 1, s_tail), _batch_block(layer, 1, c_tail)],
        out_shape=[jax.ShapeDtypeStruct(x.shape, F32),
                   jax.ShapeDtypeStruct((N_LAYERS_PER_MIXER, bsz) + s_tail, F32),
                   jax.ShapeDtypeStruct((N_LAYERS_PER_MIXER, bsz) + c_tail, F32)],
        scratch_shapes=[pltpu.VMEM((SSD_GROUPS, SSD_STATE, SSD_GW), F32),
                        pltpu.VMEM((tile + SUBLANES, SSD_CONV_DIM), F32),
                        pltpu.VMEM((tile, SSD_DINNER), F32)],
        input_output_aliases={n_in: 1, n_in + 1: 2} if carried else {},
        compiler_params=pltpu.CompilerParams(
            dimension_semantics=("arbitrary", "arbitrary"), vmem_limit_bytes=VMEM_LIMIT),
        name="ssd_prompt",
    )(x, s0, c0, *[a for a, _ in weights], *carried)


def _column(row, width, lane0=0):
    seg = row[:, lane0:lane0 + width]
    r = lax.broadcasted_iota(jnp.int32, (width, width), 0)
    c = lax.broadcasted_iota(jnp.int32, (width, width), 1)
    return jnp.sum(jnp.where(r == c, seg, 0.0), axis=1, keepdims=True)


def _gla_step_kernel(x_ref, s_ref, prew_ref, postw_ref, win_ref, wgu_ref, bg_ref, wn_ref, wo_ref,
                     *rest, bb, has_prev):
    y_ref, sout_ref, q_s, k_s, v_s, e_s, o_s = rest[1:] if has_prev else rest
    i = pl.program_id(0)

    @pl.when(i == 0)
    def _():
        hb = _rms(x_ref[...], prew_ref[...]).astype(BF16)
        q_s[...] = _dot(hb, win_ref[:, GLA_Q0:GLA_K0]) * (GLA_HK ** -0.5)
        k_s[...] = _dot(hb, win_ref[:, GLA_K0:GLA_V0])
        v_s[...] = _dot(hb, win_ref[:, GLA_V0:GLA_R0])
        e_s[...] = jnp.exp(_gla_gates(hb, win_ref, wgu_ref, bg_ref))

    def body(j, carry):
        row = i * bb + j
        qr = q_s[pl.ds(row, 1), :]
        kr = k_s[pl.ds(row, 1), :]
        vr = v_s[pl.ds(row, 1), :]
        er = e_s[pl.ds(row, 1), :]
        for h in range(GLA_HEADS):
            vs = slice(h * GLA_HV, (h + 1) * GLA_HV)
            ecol = _column(er, GLA_HK, h * GLA_HK)
            kcol = _column(kr, GLA_HK, h * GLA_HK)
            qcol = _column(qr, GLA_HK, h * GLA_HK)
            s_new = s_ref[j, h] * ecol + kcol * vr[:, vs]
            sout_ref[j, h] = s_new
            o_s[pl.ds(row, 1), vs] = jnp.sum(qcol * s_new, axis=0, keepdims=True)
        return carry

    lax.fori_loop(0, bb, body, 0)

    @pl.when(i == pl.num_programs(0) - 1)
    def _():
        x = x_ref[...]
        hb = _rms(x, prew_ref[...]).astype(BF16)
        r = _dot(hb, win_ref[:, GLA_R0:GLA_G0])
        y_ref[...] = _gla_tail(o_s[...], r, wn_ref[...], wo_ref[...], postw_ref[...], x)


def _gla_step_layer(x, states, prev_out, layer, w, *, bb=8):
    bsz = x.shape[0]
    weights = _gla_weight_list(w)
    sblk = _batch_block(layer, bb, (GLA_HEADS, GLA_HK, GLA_HV))
    carried = [] if prev_out is None else [prev_out]
    return pl.pallas_call(
        functools.partial(_gla_step_kernel, bb=bb, has_prev=bool(carried)),
        grid=(bsz // bb,),
        in_specs=[_full(x.shape), sblk] + _weight_specs(weights, layer)
                 + [pl.BlockSpec(memory_space=pl.ANY)] * len(carried),
        out_specs=[pl.BlockSpec(x.shape, lambda i: (0, 0)), sblk],
        out_shape=[jax.ShapeDtypeStruct(x.shape, F32), jax.ShapeDtypeStruct(states.shape, F32)],
        scratch_shapes=[pltpu.VMEM((bsz, GLA_DK), F32), pltpu.VMEM((bsz, GLA_DK), F32),
                        pltpu.VMEM((bsz, GLA_DV), F32), pltpu.VMEM((bsz, GLA_DK), F32),
                        pltpu.VMEM((bsz, GLA_DV), F32)],
        input_output_aliases={2 + len(weights): 1} if carried else {},
        compiler_params=pltpu.CompilerParams(
            dimension_semantics=("arbitrary",), vmem_limit_bytes=VMEM_LIMIT),
        name="gla_step",
    )(x, states, *[a for a, _ in weights], *carried)


def _ssd_step_kernel(x_ref, s_ref, cv_ref, prew_ref, postw_ref, win_ref, cw_ref, cb_ref, dtb_ref,
                     alog_ref, dskip_ref, wn_ref, wo_ref, expand_ref, expand_n_ref, *rest,
                     bb, has_prev):
    (y_ref, sout_ref, cvout_ref, xs_s, xl_s, ct_s, b_s, e_s, yt_s) = rest[2:] if has_prev else rest
    i = pl.program_id(0)
    bsz = x_ref.shape[0]
    cd = SSD_CONV_DIM

    @pl.when(i == 0)
    def _():
        hb = _rms(x_ref[...], prew_ref[...]).astype(BF16)
        xbc = _dot(hb, win_ref[:, SSD_X0:SSD_DT0])
        dtr = _narrow_proj(hb, win_ref, SSD_DT0, SSD_HEADS)
        cw = cw_ref[...]
        conv = cb_ref[...] + xbc * cw[3:4, :]
        for t in range(SSD_CONV - 1):
            conv = conv + cv_ref[:, t * cd:(t + 1) * cd] * cw[t:t + 1, :]
        cvout_ref[:, 0:cd] = cv_ref[:, cd:2 * cd]
        cvout_ref[:, cd:2 * cd] = cv_ref[:, 2 * cd:3 * cd]
        cvout_ref[:, 2 * cd:3 * cd] = xbc
        xc = _silu(conv)
        xs = xc[:, :SSD_DINNER]
        dt = _softplus(dtr + dtb_ref[...])
        ea = jnp.exp(dt * (-jnp.exp(alog_ref[...])))
        xs_s[...] = xs
        xd_t = (xs * _dot_exact_rhs(dt, expand_ref[...])).T
        c_t = xc[:, SSD_DINNER + SSD_GROUPS * SSD_STATE:].T
        for p in range(SSD_GROUPS // 2):
            for e in range(2):
                gi = 2 * p + e
                xl_s[p, :, e * bsz:(e + 1) * bsz] = (
                    xd_t[gi * SSD_GW:(gi + 1) * SSD_GW].astype(BF16))
        ct_s[...] = c_t
        b_s[...] = xc[:, SSD_DINNER:SSD_DINNER + SSD_GROUPS * SSD_STATE]
        e_s[...] = _dot_exact_rhs(ea, expand_n_ref[...])
        yt_s[...] = jnp.zeros(yt_s.shape, F32)

    sub = lax.broadcasted_iota(jnp.int32, (bsz, SSD_STATE), 0)
    lane = lax.broadcasted_iota(jnp.int32, (SSD_STATE, bsz), 1)
    zero_bn = jnp.zeros((bsz, SSD_STATE), BF16)
    zero_nb = jnp.zeros((SSD_STATE, bsz), BF16)

    for p in range(SSD_GROUPS // 2):
        y_acc = None
        for j in range(bb):
            row = i * bb + j
            er = e_s[pl.ds(row, 1), :]
            bsel, csel = [], []
            for e in range(2):
                ns = slice((2 * p + e) * SSD_STATE, (2 * p + e + 1) * SSD_STATE)
                bsel.append(jnp.where(sub == row, b_s[:, ns], 0.0).astype(BF16))
                csel.append(jnp.where(lane == row, ct_s[ns, :], 0.0).astype(BF16))
            rhs_b = jnp.concatenate([jnp.concatenate([bsel[0], zero_bn], axis=1),
                                     jnp.concatenate([zero_bn, bsel[1]], axis=1)], axis=0)
            rhs_c = jnp.concatenate([jnp.concatenate([csel[0], zero_nb], axis=1),
                                     jnp.concatenate([zero_nb, csel[1]], axis=1)], axis=0)
            outer = _dot(xl_s[p], rhs_b)
            halves = []
            for e in range(2):
                pieces = []
                for r in range(SSD_HPG):
                    h = (2 * p + e) * SSD_HPG + r
                    s_new = (s_ref[j, h] * er[:, h * SSD_STATE:(h + 1) * SSD_STATE]
                             + outer[r * SSD_HEADDIM:(r + 1) * SSD_HEADDIM,
                                     e * SSD_STATE:(e + 1) * SSD_STATE])
                    sout_ref[j, h] = s_new
                    pieces.append(s_new.astype(BF16))
                halves.append(jnp.concatenate(pieces, axis=0))
            y_j = _dot(jnp.concatenate(halves, axis=1), rhs_c)
            y_acc = y_j if y_acc is None else y_acc + y_j
        yt_s[p] += y_acc

    @pl.when(i == pl.num_programs(0) - 1)
    def _():
        x = x_ref[...]
        hb = _rms(x, prew_ref[...]).astype(BF16)
        z = _dot(hb, win_ref[:, SSD_Z0:SSD_X0])
        y = jnp.concatenate([yt_s[p, :, e * bsz:(e + 1) * bsz].T
                             for p in range(SSD_GROUPS // 2) for e in range(2)], axis=1)
        y = y + xs_s[...] * dskip_ref[...]
        y_ref[...] = _ssd_tail(y, z, wn_ref[...], wo_ref[...], postw_ref[...], x)


def _ssd_step_layer(x, states, convs, prev_out, layer, w, *, bb=4):
    bsz = x.shape[0]
    weights = _ssd_weight_list(w, step=True)
    sblk = _batch_block(layer, bb, (SSD_HEADS, SSD_HEADDIM, SSD_STATE))
    cblk = pl.BlockSpec((None,) + convs.shape[1:], lambda i: (layer, 0, 0))
    carried = [] if prev_out is None else list(prev_out)
    n_in = 3 + len(weights)
    return pl.pallas_call(
        functools.partial(_ssd_step_kernel, bb=bb, has_prev=bool(carried)),
        grid=(bsz // bb,),
        in_specs=[_full(x.shape), sblk,
                  pl.BlockSpec(cblk.block_shape, cblk.index_map, pipeline_mode=pl.Buffered(1))]
                 + _weight_specs(weights, layer)
                 + [pl.BlockSpec(memory_space=pl.ANY)] * len(carried),
        out_specs=[pl.BlockSpec(x.shape, lambda i: (0, 0)), sblk, cblk],
        out_shape=[jax.ShapeDtypeStruct(x.shape, F32), jax.ShapeDtypeStruct(states.shape, F32),
                   jax.ShapeDtypeStruct(convs.shape, F32)],
        input_output_aliases={n_in: 1, n_in + 1: 2} if carried else {},
        scratch_shapes=[pltpu.VMEM((bsz, SSD_DINNER), F32),
                        pltpu.VMEM((SSD_GROUPS // 2, SSD_GW, 2 * bsz), BF16),
                        pltpu.VMEM((SSD_GROUPS * SSD_STATE, bsz), F32),
                        pltpu.VMEM((bsz, SSD_GROUPS * SSD_STATE), F32),
                        pltpu.VMEM((bsz, SSD_HEADS * SSD_STATE), F32),
                        pltpu.VMEM((SSD_GROUPS // 2, SSD_GW, 2 * bsz), F32)],
        compiler_params=pltpu.CompilerParams(
            dimension_semantics=("arbitrary",), vmem_limit_bytes=VMEM_LIMIT),
        name="ssd_step",
    )(x, states, convs, *[a for a, _ in weights], *carried)


def _pad_last(a, n):
    return jnp.pad(a, [(0, 0)] * (a.ndim - 1) + [(0, n - a.shape[-1])])


def _rows(a):
    return a[:, None, :]


def _gla_weights(pre, post, w_in, w_gate_up, b_gate, w_norm, w_out):
    return {
        "pre": _rows(pre), "post": _rows(post),
        "win": w_in.astype(BF16),
        "wgu": jnp.pad(w_gate_up, ((0, 0), (0, LANES - GLA_RANK), (0, 0))).astype(BF16),
        "bg": _rows(b_gate), "wn": _rows(w_norm), "wo": w_out.astype(BF16),
    }


def _ssd_weights(pre, post, w_in, conv_w, conv_b, dt_bias, a_log, d_skip, w_norm, w_out):
    head_of_lane = jnp.arange(SSD_DINNER) // SSD_HEADDIM
    head_of_lane_n = jnp.arange(SSD_HEADS * SSD_STATE) // SSD_STATE
    heads = jnp.arange(LANES)[:, None]
    return {
        "pre": _rows(pre), "post": _rows(post),
        "win": w_in.astype(BF16),
        "cw": jnp.pad(conv_w, ((0, 0), (0, SUBLANES - SSD_CONV), (0, 0))), "cb": _rows(conv_b),
        "dtb": _rows(_pad_last(dt_bias, LANES)), "alog": _rows(_pad_last(a_log, LANES)),
        "dskip": _rows(jnp.repeat(d_skip, SSD_HEADDIM, axis=1)),
        "wn": _rows(w_norm), "wo": w_out.astype(BF16),
        "expand": (heads == head_of_lane[None, :]).astype(BF16),
        "expand_n": (heads == head_of_lane_n[None, :]).astype(BF16),
    }


def _prompt_trunk(x, gla0, ssm0, conv0, gw, sw, *, n_pad, gla_tile, ssd_tile):
    gla_s = ssm_conv_s = None
    for j in range(N_LAYERS_PER_MIXER):
        x, gla_s = _gla_prompt_layer(x, gla0, gla_s, j, gw, n_pad=n_pad, tile=gla_tile)
        x, *ssm_conv_s = _ssd_prompt_layer(x, ssm0, conv0, ssm_conv_s, j, sw,
                                           n_pad=n_pad, tile=ssd_tile)
    return x, gla_s, ssm_conv_s[0], ssm_conv_s[1]


def kernel(x_prompt, x_sample, state_gla, state_ssm, state_conv, meta_tokens, pre_norm, post_norm,
           gla_w_in, gla_w_gate_up, gla_b_gate, gla_w_norm, gla_w_out,
           ssd_w_in, ssd_conv_w, ssd_conv_b, ssd_dt_bias, ssd_a_log, ssd_d_skip, ssd_w_norm, ssd_w_out):
    gw = _gla_weights(pre_norm[0::2], post_norm[0::2], gla_w_in, gla_w_gate_up, gla_b_gate,
                      gla_w_norm, gla_w_out)
    sw = _ssd_weights(pre_norm[1::2], post_norm[1::2], ssd_w_in, ssd_conv_w, ssd_conv_b,
                      ssd_dt_bias, ssd_a_log, ssd_d_skip, ssd_w_norm, ssd_w_out)

    n_pad = CHUNK - N_META
    x_meta = jnp.pad(meta_tokens.astype(F32), ((n_pad, 0), (0, 0)))[None]
    zg = jnp.zeros((N_LAYERS_PER_MIXER, 1, GLA_HEADS, GLA_HK, GLA_HV), F32)
    zs = jnp.zeros((N_LAYERS_PER_MIXER, 1, SSD_HEADS, SSD_HEADDIM, SSD_STATE), F32)
    zc = jnp.zeros((N_LAYERS_PER_MIXER, 1, SUBLANES, SSD_CONV_DIM), F32)
    _, mg, ms, mc = _prompt_trunk(x_meta, zg, zs, zc, gw, sw,
                                  n_pad=n_pad, gla_tile=CHUNK, ssd_tile=CHUNK)
    y_prompt, gla_p, ssm_p, conv_p = _prompt_trunk(x_prompt, mg, ms, mc, gw, sw, n_pad=0,
                                                   gla_tile=GLA_TILE, ssd_tile=SSD_TILE)
    conv_p = conv_p[:, :, SUBLANES - (SSD_CONV - 1):, :]

    xs = x_sample[:, 0, :]
    sbsz = xs.shape[0]
    convs = state_conv.reshape(N_LAYERS_PER_MIXER, sbsz, (SSD_CONV - 1) * SSD_CONV_DIM)
    gla_s = ssm_conv_s = None
    for j in range(N_LAYERS_PER_MIXER):
        xs, gla_s = _gla_step_layer(xs, state_gla, gla_s, j, gw)
        xs, *ssm_conv_s = _ssd_step_layer(xs, state_ssm, convs, ssm_conv_s, j, sw)
    ssm_s, conv_s = ssm_conv_s
    y_sample = xs[:, None, :]
    return (y_prompt, y_sample, gla_p, ssm_p, conv_p, gla_s, ssm_s,
            conv_s.reshape(N_LAYERS_PER_MIXER, sbsz, SSD_CONV - 1, SSD_CONV_DIM))
```

```python
import functools

import jax
import jax.numpy as jnp
from jax import lax
from jax.experimental import pallas as pl
from jax.experimental.pallas import tpu as pltpu

F32 = jnp.float32
BF16 = jnp.bfloat16

D_MODEL = 1024
N_META = 16
NORM_EPS = 1e-6
N_LAYERS_PER_MIXER = 2

GLA_HEADS = 4
GLA_DK = 512
GLA_DV = 1024
GLA_HK = 128
GLA_HV = 256
GLA_RANK = 16
GLA_TAU = 16.0
GLA_SAFE_DROP = 40.0

SSD_DINNER = 2048
SSD_HEADDIM = 64
SSD_HEADS = 32
SSD_GROUPS = 4
SSD_HPG = 8
SSD_STATE = 128
SSD_CONV = 4
SSD_CONV_DIM = 3072
SSD_GW = SSD_HPG * SSD_HEADDIM

LANES = 128
SUBLANES = 8
CHUNK = 128
GLA_TILE = 1024
SSD_TILE = 512
VMEM_LIMIT = 56 * 1024 * 1024

GLA_Q0, GLA_K0, GLA_V0, GLA_R0, GLA_G0 = 0, GLA_DK, 2 * GLA_DK, 2 * GLA_DK + GLA_DV, 2 * GLA_DK + 2 * GLA_DV
SSD_Z0, SSD_X0, SSD_DT0 = 0, SSD_DINNER, SSD_DINNER + SSD_CONV_DIM


def _narrow_proj(hb, w_ref, start, width):
    res = jnp.dot(hb, w_ref[:, start:start + width], preferred_element_type=F32)
    return jnp.concatenate([res, jnp.zeros((res.shape[0], LANES - width), F32)], axis=1)


def _dot(a, b):
    return jnp.dot(a, b, preferred_element_type=F32)


def _dot_nt(a, b):
    return lax.dot_general(a, b, (((1,), (1,)), ((), ())), preferred_element_type=F32)


def _dot_tn(a, b):
    return lax.dot_general(a, b, (((0,), (0,)), ((), ())), preferred_element_type=F32)


def _rms(x, w):
    return x * lax.rsqrt(jnp.mean(x * x, axis=-1, keepdims=True) + NORM_EPS) * w


def _silu(x):
    h = 0.5 * x
    return h + h * jnp.tanh(h)


def _softplus(x):
    return jnp.maximum(x, 0.0) + jnp.log1p(jnp.exp(-jnp.abs(x)))


def _log_sigmoid(x):
    return -_softplus(-x)


def _tri(n):
    r = lax.broadcasted_iota(jnp.int32, (n, n), 0)
    c = lax.broadcasted_iota(jnp.int32, (n, n), 1)
    return r >= c


def _split3(x):
    hi = x.astype(BF16)
    r1 = x - hi.astype(F32)
    mid = r1.astype(BF16)
    lo = (r1 - mid.astype(F32)).astype(BF16)
    return hi, mid, lo


def _dot_exact_rhs(x, m_bf16):
    hi, mid, lo = _split3(x)
    return _dot(hi, m_bf16) + _dot(mid, m_bf16) + _dot(lo, m_bf16)


def _cumsum_rows(tril_bf16, x):
    hi, mid, lo = _split3(x)
    return _dot(tril_bf16, hi) + _dot(tril_bf16, mid) + _dot(tril_bf16, lo)


def _full(shape):
    nd = len(shape)
    return pl.BlockSpec(shape, lambda *_: (0,) * nd, pipeline_mode=pl.Buffered(1))


def _of_layer(a, layer):
    nz = a.ndim - 1
    return pl.BlockSpec((None,) + a.shape[1:], lambda *_: (layer,) + (0,) * nz,
                        pipeline_mode=pl.Buffered(1))


def _weight_specs(weights, layer):
    return [_of_layer(a, layer) if stacked else _full(a.shape) for a, stacked in weights]


def _batch_block(layer, bb, tail, own=True):
    nz = len(tail)
    return pl.BlockSpec((None, bb) + tail,
                        lambda i, *_: (layer, i if own else 0) + (0,) * nz)


def _gla_attn_pairwise(qf, kf, bc, causal, kbuf_ref, bbuf_ref):
    n = qf.shape[0]
    kbuf_ref[...] = kf
    bbuf_ref[...] = bc
    lane = lax.broadcasted_iota(jnp.int32, (n, n), 1)

    def body(j, acc):
        kj = kbuf_ref[pl.ds(j, 1), :]
        bj = bbuf_ref[pl.ds(j, 1), :]
        col = jnp.sum(qf * jnp.exp(jnp.minimum(bc - bj, 0.0)) * kj, axis=1, keepdims=True)
        return jnp.where(lane == j, col, acc)

    acc = lax.fori_loop(0, n, body, jnp.zeros((n, n), F32))
    return jnp.where(causal, acc, 0.0)


def _gla_scan_tile(q, k, v, g, states, pairwise, kbuf_ref, bbuf_ref):
    causal = _tri(CHUNK)
    tril = causal.astype(BF16)
    states = list(states)
    outs = []
    for c in range(q.shape[0] // CHUNK):
        rows = slice(c * CHUNK, (c + 1) * CHUNK)
        bc = _cumsum_rows(tril, g[rows])
        b_last = bc[-1:, :]
        e_last = jnp.exp(b_last)
        qh = (q[rows] * jnp.exp(bc)).astype(BF16)
        if pairwise:
            kd = (k[rows] * jnp.exp(b_last - bc)).astype(BF16)
        else:
            kh = k[rows] * jnp.exp(-bc)
            kd = (kh * e_last).astype(BF16)
            kh = kh.astype(BF16)
        vb = v[rows].astype(BF16)
        parts = []
        for h in range(GLA_HEADS):
            ks = slice(h * GLA_HK, (h + 1) * GLA_HK)
            vs = slice(h * GLA_HV, (h + 1) * GLA_HV)
            if pairwise:
                attn = _gla_attn_pairwise(q[rows, ks], k[rows, ks], bc[:, ks], causal,
                                          kbuf_ref, bbuf_ref)
            else:
                attn = jnp.where(causal, _dot_nt(qh[:, ks], kh[:, ks]), 0.0)
            st = states[h]
            parts.append(_dot(attn.astype(BF16), vb[:, vs]) + _dot_nt(qh[:, ks], st.astype(BF16)))
            states[h] = st * e_last[:, ks] + _dot_tn(vb[:, vs], kd[:, ks])
        outs.append(jnp.concatenate(parts, axis=1))
    o = jnp.concatenate(outs, axis=0) if len(outs) > 1 else outs[0]
    return o, states


def _gla_tail(o, r, wn, wo, postw, x):
    parts = []
    for h in range(GLA_HEADS):
        vs = slice(h * GLA_HV, (h + 1) * GLA_HV)
        parts.append(_rms(o[:, vs], wn[:, vs]))
    on = jnp.concatenate(parts, axis=1) * _silu(r)
    y = _dot(on.astype(BF16), wo)
    return x + _rms(y, postw)


def _gla_gates(hb, win_ref, wgu_ref, bg_ref):
    gl = _narrow_proj(hb, win_ref, GLA_G0, GLA_RANK)
    return _log_sigmoid(_dot(gl.astype(BF16), wgu_ref[...]) + bg_ref[...]) * (1.0 / GLA_TAU)


def _gla_prompt_kernel(x_ref, s0_ref, prew_ref, postw_ref, win_ref, wgu_ref, bg_ref, wn_ref,
                       wo_ref, *rest, n_pad, tile, has_prev):
    y_ref, sout_ref, st_ref, o_ref, kbuf_ref, bbuf_ref = rest[1:] if has_prev else rest
    t = pl.program_id(1)

    @pl.when(t == 0)
    def _():
        for h in range(GLA_HEADS):
            st_ref[h] = s0_ref[0, h].T

    x = x_ref[0]
    hn = _rms(x, prew_ref[...])
    if n_pad:
        row = lax.broadcasted_iota(jnp.int32, (tile, 1), 0) + t * tile
        hn = jnp.where(row >= n_pad, hn, 0.0)
    hb = hn.astype(BF16)
    q = _dot(hb, win_ref[:, GLA_Q0:GLA_K0]) * (GLA_HK ** -0.5)
    k = _dot(hb, win_ref[:, GLA_K0:GLA_V0])
    v = _dot(hb, win_ref[:, GLA_V0:GLA_R0])
    g = _gla_gates(hb, win_ref, wgu_ref, bg_ref)

    drop = None
    for c in range(tile // CHUNK):
        tot = jnp.sum(g[c * CHUNK:(c + 1) * CHUNK], axis=0, keepdims=True)
        drop = tot if drop is None else jnp.minimum(drop, tot)
    safe = jnp.min(drop) >= -GLA_SAFE_DROP

    def scan(pairwise):
        o, states = _gla_scan_tile(q, k, v, g, [st_ref[h] for h in range(GLA_HEADS)], pairwise,
                                   kbuf_ref, bbuf_ref)
        o_ref[...] = o
        for h in range(GLA_HEADS):
            st_ref[h] = states[h]

    pl.when(safe)(functools.partial(scan, False))
    pl.when(jnp.logical_not(safe))(functools.partial(scan, True))
    r = _dot(hb, win_ref[:, GLA_R0:GLA_G0])
    y_ref[0] = _gla_tail(o_ref[...], r, wn_ref[...], wo_ref[...], postw_ref[...], x)

    @pl.when(t == pl.num_programs(1) - 1)
    def _():
        for h in range(GLA_HEADS):
            sout_ref[0, h] = st_ref[h].T


def _gla_weight_list(w):
    return [(w[k], True) for k in ("pre", "post", "win", "wgu", "bg", "wn", "wo")]


def _gla_prompt_layer(x, s0, prev_out, layer, w, *, n_pad, tile):
    bsz, seq, _ = x.shape
    tail = (GLA_HEADS, GLA_HK, GLA_HV)
    weights = _gla_weight_list(w)
    carried = [] if prev_out is None else [prev_out]
    return pl.pallas_call(
        functools.partial(_gla_prompt_kernel, n_pad=n_pad, tile=tile, has_prev=bool(carried)),
        grid=(bsz, seq // tile),
        in_specs=[pl.BlockSpec((1, tile, D_MODEL), lambda b, t: (b, t, 0)),
                  _batch_block(layer, 1, tail, own=s0.shape[1] == bsz)]
                 + _weight_specs(weights, layer)
                 + [pl.BlockSpec(memory_space=pl.ANY)] * len(carried),
        out_specs=[pl.BlockSpec((1, tile, D_MODEL), lambda b, t: (b, t, 0)),
                   _batch_block(layer, 1, tail)],
        out_shape=[jax.ShapeDtypeStruct(x.shape, F32),
                   jax.ShapeDtypeStruct((N_LAYERS_PER_MIXER, bsz) + tail, F32)],
        scratch_shapes=[pltpu.VMEM((GLA_HEADS, GLA_HV, GLA_HK), F32),
                        pltpu.VMEM((tile, GLA_DV), F32),
                        pltpu.VMEM((CHUNK, GLA_HK), F32),
                        pltpu.VMEM((CHUNK, GLA_HK), F32)],
        input_output_aliases={2 + len(weights): 1} if carried else {},
        compiler_params=pltpu.CompilerParams(
            dimension_semantics=("arbitrary", "arbitrary"), vmem_limit_bytes=VMEM_LIMIT),
        name="gla_prompt",
    )(x, s0, *[a for a, _ in weights], *carried)


def _ssd_conv(xbc, cbuf_ref, cw, cb, tile):
    cbuf_ref[SUBLANES:SUBLANES + tile, :] = xbc
    conv = cb + xbc * cw[3:4, :]
    for i in range(SSD_CONV - 1):
        off = SUBLANES - (SSD_CONV - 1) + i
        conv = conv + cbuf_ref[off:off + tile, :] * cw[i:i + 1, :]
    return _silu(conv)


def _ssd_scan_chunk(xs, bm, cm, dt, a, expand, st_ref, y_ref, row0, causal, tril, lo_half):
    ac = _cumsum_rows(tril, a)
    ac_t = ac.T
    dt_t = dt.T
    a_last = ac[-1:, :]
    wgt = jnp.exp(a_last - ac) * dt
    xw = (xs * _dot(wgt.astype(BF16), expand)).astype(BF16)
    e_last = _dot_exact_rhs(jnp.exp(a_last), expand)
    pair = 2 * SSD_HEADDIM
    for gi in range(SSD_GROUPS):
        ns = slice(gi * SSD_STATE, (gi + 1) * SSD_STATE)
        gs = slice(gi * SSD_GW, (gi + 1) * SSD_GW)
        bg = bm[:, ns].astype(BF16)
        cg = cm[:, ns].astype(BF16)
        cb = _dot_nt(cg, bg)
        st = st_ref[gi]
        y_int = _dot(cg, st.astype(BF16))
        for qi in range(SSD_HPG // 2):
            h0 = gi * SSD_HPG + 2 * qi
            ps = slice(h0 * SSD_HEADDIM, h0 * SSD_HEADDIM + pair)
            ms, eas = [], []
            for e in range(2):
                h = h0 + e
                col = jnp.broadcast_to(ac[:, h:h + 1], (CHUNK, CHUNK))
                seg = col - ac_t[h:h + 1, :]
                ms.append(jnp.where(causal, cb * jnp.exp(seg) * dt_t[h:h + 1, :], 0.0).astype(BF16))
                eas.append(jnp.exp(col))
            xp = xs[:, ps]
            rhs = jnp.concatenate([jnp.where(lo_half, xp, 0.0).astype(BF16),
                                   jnp.where(lo_half, 0.0, xp).astype(BF16)], axis=0)
            ea = jnp.where(lo_half, eas[0], eas[1])
            y_ref[row0:row0 + CHUNK, ps] = (_dot(jnp.concatenate(ms, axis=1), rhs)
                                            + y_int[:, qi * pair:(qi + 1) * pair] * ea)
        st_ref[gi] = st * e_last[:, gs] + _dot_tn(bg, xw[:, gs])


def _ssd_tail(y, z, wn, wo, postw, x):
    y = y * _silu(z)
    parts = []
    for gi in range(SSD_GROUPS):
        gs = slice(gi * SSD_GW, (gi + 1) * SSD_GW)
        parts.append(_rms(y[:, gs], wn[:, gs]))
    yn = jnp.concatenate(parts, axis=1)
    out = _dot(yn.astype(BF16), wo)
    return x + _rms(out, postw)


def _ssd_prompt_kernel(x_ref, s0_ref, c0_ref, prew_ref, postw_ref, win_ref, cw_ref, cb_ref,
                       dtb_ref, alog_ref, dskip_ref, wn_ref, wo_ref, expand_ref, *rest,
                       n_pad, tile, has_prev):
    y_ref, sout_ref, cout_ref, st_ref, cbuf_ref, ys_ref = rest[2:] if has_prev else rest
    t = pl.program_id(1)

    @pl.when(t == 0)
    def _():
        for gi in range(SSD_GROUPS):
            s0 = s0_ref[0, gi * SSD_HPG:(gi + 1) * SSD_HPG].reshape(SSD_GW, SSD_STATE)
            st_ref[gi] = s0.T
        cbuf_ref[0:SUBLANES, :] = c0_ref[0]

    x = x_ref[0]
    hn = _rms(x, prew_ref[...])
    row = lax.broadcasted_iota(jnp.int32, (tile, 1), 0) + t * tile
    if n_pad:
        hn = jnp.where(row >= n_pad, hn, 0.0)
    hb = hn.astype(BF16)
    xbc = _dot(hb, win_ref[:, SSD_X0:SSD_DT0])
    dtr = _narrow_proj(hb, win_ref, SSD_DT0, SSD_HEADS)
    xc = _ssd_conv(xbc, cbuf_ref, cw_ref[...], cb_ref[...], tile)
    z = _dot(hb, win_ref[:, SSD_Z0:SSD_X0])
    cbuf_ref[0:SUBLANES, :] = cbuf_ref[tile:tile + SUBLANES, :]
    xs = xc[:, :SSD_DINNER]
    bm = xc[:, SSD_DINNER:SSD_DINNER + SSD_GROUPS * SSD_STATE]
    cm = xc[:, SSD_DINNER + SSD_GROUPS * SSD_STATE:]
    dt = _softplus(dtr + dtb_ref[...])
    if n_pad:
        dt = jnp.where(row >= n_pad, dt, 0.0)
    a = dt * (-jnp.exp(alog_ref[...]))

    causal = _tri(CHUNK)
    tril = causal.astype(BF16)
    lo_half = lax.broadcasted_iota(jnp.int32, (CHUNK, 2 * SSD_HEADDIM), 1) < SSD_HEADDIM
    expand = expand_ref[...]
    for c in range(tile // CHUNK):
        rows = slice(c * CHUNK, (c + 1) * CHUNK)
        _ssd_scan_chunk(xs[rows], bm[rows], cm[rows], dt[rows], a[rows], expand,
                        st_ref, ys_ref, c * CHUNK, causal, tril, lo_half)
    y = ys_ref[...] + xs * dskip_ref[...]
    y_ref[0] = _ssd_tail(y, z, wn_ref[...], wo_ref[...], postw_ref[...], x)

    @pl.when(t == pl.num_programs(1) - 1)
    def _():
        for gi in range(SSD_GROUPS):
            sout_ref[0, gi * SSD_HPG:(gi + 1) * SSD_HPG] = (
                st_ref[gi].T.reshape(SSD_HPG, SSD_HEADDIM, SSD_STATE))
        cout_ref[0] = cbuf_ref[0:SUBLANES, :]


def _ssd_weight_list(w, step):
    names = ("pre", "post", "win", "cw", "cb", "dtb", "alog", "dskip", "wn", "wo")
    consts = ("expand", "expand_n") if step else ("expand",)
    return [(w[k], True) for k in names] + [(w[k], False) for k in consts]


def _ssd_prompt_layer(x, s0, c0, prev_out, layer, w, *, n_pad, tile):
    bsz, seq, _ = x.shape
    s_tail = (SSD_HEADS, SSD_HEADDIM, SSD_STATE)
    c_tail = (SUBLANES, SSD_CONV_DIM)
    own = s0.shape[1] == bsz
    weights = _ssd_weight_list(w, step=False)
    carried = [] if prev_out is None else list(prev_out)
    n_in = 3 + len(weights)
    return pl.pallas_call(
        functools.partial(_ssd_prompt_kernel, n_pad=n_pad, tile=tile, has_prev=bool(carried)),
        grid=(bsz, seq // tile),
        in_specs=[pl.BlockSpec((1, tile, D_MODEL), lambda b, t: (b, t, 0)),
                  _batch_block(layer, 1, s_tail, own), _batch_block(layer, 1, c_tail, own)]
                 + _weight_specs(weights, layer)
                 + [pl.BlockSpec(memory_space=pl.ANY)] * len(carried),
        out_specs=[pl.BlockSpec((1, tile, D_MODEL), lambda b, t: (b, t, 0)),
                   _batch_block(layer, 1, s_tail), _batch_block(layer, 1, c_tail)],
        out_shape=[jax.ShapeDtypeStruct(x.shape, F32),
                   jax.ShapeDtypeStruct((N_LAYERS_PER_MIXER, bsz) + s_tail, F32),
                   jax.ShapeDtypeStruct((N_LAYERS_PER_MIXER, bsz) + c_tail, F32)],
        scratch_shapes=[pltpu.VMEM((SSD_GROUPS, SSD_STATE, SSD_GW), F32),
                        pltpu.VMEM((tile + SUBLANES, SSD_CONV_DIM), F32),
                        pltpu.VMEM((tile, SSD_DINNER), F32)],
        input_output_aliases={n_in: 1, n_in + 1: 2} if carried else {},
        compiler_params=pltpu.CompilerParams(
            dimension_semantics=("arbitrary", "arbitrary"), vmem_limit_bytes=VMEM_LIMIT),
        name="ssd_prompt",
    )(x, s0, c0, *[a for a, _ in weights], *carried)


def _column(row, width, lane0=0):
    seg = row[:, lane0:lane0 + width]
    r = lax.broadcasted_iota(jnp.int32, (width, width), 0)
    c = lax.broadcasted_iota(jnp.int32, (width, width), 1)
    return jnp.sum(jnp.where(r == c, seg, 0.0), axis=1, keepdims=True)


def _gla_step_kernel(x_ref, s_ref, prew_ref, postw_ref, win_ref, wgu_ref, bg_ref, wn_ref, wo_ref,
                     *rest, bb, has_prev):
    y_ref, sout_ref, q_s, kl_s, v_s, e_s, o_s = rest[1:] if has_prev else rest
    i = pl.program_id(0)
    bsz = x_ref.shape[0]

    @pl.when(i == 0)
    def _():
        hb = _rms(x_ref[...], prew_ref[...]).astype(BF16)
        q_s[...] = _dot(hb, win_ref[:, GLA_Q0:GLA_K0]) * (GLA_HK ** -0.5)
        o_s[:, 0:GLA_DK] = _dot(hb, win_ref[:, GLA_K0:GLA_V0])
        k_t = o_s[:, 0:GLA_DK].T
        for h in range(GLA_HEADS):
            kl_s[h // 2, :, (h % 2) * bsz:(h % 2 + 1) * bsz] = (
                k_t[h * GLA_HK:(h + 1) * GLA_HK].astype(BF16))
        v_s[...] = _dot(hb, win_ref[:, GLA_V0:GLA_R0])
        e_s[...] = jnp.exp(_gla_gates(hb, win_ref, wgu_ref, bg_ref))

    sub = lax.broadcasted_iota(jnp.int32, (bsz, GLA_HV), 0)
    zero_bv = jnp.zeros((bsz, GLA_HV), BF16)

    for j in range(bb):
        row = i * bb + j
        qr = q_s[pl.ds(row, 1), :]
        er = e_s[pl.ds(row, 1), :]
        for p in range(GLA_HEADS // 2):
            vsel = [jnp.where(sub == row, v_s[:, (2 * p + e) * GLA_HV:(2 * p + e + 1) * GLA_HV],
                              0.0).astype(BF16) for e in range(2)]
            rhs = jnp.concatenate([jnp.concatenate([vsel[0], zero_bv], axis=1),
                                   jnp.concatenate([zero_bv, vsel[1]], axis=1)], axis=0)
            kv = _dot(kl_s[p], rhs)
            for e in range(2):
                h = 2 * p + e
                vs = slice(h * GLA_HV, (h + 1) * GLA_HV)
                ecol = _column(er, GLA_HK, h * GLA_HK)
                qcol = _column(qr, GLA_HK, h * GLA_HK)
                s_new = s_ref[j, h] * ecol + kv[:, e * GLA_HV:(e + 1) * GLA_HV]
                sout_ref[j, h] = s_new
                o_s[pl.ds(row, 1), vs] = jnp.sum(qcol * s_new, axis=0, keepdims=True)

    @pl.when(i == pl.num_programs(0) - 1)
    def _():
        x = x_ref[...]
        hb = _rms(x, prew_ref[...]).astype(BF16)
        r = _dot(hb, win_ref[:, GLA_R0:GLA_G0])
        y_ref[...] = _gla_tail(o_s[...], r, wn_ref[...], wo_ref[...], postw_ref[...], x)


def _gla_step_layer(x, states, prev_out, layer, w, *, bb=8):
    bsz = x.shape[0]
    weights = _gla_weight_list(w)
    sblk = _batch_block(layer, bb, (GLA_HEADS, GLA_HK, GLA_HV))
    carried = [] if prev_out is None else [prev_out]
    return pl.pallas_call(
        functools.partial(_gla_step_kernel, bb=bb, has_prev=bool(carried)),
        grid=(bsz // bb,),
        in_specs=[_full(x.shape), sblk] + _weight_specs(weights, layer)
                 + [pl.BlockSpec(memory_space=pl.ANY)] * len(carried),
        out_specs=[pl.BlockSpec(x.shape, lambda i: (0, 0)), sblk],
        out_shape=[jax.ShapeDtypeStruct(x.shape, F32), jax.ShapeDtypeStruct(states.shape, F32)],
        scratch_shapes=[pltpu.VMEM((bsz, GLA_DK), F32),
                        pltpu.VMEM((GLA_HEADS // 2, GLA_HK, 2 * bsz), BF16),
                        pltpu.VMEM((bsz, GLA_DV), F32),
                        pltpu.VMEM((bsz, GLA_DK), F32),
                        pltpu.VMEM((bsz, GLA_DV), F32)],
        input_output_aliases={2 + len(weights): 1} if carried else {},
        compiler_params=pltpu.CompilerParams(
            dimension_semantics=("arbitrary",), vmem_limit_bytes=VMEM_LIMIT),
        name="gla_step",
    )(x, states, *[a for a, _ in weights], *carried)


def _ssd_step_kernel(x_ref, s_ref, cv_ref, prew_ref, postw_ref, win_ref, cw_ref, cb_ref, dtb_ref,
                     alog_ref, dskip_ref, wn_ref, wo_ref, expand_ref, expand_n_ref, *rest,
                     bb, has_prev):
    (y_ref, sout_ref, cvout_ref, xs_s, xl_s, ct_s, b_s, e_s, yt_s) = rest[2:] if has_prev else rest
    i = pl.program_id(0)
    bsz = x_ref.shape[0]
    cd = SSD_CONV_DIM

    @pl.when(i == 0)
    def _():
        hb = _rms(x_ref[...], prew_ref[...]).astype(BF16)
        xbc = _dot(hb, win_ref[:, SSD_X0:SSD_DT0])
        dtr = _narrow_proj(hb, win_ref, SSD_DT0, SSD_HEADS)
        cw = cw_ref[...]
        conv = cb_ref[...] + xbc * cw[3:4, :]
        for t in range(SSD_CONV - 1):
            conv = conv + cv_ref[:, t * cd:(t + 1) * cd] * cw[t:t + 1, :]
        cvout_ref[:, 0:cd] = cv_ref[:, cd:2 * cd]
        cvout_ref[:, cd:2 * cd] = cv_ref[:, 2 * cd:3 * cd]
        cvout_ref[:, 2 * cd:3 * cd] = xbc
        xc = _silu(conv)
        xs = xc[:, :SSD_DINNER]
        dt = _softplus(dtr + dtb_ref[...])
        ea = jnp.exp(dt * (-jnp.exp(alog_ref[...])))
        xs_s[...] = xs
        xd_t = (xs * _dot_exact_rhs(dt, expand_ref[...])).T
        c_t = xc[:, SSD_DINNER + SSD_GROUPS * SSD_STATE:].T
        for p in range(SSD_GROUPS // 2):
            for e in range(2):
                gi = 2 * p + e
                xl_s[p, :, e * bsz:(e + 1) * bsz] = (
                    xd_t[gi * SSD_GW:(gi + 1) * SSD_GW].astype(BF16))
        ct_s[...] = c_t
        b_s[...] = xc[:, SSD_DINNER:SSD_DINNER + SSD_GROUPS * SSD_STATE]
        e_s[...] = _dot_exact_rhs(ea, expand_n_ref[...])
        yt_s[...] = jnp.zeros(yt_s.shape, F32)

    sub = lax.broadcasted_iota(jnp.int32, (bsz, SSD_STATE), 0)
    lane = lax.broadcasted_iota(jnp.int32, (SSD_STATE, bsz), 1)
    zero_bn = jnp.zeros((bsz, SSD_STATE), BF16)
    zero_nb = jnp.zeros((SSD_STATE, bsz), BF16)

    for p in range(SSD_GROUPS // 2):
        y_acc = None
        for j in range(bb):
            row = i * bb + j
            er = e_s[pl.ds(row, 1), :]
            bsel, csel = [], []
            for e in range(2):
                ns = slice((2 * p + e) * SSD_STATE, (2 * p + e + 1) * SSD_STATE)
                bsel.append(jnp.where(sub == row, b_s[:, ns], 0.0).astype(BF16))
                csel.append(jnp.where(lane == row, ct_s[ns, :], 0.0).astype(BF16))
            rhs_b = jnp.concatenate([jnp.concatenate([bsel[0], zero_bn], axis=1),
                                     jnp.concatenate([zero_bn, bsel[1]], axis=1)], axis=0)
            rhs_c = jnp.concatenate([jnp.concatenate([csel[0], zero_nb], axis=1),
                                     jnp.concatenate([zero_nb, csel[1]], axis=1)], axis=0)
            outer = _dot(xl_s[p], rhs_b)
            halves = []
            for e in range(2):
                pieces = []
                for r in range(SSD_HPG):
                    h = (2 * p + e) * SSD_HPG + r
                    s_new = (s_ref[j, h] * er[:, h * SSD_STATE:(h + 1) * SSD_STATE]
                             + outer[r * SSD_HEADDIM:(r + 1) * SSD_HEADDIM,
                                     e * SSD_STATE:(e + 1) * SSD_STATE])
                    sout_ref[j, h] = s_new
                    pieces.append(s_new.astype(BF16))
                halves.append(jnp.concatenate(pieces, axis=0))
            y_j = _dot(jnp.concatenate(halves, axis=1), rhs_c)
            y_acc = y_j if y_acc is None else y_acc + y_j
        yt_s[p] += y_acc

    @pl.when(i == pl.num_programs(0) - 1)
    def _():
        x = x_ref[...]
        hb = _rms(x, prew_ref[...]).astype(BF16)
        z = _dot(hb, win_ref[:, SSD_Z0:SSD_X0])
        y = jnp.concatenate([yt_s[p, :, e * bsz:(e + 1) * bsz].T
                             for p in range(SSD_GROUPS // 2) for e in range(2)], axis=1)
        y = y + xs_s[...] * dskip_ref[...]
        y_ref[...] = _ssd_tail(y, z, wn_ref[...], wo_ref[...], postw_ref[...], x)


def _ssd_step_layer(x, states, convs, prev_out, layer, w, *, bb=4):
    bsz = x.shape[0]
    weights = _ssd_weight_list(w, step=True)
    sblk = _batch_block(layer, bb, (SSD_HEADS, SSD_HEADDIM, SSD_STATE))
    cblk = pl.BlockSpec((None,) + convs.shape[1:], lambda i: (layer, 0, 0))
    carried = [] if prev_out is None else list(prev_out)
    n_in = 3 + len(weights)
    return pl.pallas_call(
        functools.partial(_ssd_step_kernel, bb=bb, has_prev=bool(carried)),
        grid=(bsz // bb,),
        in_specs=[_full(x.shape), sblk,
                  pl.BlockSpec(cblk.block_shape, cblk.index_map, pipeline_mode=pl.Buffered(1))]
                 + _weight_specs(weights, layer)
                 + [pl.BlockSpec(memory_space=pl.ANY)] * len(carried),
        out_specs=[pl.BlockSpec(x.shape, lambda i: (0, 0)), sblk, cblk],
        out_shape=[jax.ShapeDtypeStruct(x.shape, F32), jax.ShapeDtypeStruct(states.shape, F32),
                   jax.ShapeDtypeStruct(convs.shape, F32)],
        input_output_aliases={n_in: 1, n_in + 1: 2} if carried else {},
        scratch_shapes=[pltpu.VMEM((bsz, SSD_DINNER), F32),
                        pltpu.VMEM((SSD_GROUPS // 2, SSD_GW, 2 * bsz), BF16),
                        pltpu.VMEM((SSD_GROUPS * SSD_STATE, bsz), F32),
                        pltpu.VMEM((bsz, SSD_GROUPS * SSD_STATE), F32),
                        pltpu.VMEM((bsz, SSD_HEADS * SSD_STATE), F32),
                        pltpu.VMEM((SSD_GROUPS // 2, SSD_GW, 2 * bsz), F32)],
        compiler_params=pltpu.CompilerParams(
            dimension_semantics=("arbitrary",), vmem_limit_bytes=VMEM_LIMIT),
        name="ssd_step",
    )(x, states, convs, *[a for a, _ in weights], *carried)


def _pad_last(a, n):
    return jnp.pad(a, [(0, 0)] * (a.ndim - 1) + [(0, n - a.shape[-1])])


def _rows(a):
    return a[:, None, :]


def _gla_weights(pre, post, w_in, w_gate_up, b_gate, w_norm, w_out):
    return {
        "pre": _rows(pre), "post": _rows(post),
        "win": w_in.astype(BF16),
        "wgu": jnp.pad(w_gate_up, ((0, 0), (0, LANES - GLA_RANK), (0, 0))).astype(BF16),
        "bg": _rows(b_gate), "wn": _rows(w_norm), "wo": w_out.astype(BF16),
    }


def _ssd_weights(pre, post, w_in, conv_w, conv_b, dt_bias, a_log, d_skip, w_norm, w_out):
    head_of_lane = jnp.arange(SSD_DINNER) // SSD_HEADDIM
    head_of_lane_n = jnp.arange(SSD_HEADS * SSD_STATE) // SSD_STATE
    heads = jnp.arange(LANES)[:, None]
    return {
        "pre": _rows(pre), "post": _rows(post),
        "win": w_in.astype(BF16),
        "cw": jnp.pad(conv_w, ((0, 0), (0, SUBLANES - SSD_CONV), (0, 0))), "cb": _rows(conv_b),
        "dtb": _rows(_pad_last(dt_bias, LANES)), "alog": _rows(_pad_last(a_log, LANES)),
        "dskip": _rows(jnp.repeat(d_skip, SSD_HEADDIM, axis=1)),
        "wn": _rows(w_norm), "wo": w_out.astype(BF16),
        "expand": (heads == head_of_lane[None, :]).astype(BF16),
        "expand_n": (heads == head_of_lane_n[None, :]).astype(BF16),
    }


def _prompt_trunk(x, gla0, ssm0, conv0, gw, sw, *, n_pad, gla_tile, ssd_tile):
    gla_s = ssm_conv_s = None
    for j in range(N_LAYERS_PER_MIXER):
        x, gla_s = _gla_prompt_layer(x, gla0, gla_s, j, gw, n_pad=n_pad, tile=gla_tile)
        x, *ssm_conv_s = _ssd_prompt_layer(x, ssm0, conv0, ssm_conv_s, j, sw,
                                           n_pad=n_pad, tile=ssd_tile)
    return x, gla_s, ssm_conv_s[0], ssm_conv_s[1]


def kernel(x_prompt, x_sample, state_gla, state_ssm, state_conv, meta_tokens, pre_norm, post_norm,
           gla_w_in, gla_w_gate_up, gla_b_gate, gla_w_norm, gla_w_out,
           ssd_w_in, ssd_conv_w, ssd_conv_b, ssd_dt_bias, ssd_a_log, ssd_d_skip, ssd_w_norm, ssd_w_out):
    gw = _gla_weights(pre_norm[0::2], post_norm[0::2], gla_w_in, gla_w_gate_up, gla_b_gate,
                      gla_w_norm, gla_w_out)
    sw = _ssd_weights(pre_norm[1::2], post_norm[1::2], ssd_w_in, ssd_conv_w, ssd_conv_b,
                      ssd_dt_bias, ssd_a_log, ssd_d_skip, ssd_w_norm, ssd_w_out)

    n_pad = CHUNK - N_META
    x_meta = jnp.pad(meta_tokens.astype(F32), ((n_pad, 0), (0, 0)))[None]
    zg = jnp.zeros((N_LAYERS_PER_MIXER, 1, GLA_HEADS, GLA_HK, GLA_HV), F32)
    zs = jnp.zeros((N_LAYERS_PER_MIXER, 1, SSD_HEADS, SSD_HEADDIM, SSD_STATE), F32)
    zc = jnp.zeros((N_LAYERS_PER_MIXER, 1, SUBLANES, SSD_CONV_DIM), F32)
    _, mg, ms, mc = _prompt_trunk(x_meta, zg, zs, zc, gw, sw,
                                  n_pad=n_pad, gla_tile=CHUNK, ssd_tile=CHUNK)
    y_prompt, gla_p, ssm_p, conv_p = _prompt_trunk(x_prompt, mg, ms, mc, gw, sw, n_pad=0,
                                                   gla_tile=GLA_TILE, ssd_tile=SSD_TILE)
    conv_p = conv_p[:, :, SUBLANES - (SSD_CONV - 1):, :]

    xs = x_sample[:, 0, :]
    sbsz = xs.shape[0]
    convs = state_conv.reshape(N_LAYERS_PER_MIXER, sbsz, (SSD_CONV - 1) * SSD_CONV_DIM)
    gla_s = ssm_conv_s = None
    for j in range(N_LAYERS_PER_MIXER):
        xs, gla_s = _gla_step_layer(xs, state_gla, gla_s, j, gw)
        xs, *ssm_conv_s = _ssd_step_layer(xs, state_ssm, convs, ssm_conv_s, j, sw)
    ssm_s, conv_s = ssm_conv_s
    y_sample = xs[:, None, :]
    return (y_prompt, y_sample, gla_p, ssm_p, conv_p, gla_s, ssm_s,
            conv_s.reshape(N_LAYERS_PER_MIXER, sbsz, SSD_CONV - 1, SSD_CONV_DIM))
```

```python
import functools

import jax
import jax.numpy as jnp
from jax import lax
from jax.experimental import pallas as pl
from jax.experimental.pallas import tpu as pltpu

F32 = jnp.float32
BF16 = jnp.bfloat16

D_MODEL = 1024
N_META = 16
NORM_EPS = 1e-6
N_LAYERS_PER_MIXER = 2

GLA_HEADS = 4
GLA_DK = 512
GLA_DV = 1024
GLA_HK = 128
GLA_HV = 256
GLA_RANK = 16
GLA_TAU = 16.0
GLA_SAFE_DROP = 40.0

SSD_DINNER = 2048
SSD_HEADDIM = 64
SSD_HEADS = 32
SSD_GROUPS = 4
SSD_HPG = 8
SSD_STATE = 128
SSD_CONV = 4
SSD_CONV_DIM = 3072
SSD_GW = SSD_HPG * SSD_HEADDIM

LANES = 128
SUBLANES = 8
CHUNK = 128
GLA_TILE = 1024
SSD_TILE = 512
VMEM_LIMIT = 56 * 1024 * 1024

GLA_Q0, GLA_K0, GLA_V0, GLA_R0, GLA_G0 = 0, GLA_DK, 2 * GLA_DK, 2 * GLA_DK + GLA_DV, 2 * GLA_DK + 2 * GLA_DV
SSD_Z0, SSD_X0, SSD_DT0 = 0, SSD_DINNER, SSD_DINNER + SSD_CONV_DIM


def _narrow_proj(hb, w_ref, start, width):
    res = jnp.dot(hb, w_ref[:, start:start + width], preferred_element_type=F32)
    return jnp.concatenate([res, jnp.zeros((res.shape[0], LANES - width), F32)], axis=1)


def _dot(a, b):
    return jnp.dot(a, b, preferred_element_type=F32)


def _dot_nt(a, b):
    return lax.dot_general(a, b, (((1,), (1,)), ((), ())), preferred_element_type=F32)


def _dot_tn(a, b):
    return lax.dot_general(a, b, (((0,), (0,)), ((), ())), preferred_element_type=F32)


def _rms(x, w):
    return x * lax.rsqrt(jnp.mean(x * x, axis=-1, keepdims=True) + NORM_EPS) * w


def _silu(x):
    h = 0.5 * x
    return h + h * jnp.tanh(h)


def _softplus(x):
    return jnp.maximum(x, 0.0) + jnp.log1p(jnp.exp(-jnp.abs(x)))


def _log_sigmoid(x):
    return -_softplus(-x)


def _tri(n):
    r = lax.broadcasted_iota(jnp.int32, (n, n), 0)
    c = lax.broadcasted_iota(jnp.int32, (n, n), 1)
    return r >= c


def _split3(x):
    hi = x.astype(BF16)
    r1 = x - hi.astype(F32)
    mid = r1.astype(BF16)
    lo = (r1 - mid.astype(F32)).astype(BF16)
    return hi, mid, lo


def _dot_exact_rhs(x, m_bf16):
    hi, mid, lo = _split3(x)
    return _dot(hi, m_bf16) + _dot(mid, m_bf16) + _dot(lo, m_bf16)


def _cumsum_rows(tril_bf16, x):
    hi, mid, lo = _split3(x)
    return _dot(tril_bf16, hi) + _dot(tril_bf16, mid) + _dot(tril_bf16, lo)


def _full(shape):
    nd = len(shape)
    return pl.BlockSpec(shape, lambda *_: (0,) * nd, pipeline_mode=pl.Buffered(1))


def _of_layer(a, layer):
    nz = a.ndim - 1
    return pl.BlockSpec((None,) + a.shape[1:], lambda *_: (layer,) + (0,) * nz,
                        pipeline_mode=pl.Buffered(1))


def _weight_specs(weights, layer):
    return [_of_layer(a, layer) if stacked else _full(a.shape) for a, stacked in weights]


def _batch_block(layer, bb, tail, own=True):
    nz = len(tail)
    return pl.BlockSpec((None, bb) + tail,
                        lambda i, *_: (layer, i if own else 0) + (0,) * nz)


def _gla_attn_pairwise(qf, kf, bc, causal, kbuf_ref, bbuf_ref):
    n = qf.shape[0]
    kbuf_ref[...] = kf
    bbuf_ref[...] = bc
    lane = lax.broadcasted_iota(jnp.int32, (n, n), 1)

    def body(j, acc):
        kj = kbuf_ref[pl.ds(j, 1), :]
        bj = bbuf_ref[pl.ds(j, 1), :]
        col = jnp.sum(qf * jnp.exp(jnp.minimum(bc - bj, 0.0)) * kj, axis=1, keepdims=True)
        return jnp.where(lane == j, col, acc)

    acc = lax.fori_loop(0, n, body, jnp.zeros((n, n), F32))
    return jnp.where(causal, acc, 0.0)


def _gla_scan_tile(q, k, v, g, states, pairwise, kbuf_ref, bbuf_ref):
    causal = _tri(CHUNK)
    tril = causal.astype(BF16)
    states = list(states)
    outs = []
    for c in range(q.shape[0] // CHUNK):
        rows = slice(c * CHUNK, (c + 1) * CHUNK)
        bc = _cumsum_rows(tril, g[rows])
        b_last = bc[-1:, :]
        e_last = jnp.exp(b_last)
        qh = (q[rows] * jnp.exp(bc)).astype(BF16)
        if pairwise:
            kd = (k[rows] * jnp.exp(b_last - bc)).astype(BF16)
        else:
            kh = k[rows] * jnp.exp(-bc)
            kd = (kh * e_last).astype(BF16)
            kh = kh.astype(BF16)
        vb = v[rows].astype(BF16)
        parts = []
        for h in range(GLA_HEADS):
            ks = slice(h * GLA_HK, (h + 1) * GLA_HK)
            vs = slice(h * GLA_HV, (h + 1) * GLA_HV)
            if pairwise:
                attn = _gla_attn_pairwise(q[rows, ks], k[rows, ks], bc[:, ks], causal,
                                          kbuf_ref, bbuf_ref)
            else:
                attn = jnp.where(causal, _dot_nt(qh[:, ks], kh[:, ks]), 0.0)
            st = states[h]
            parts.append(_dot(attn.astype(BF16), vb[:, vs]) + _dot_nt(qh[:, ks], st.astype(BF16)))
            states[h] = st * e_last[:, ks] + _dot_tn(vb[:, vs], kd[:, ks])
        outs.append(jnp.concatenate(parts, axis=1))
    o = jnp.concatenate(outs, axis=0) if len(outs) > 1 else outs[0]
    return o, states


def _gla_tail(o, r, wn, wo, postw, x):
    parts = []
    for h in range(GLA_HEADS):
        vs = slice(h * GLA_HV, (h + 1) * GLA_HV)
        parts.append(_rms(o[:, vs], wn[:, vs]))
    on = jnp.concatenate(parts, axis=1) * _silu(r)
    y = _dot(on.astype(BF16), wo)
    return x + _rms(y, postw)


def _gla_gates(hb, win_ref, wgu_ref, bg_ref):
    gl = _narrow_proj(hb, win_ref, GLA_G0, GLA_RANK)
    return _log_sigmoid(_dot(gl.astype(BF16), wgu_ref[...]) + bg_ref[...]) * (1.0 / GLA_TAU)


def _gla_prompt_kernel(x_ref, s0_ref, prew_ref, postw_ref, win_ref, wgu_ref, bg_ref, wn_ref,
                       wo_ref, *rest, n_pad, tile, has_prev):
    y_ref, sout_ref, st_ref, o_ref, kbuf_ref, bbuf_ref = rest[1:] if has_prev else rest
    t = pl.program_id(1)

    @pl.when(t == 0)
    def _():
        for h in range(GLA_HEADS):
            st_ref[h] = s0_ref[0, h].T

    x = x_ref[0]
    hn = _rms(x, prew_ref[...])
    if n_pad:
        row = lax.broadcasted_iota(jnp.int32, (tile, 1), 0) + t * tile
        hn = jnp.where(row >= n_pad, hn, 0.0)
    hb = hn.astype(BF16)
    q = _dot(hb, win_ref[:, GLA_Q0:GLA_K0]) * (GLA_HK ** -0.5)
    k = _dot(hb, win_ref[:, GLA_K0:GLA_V0])
    v = _dot(hb, win_ref[:, GLA_V0:GLA_R0])
    g = _gla_gates(hb, win_ref, wgu_ref, bg_ref)

    drop = None
    for c in range(tile // CHUNK):
        tot = jnp.sum(g[c * CHUNK:(c + 1) * CHUNK], axis=0, keepdims=True)
        drop = tot if drop is None else jnp.minimum(drop, tot)
    safe = jnp.min(drop) >= -GLA_SAFE_DROP

    def scan(pairwise):
        o, states = _gla_scan_tile(q, k, v, g, [st_ref[h] for h in range(GLA_HEADS)], pairwise,
                                   kbuf_ref, bbuf_ref)
        o_ref[...] = o
        for h in range(GLA_HEADS):
            st_ref[h] = states[h]

    pl.when(safe)(functools.partial(scan, False))
    pl.when(jnp.logical_not(safe))(functools.partial(scan, True))
    r = _dot(hb, win_ref[:, GLA_R0:GLA_G0])
    y_ref[0] = _gla_tail(o_ref[...], r, wn_ref[...], wo_ref[...], postw_ref[...], x)

    @pl.when(t == pl.num_programs(1) - 1)
    def _():
        for h in range(GLA_HEADS):
            sout_ref[0, h] = st_ref[h].T


def _gla_weight_list(w):
    return [(w[k], True) for k in ("pre", "post", "win", "wgu", "bg", "wn", "wo")]


def _gla_prompt_layer(x, s0, prev_out, layer, w, *, n_pad, tile):
    bsz, seq, _ = x.shape
    tail = (GLA_HEADS, GLA_HK, GLA_HV)
    weights = _gla_weight_list(w)
    carried = [] if prev_out is None else [prev_out]
    return pl.pallas_call(
        functools.partial(_gla_prompt_kernel, n_pad=n_pad, tile=tile, has_prev=bool(carried)),
        grid=(bsz, seq // tile),
        in_specs=[pl.BlockSpec((1, tile, D_MODEL), lambda b, t: (b, t, 0)),
                  _batch_block(layer, 1, tail, own=s0.shape[1] == bsz)]
                 + _weight_specs(weights, layer)
                 + [pl.BlockSpec(memory_space=pl.ANY)] * len(carried),
        out_specs=[pl.BlockSpec((1, tile, D_MODEL), lambda b, t: (b, t, 0)),
                   _batch_block(layer, 1, tail)],
        out_shape=[jax.ShapeDtypeStruct(x.shape, F32),
                   jax.ShapeDtypeStruct((N_LAYERS_PER_MIXER, bsz) + tail, F32)],
        scratch_shapes=[pltpu.VMEM((GLA_HEADS, GLA_HV, GLA_HK), F32),
                        pltpu.VMEM((tile, GLA_DV), F32),
                        pltpu.VMEM((CHUNK, GLA_HK), F32),
                        pltpu.VMEM((CHUNK, GLA_HK), F32)],
        input_output_aliases={2 + len(weights): 1} if carried else {},
        compiler_params=pltpu.CompilerParams(
            dimension_semantics=("arbitrary", "arbitrary"), vmem_limit_bytes=VMEM_LIMIT),
        name="gla_prompt",
    )(x, s0, *[a for a, _ in weights], *carried)


UNIT_GROUPS = 4
UNIT_ROWS = UNIT_GROUPS * SUBLANES
WRAP_ROWS = (SSD_CONV - 1) * SUBLANES
D_SLABS = D_MODEL // LANES
assert SSD_CONV - 1 < UNIT_GROUPS and CHUNK % UNIT_ROWS == 0


def _position_time(n):
    p = lax.iota(jnp.int32, n)
    q = p % UNIT_ROWS
    return (p - q) + (q % SUBLANES) * UNIT_GROUPS + q // SUBLANES


def _load_interleaved(slab_refs, tile):
    groups = []
    for u in range(tile // UNIT_ROWS):
        for g in range(UNIT_GROUPS):
            rows = pl.ds(u * UNIT_ROWS + g, SUBLANES, stride=UNIT_GROUPS)
            groups.append(jnp.concatenate([r[0, rows, :] for r in slab_refs], axis=1))
    return jnp.concatenate(groups, axis=0)


def _store_interleaved(y_ref, slab_ref, y, tile):
    for u in range(tile // UNIT_ROWS):
        for g in range(UNIT_GROUPS):
            r0 = u * UNIT_ROWS + g * SUBLANES
            rows = pl.ds(u * UNIT_ROWS + g, SUBLANES, stride=UNIT_GROUPS)
            for j in range(D_SLABS):
                slab_ref[j, rows, :] = y[r0:r0 + SUBLANES, j * LANES:(j + 1) * LANES]
    y_ref[0] = jnp.concatenate([slab_ref[j] for j in range(D_SLABS)], axis=1)


def _ssd_conv(xbc, cbuf_ref, cw, cb, tile):
    sub = lax.broadcasted_iota(jnp.int32, (SUBLANES, 1), 0)
    outs = []
    prev = cbuf_ref[...]
    for u in range(tile // UNIT_ROWS):
        xb = xbc[u * UNIT_ROWS:(u + 1) * UNIT_ROWS]
        last = xb[UNIT_ROWS - WRAP_ROWS:]
        wrapped = []
        for v in range(SSD_CONV - 1):
            rs = slice(v * SUBLANES, (v + 1) * SUBLANES)
            mix = jnp.where(sub == SUBLANES - 1, prev[rs], last[rs])
            wrapped.append(pltpu.roll(mix, 1, axis=0))
        ext = jnp.concatenate(wrapped + [xb], axis=0)
        conv = cb + xb * cw[SSD_CONV - 1:SSD_CONV, :]
        for i in range(SSD_CONV - 1):
            conv = conv + ext[i * SUBLANES:i * SUBLANES + UNIT_ROWS] * cw[i:i + 1, :]
        outs.append(conv)
        prev = last
    cbuf_ref[...] = prev
    return _silu(jnp.concatenate(outs, axis=0))


def _ssd_scan_chunk(xs, bm, cm, dt, a, expand, st_ref, y_ref, row0, causal, tril, lo_half):
    ac = _cumsum_rows(tril, a)
    ac_t = ac.T
    dt_t = dt.T
    a_last = ac[-1:, :]
    wgt = jnp.exp(a_last - ac) * dt
    xw = (xs * _dot(wgt.astype(BF16), expand)).astype(BF16)
    e_last = _dot_exact_rhs(jnp.exp(a_last), expand)
    pair = 2 * SSD_HEADDIM
    for gi in range(SSD_GROUPS):
        ns = slice(gi * SSD_STATE, (gi + 1) * SSD_STATE)
        gs = slice(gi * SSD_GW, (gi + 1) * SSD_GW)
        bg = bm[:, ns].astype(BF16)
        cg = cm[:, ns].astype(BF16)
        cb = _dot_nt(cg, bg)
        st = st_ref[gi]
        y_int = _dot(cg, st.astype(BF16))
        for qi in range(SSD_HPG // 2):
            h0 = gi * SSD_HPG + 2 * qi
            ps = slice(h0 * SSD_HEADDIM, h0 * SSD_HEADDIM + pair)
            ms, eas = [], []
            for e in range(2):
                h = h0 + e
                col = jnp.broadcast_to(ac[:, h:h + 1], (CHUNK, CHUNK))
                seg = col - ac_t[h:h + 1, :]
                ms.append(jnp.where(causal, cb * jnp.exp(seg) * dt_t[h:h + 1, :], 0.0).astype(BF16))
                eas.append(jnp.exp(col))
            xp = xs[:, ps]
            rhs = jnp.concatenate([jnp.where(lo_half, xp, 0.0).astype(BF16),
                                   jnp.where(lo_half, 0.0, xp).astype(BF16)], axis=0)
            ea = jnp.where(lo_half, eas[0], eas[1])
            y_ref[row0:row0 + CHUNK, ps] = (_dot(jnp.concatenate(ms, axis=1), rhs)
                                            + y_int[:, qi * pair:(qi + 1) * pair] * ea)
        st_ref[gi] = st * e_last[:, gs] + _dot_tn(bg, xw[:, gs])


def _ssd_tail(y, z, wn, wo, postw, x):
    y = y * _silu(z)
    parts = []
    for gi in range(SSD_GROUPS):
        gs = slice(gi * SSD_GW, (gi + 1) * SSD_GW)
        parts.append(_rms(y[:, gs], wn[:, gs]))
    yn = jnp.concatenate(parts, axis=1)
    out = _dot(yn.astype(BF16), wo)
    return x + _rms(out, postw)


def _ssd_prompt_kernel(*refs, n_pad, tile, has_prev):
    x_refs, refs = refs[:D_SLABS], refs[D_SLABS:]
    (s0_ref, c0_ref, prew_ref, postw_ref, win_ref, cw_ref, cb_ref, dtb_ref, alog_ref, dskip_ref,
     wn_ref, wo_ref, expand_ref) = refs[:13]
    y_ref, sout_ref, cout_ref, st_ref, cbuf_ref, ys_ref, slab_ref = refs[13 + (2 if has_prev else 0):]
    t = pl.program_id(1)
    first_buffered = SUBLANES - (SSD_CONV - 1)

    @pl.when(t == 0)
    def _():
        for gi in range(SSD_GROUPS):
            s0 = s0_ref[0, gi * SSD_HPG:(gi + 1) * SSD_HPG].reshape(SSD_GW, SSD_STATE)
            st_ref[gi] = s0.T
        cbuf_ref[...] = jnp.zeros(cbuf_ref.shape, F32)
        for v in range(SSD_CONV - 1):
            r = (v + 1) * SUBLANES - 1
            cbuf_ref[r:r + 1, :] = c0_ref[0, first_buffered + v:first_buffered + v + 1, :]

    x = _load_interleaved(x_refs, tile)
    hn = _rms(x, prew_ref[...])
    if n_pad:
        row = t * tile + _position_time(tile)[:, None]
        hn = jnp.where(row >= n_pad, hn, 0.0)
    hb = hn.astype(BF16)
    xbc = _dot(hb, win_ref[:, SSD_X0:SSD_DT0])
    dtr = _narrow_proj(hb, win_ref, SSD_DT0, SSD_HEADS)
    xc = _ssd_conv(xbc, cbuf_ref, cw_ref[...], cb_ref[...], tile)
    z = _dot(hb, win_ref[:, SSD_Z0:SSD_X0])
    xs = xc[:, :SSD_DINNER]
    bm = xc[:, SSD_DINNER:SSD_DINNER + SSD_GROUPS * SSD_STATE]
    cm = xc[:, SSD_DINNER + SSD_GROUPS * SSD_STATE:]
    dt = _softplus(dtr + dtb_ref[...])
    if n_pad:
        dt = jnp.where(row >= n_pad, dt, 0.0)
    a = dt * (-jnp.exp(alog_ref[...]))

    tm = _position_time(CHUNK)
    causal = tm[:, None] >= tm[None, :]
    tril = causal.astype(BF16)
    lo_half = lax.broadcasted_iota(jnp.int32, (CHUNK, 2 * SSD_HEADDIM), 1) < SSD_HEADDIM
    expand = expand_ref[...]
    for c in range(tile // CHUNK):
        rows = slice(c * CHUNK, (c + 1) * CHUNK)
        _ssd_scan_chunk(xs[rows], bm[rows], cm[rows], dt[rows], a[rows], expand,
                        st_ref, ys_ref, c * CHUNK, causal, tril, lo_half)
    y = ys_ref[...] + xs * dskip_ref[...]
    out = _ssd_tail(y, z, wn_ref[...], wo_ref[...], postw_ref[...], x)
    _store_interleaved(y_ref, slab_ref, out, tile)

    @pl.when(t == pl.num_programs(1) - 1)
    def _():
        for gi in range(SSD_GROUPS):
            sout_ref[0, gi * SSD_HPG:(gi + 1) * SSD_HPG] = (
                st_ref[gi].T.reshape(SSD_HPG, SSD_HEADDIM, SSD_STATE))
        cout_ref[0] = jnp.zeros(cout_ref.shape[1:], F32)
        for v in range(SSD_CONV - 1):
            r = (v + 1) * SUBLANES - 1
            cout_ref[0, first_buffered + v:first_buffered + v + 1, :] = cbuf_ref[r:r + 1, :]


def _ssd_weight_list(w, step):
    names = ("pre", "post", "win", "cw", "cb", "dtb", "alog", "dskip", "wn", "wo")
    consts = ("expand", "expand_n") if step else ("expand",)
    return [(w[k], True) for k in names] + [(w[k], False) for k in consts]


def _ssd_prompt_layer(x, s0, c0, prev_out, layer, w, *, n_pad, tile):
    bsz, seq, _ = x.shape
    s_tail = (SSD_HEADS, SSD_HEADDIM, SSD_STATE)
    c_tail = (SUBLANES, SSD_CONV_DIM)
    own = s0.shape[1] == bsz
    weights = _ssd_weight_list(w, step=False)
    carried = [] if prev_out is None else list(prev_out)
    n_in = D_SLABS + 2 + len(weights)
    x_slabs = [pl.BlockSpec((1, tile, LANES), lambda b, t, j=j: (b, t, j)) for j in range(D_SLABS)]
    return pl.pallas_call(
        functools.partial(_ssd_prompt_kernel, n_pad=n_pad, tile=tile, has_prev=bool(carried)),
        grid=(bsz, seq // tile),
        in_specs=x_slabs
                 + [_batch_block(layer, 1, s_tail, own), _batch_block(layer, 1, c_tail, own)]
                 + _weight_specs(weights, layer)
                 + [pl.BlockSpec(memory_space=pl.ANY)] * len(carried),
        out_specs=[pl.BlockSpec((1, tile, D_MODEL), lambda b, t: (b, t, 0)),
                   _batch_block(layer, 1, s_tail), _batch_block(layer, 1, c_tail)],
        out_shape=[jax.ShapeDtypeStruct(x.shape, F32),
                   jax.ShapeDtypeStruct((N_LAYERS_PER_MIXER, bsz) + s_tail, F32),
                   jax.ShapeDtypeStruct((N_LAYERS_PER_MIXER, bsz) + c_tail, F32)],
        scratch_shapes=[pltpu.VMEM((SSD_GROUPS, SSD_STATE, SSD_GW), F32),
                        pltpu.VMEM((WRAP_ROWS, SSD_CONV_DIM), F32),
                        pltpu.VMEM((tile, SSD_DINNER), F32),
                        pltpu.VMEM((D_SLABS, tile, LANES), F32)],
        input_output_aliases={n_in: 1, n_in + 1: 2} if carried else {},
        compiler_params=pltpu.CompilerParams(
            dimension_semantics=("arbitrary", "arbitrary"), vmem_limit_bytes=VMEM_LIMIT),
        name="ssd_prompt",
    )(*([x] * D_SLABS), s0, c0, *[a for a, _ in weights], *carried)


def _column(row, width, lane0=0):
    seg = row[:, lane0:lane0 + width]
    r = lax.broadcasted_iota(jnp.int32, (width, width), 0)
    c = lax.broadcasted_iota(jnp.int32, (width, width), 1)
    return jnp.sum(jnp.where(r == c, seg, 0.0), axis=1, keepdims=True)


def _gla_step_kernel(x_ref, s_ref, prew_ref, postw_ref, win_ref, wgu_ref, bg_ref, wn_ref, wo_ref,
                     *rest, bb, has_prev):
    y_ref, sout_ref, q_s, kl_s, v_s, e_s, o_s = rest[1:] if has_prev else rest
    i = pl.program_id(0)
    bsz = x_ref.shape[0]

    @pl.when(i == 0)
    def _():
        hb = _rms(x_ref[...], prew_ref[...]).astype(BF16)
        q_s[...] = _dot(hb, win_ref[:, GLA_Q0:GLA_K0]) * (GLA_HK ** -0.5)
        o_s[:, 0:GLA_DK] = _dot(hb, win_ref[:, GLA_K0:GLA_V0])
        k_t = o_s[:, 0:GLA_DK].T
        for h in range(GLA_HEADS):
            kl_s[h // 2, :, (h % 2) * bsz:(h % 2 + 1) * bsz] = (
                k_t[h * GLA_HK:(h + 1) * GLA_HK].astype(BF16))
        v_s[...] = _dot(hb, win_ref[:, GLA_V0:GLA_R0])
        e_s[...] = jnp.exp(_gla_gates(hb, win_ref, wgu_ref, bg_ref))

    sub = lax.broadcasted_iota(jnp.int32, (bsz, GLA_HV), 0)
    zero_bv = jnp.zeros((bsz, GLA_HV), BF16)

    for j in range(bb):
        row = i * bb + j
        qr = q_s[pl.ds(row, 1), :]
        er = e_s[pl.ds(row, 1), :]
        for p in range(GLA_HEADS // 2):
            vsel = [jnp.where(sub == row, v_s[:, (2 * p + e) * GLA_HV:(2 * p + e + 1) * GLA_HV],
                              0.0).astype(BF16) for e in range(2)]
            rhs = jnp.concatenate([jnp.concatenate([vsel[0], zero_bv], axis=1),
                                   jnp.concatenate([zero_bv, vsel[1]], axis=1)], axis=0)
            kv = _dot(kl_s[p], rhs)
            for e in range(2):
                h = 2 * p + e
                vs = slice(h * GLA_HV, (h + 1) * GLA_HV)
                ecol = _column(er, GLA_HK, h * GLA_HK)
                qcol = _column(qr, GLA_HK, h * GLA_HK)
                s_new = s_ref[j, h] * ecol + kv[:, e * GLA_HV:(e + 1) * GLA_HV]
                sout_ref[j, h] = s_new
                o_s[pl.ds(row, 1), vs] = jnp.sum(qcol * s_new, axis=0, keepdims=True)

    @pl.when(i == pl.num_programs(0) - 1)
    def _():
        x = x_ref[...]
        hb = _rms(x, prew_ref[...]).astype(BF16)
        r = _dot(hb, win_ref[:, GLA_R0:GLA_G0])
        y_ref[...] = _gla_tail(o_s[...], r, wn_ref[...], wo_ref[...], postw_ref[...], x)


def _gla_step_layer(x, states, prev_out, layer, w, *, bb=8):
    bsz = x.shape[0]
    weights = _gla_weight_list(w)
    sblk = _batch_block(layer, bb, (GLA_HEADS, GLA_HK, GLA_HV))
    carried = [] if prev_out is None else [prev_out]
    return pl.pallas_call(
        functools.partial(_gla_step_kernel, bb=bb, has_prev=bool(carried)),
        grid=(bsz // bb,),
        in_specs=[_full(x.shape), sblk] + _weight_specs(weights, layer)
                 + [pl.BlockSpec(memory_space=pl.ANY)] * len(carried),
        out_specs=[pl.BlockSpec(x.shape, lambda i: (0, 0)), sblk],
        out_shape=[jax.ShapeDtypeStruct(x.shape, F32), jax.ShapeDtypeStruct(states.shape, F32)],
        scratch_shapes=[pltpu.VMEM((bsz, GLA_DK), F32),
                        pltpu.VMEM((GLA_HEADS // 2, GLA_HK, 2 * bsz), BF16),
                        pltpu.VMEM((bsz, GLA_DV), F32),
                        pltpu.VMEM((bsz, GLA_DK), F32),
                        pltpu.VMEM((bsz, GLA_DV), F32)],
        input_output_aliases={2 + len(weights): 1} if carried else {},
        compiler_params=pltpu.CompilerParams(
            dimension_semantics=("arbitrary",), vmem_limit_bytes=VMEM_LIMIT),
        name="gla_step",
    )(x, states, *[a for a, _ in weights], *carried)


def _ssd_step_kernel(x_ref, s_ref, cv_ref, prew_ref, postw_ref, win_ref, cw_ref, cb_ref, dtb_ref,
                     alog_ref, dskip_ref, wn_ref, wo_ref, expand_ref, expand_n_ref, *rest,
                     bb, has_prev):
    (y_ref, sout_ref, cvout_ref, xs_s, xl_s, ct_s, b_s, e_s, yt_s) = rest[2:] if has_prev else rest
    i = pl.program_id(0)
    bsz = x_ref.shape[0]
    cd = SSD_CONV_DIM

    @pl.when(i == 0)
    def _():
        hb = _rms(x_ref[...], prew_ref[...]).astype(BF16)
        xbc = _dot(hb, win_ref[:, SSD_X0:SSD_DT0])
        dtr = _narrow_proj(hb, win_ref, SSD_DT0, SSD_HEADS)
        cw = cw_ref[...]
        conv = cb_ref[...] + xbc * cw[3:4, :]
        for t in range(SSD_CONV - 1):
            conv = conv + cv_ref[:, t * cd:(t + 1) * cd] * cw[t:t + 1, :]
        cvout_ref[:, 0:cd] = cv_ref[:, cd:2 * cd]
        cvout_ref[:, cd:2 * cd] = cv_ref[:, 2 * cd:3 * cd]
        cvout_ref[:, 2 * cd:3 * cd] = xbc
        xc = _silu(conv)
        xs = xc[:, :SSD_DINNER]
        dt = _softplus(dtr + dtb_ref[...])
        ea = jnp.exp(dt * (-jnp.exp(alog_ref[...])))
        xs_s[...] = xs
        xd_t = (xs * _dot_exact_rhs(dt, expand_ref[...])).T
        c_t = xc[:, SSD_DINNER + SSD_GROUPS * SSD_STATE:].T
        for p in range(SSD_GROUPS // 2):
            for e in range(2):
                gi = 2 * p + e
                xl_s[p, :, e * bsz:(e + 1) * bsz] = (
                    xd_t[gi * SSD_GW:(gi + 1) * SSD_GW].astype(BF16))
        ct_s[...] = c_t
        b_s[...] = xc[:, SSD_DINNER:SSD_DINNER + SSD_GROUPS * SSD_STATE]
        e_s[...] = _dot_exact_rhs(ea, expand_n_ref[...])
        yt_s[...] = jnp.zeros(yt_s.shape, F32)

    sub = lax.broadcasted_iota(jnp.int32, (bsz, SSD_STATE), 0)
    lane = lax.broadcasted_iota(jnp.int32, (SSD_STATE, bsz), 1)
    zero_bn = jnp.zeros((bsz, SSD_STATE), BF16)
    zero_nb = jnp.zeros((SSD_STATE, bsz), BF16)

    for p in range(SSD_GROUPS // 2):
        y_acc = None
        for j in range(bb):
            row = i * bb + j
            er = e_s[pl.ds(row, 1), :]
            bsel, csel = [], []
            for e in range(2):
                ns = slice((2 * p + e) * SSD_STATE, (2 * p + e + 1) * SSD_STATE)
                bsel.append(jnp.where(sub == row, b_s[:, ns], 0.0).astype(BF16))
                csel.append(jnp.where(lane == row, ct_s[ns, :], 0.0).astype(BF16))
            rhs_b = jnp.concatenate([jnp.concatenate([bsel[0], zero_bn], axis=1),
                                     jnp.concatenate([zero_bn, bsel[1]], axis=1)], axis=0)
            rhs_c = jnp.concatenate([jnp.concatenate([csel[0], zero_nb], axis=1),
                                     jnp.concatenate([zero_nb, csel[1]], axis=1)], axis=0)
            outer = _dot(xl_s[p], rhs_b)
            halves = []
            for e in range(2):
                pieces = []
                for r in range(SSD_HPG):
                    h = (2 * p + e) * SSD_HPG + r
                    s_new = (s_ref[j, h] * er[:, h * SSD_STATE:(h + 1) * SSD_STATE]
                             + outer[r * SSD_HEADDIM:(r + 1) * SSD_HEADDIM,
                                     e * SSD_STATE:(e + 1) * SSD_STATE])
                    sout_ref[j, h] = s_new
                    pieces.append(s_new.astype(BF16))
                halves.append(jnp.concatenate(pieces, axis=0))
            y_j = _dot(jnp.concatenate(halves, axis=1), rhs_c)
            y_acc = y_j if y_acc is None else y_acc + y_j
        yt_s[p] += y_acc

    @pl.when(i == pl.num_programs(0) - 1)
    def _():
        x = x_ref[...]
        hb = _rms(x, prew_ref[...]).astype(BF16)
        z = _dot(hb, win_ref[:, SSD_Z0:SSD_X0])
        y = jnp.concatenate([yt_s[p, :, e * bsz:(e + 1) * bsz].T
                             for p in range(SSD_GROUPS // 2) for e in range(2)], axis=1)
        y = y + xs_s[...] * dskip_ref[...]
        y_ref[...] = _ssd_tail(y, z, wn_ref[...], wo_ref[...], postw_ref[...], x)


def _ssd_step_layer(x, states, convs, prev_out, layer, w, *, bb=4):
    bsz = x.shape[0]
    weights = _ssd_weight_list(w, step=True)
    sblk = _batch_block(layer, bb, (SSD_HEADS, SSD_HEADDIM, SSD_STATE))
    cblk = pl.BlockSpec((None,) + convs.shape[1:], lambda i: (layer, 0, 0))
    carried = [] if prev_out is None else list(prev_out)
    n_in = 3 + len(weights)
    return pl.pallas_call(
        functools.partial(_ssd_step_kernel, bb=bb, has_prev=bool(carried)),
        grid=(bsz // bb,),
        in_specs=[_full(x.shape), sblk,
                  pl.BlockSpec(cblk.block_shape, cblk.index_map, pipeline_mode=pl.Buffered(1))]
                 + _weight_specs(weights, layer)
                 + [pl.BlockSpec(memory_space=pl.ANY)] * len(carried),
        out_specs=[pl.BlockSpec(x.shape, lambda i: (0, 0)), sblk, cblk],
        out_shape=[jax.ShapeDtypeStruct(x.shape, F32), jax.ShapeDtypeStruct(states.shape, F32),
                   jax.ShapeDtypeStruct(convs.shape, F32)],
        input_output_aliases={n_in: 1, n_in + 1: 2} if carried else {},
        scratch_shapes=[pltpu.VMEM((bsz, SSD_DINNER), F32),
                        pltpu.VMEM((SSD_GROUPS // 2, SSD_GW, 2 * bsz), BF16),
                        pltpu.VMEM((SSD_GROUPS * SSD_STATE, bsz), F32),
                        pltpu.VMEM((bsz, SSD_GROUPS * SSD_STATE), F32),
                        pltpu.VMEM((bsz, SSD_HEADS * SSD_STATE), F32),
                        pltpu.VMEM((SSD_GROUPS // 2, SSD_GW, 2 * bsz), F32)],
        compiler_params=pltpu.CompilerParams(
            dimension_semantics=("arbitrary",), vmem_limit_bytes=VMEM_LIMIT),
        name="ssd_step",
    )(x, states, convs, *[a for a, _ in weights], *carried)


def _pad_last(a, n):
    return jnp.pad(a, [(0, 0)] * (a.ndim - 1) + [(0, n - a.shape[-1])])


def _rows(a):
    return a[:, None, :]


def _gla_weights(pre, post, w_in, w_gate_up, b_gate, w_norm, w_out):
    return {
        "pre": _rows(pre), "post": _rows(post),
        "win": w_in.astype(BF16),
        "wgu": jnp.pad(w_gate_up, ((0, 0), (0, LANES - GLA_RANK), (0, 0))).astype(BF16),
        "bg": _rows(b_gate), "wn": _rows(w_norm), "wo": w_out.astype(BF16),
    }


def _ssd_weights(pre, post, w_in, conv_w, conv_b, dt_bias, a_log, d_skip, w_norm, w_out):
    head_of_lane = jnp.arange(SSD_DINNER) // SSD_HEADDIM
    head_of_lane_n = jnp.arange(SSD_HEADS * SSD_STATE) // SSD_STATE
    heads = jnp.arange(LANES)[:, None]
    return {
        "pre": _rows(pre), "post": _rows(post),
        "win": w_in.astype(BF16),
        "cw": jnp.pad(conv_w, ((0, 0), (0, SUBLANES - SSD_CONV), (0, 0))), "cb": _rows(conv_b),
        "dtb": _rows(_pad_last(dt_bias, LANES)), "alog": _rows(_pad_last(a_log, LANES)),
        "dskip": _rows(jnp.repeat(d_skip, SSD_HEADDIM, axis=1)),
        "wn": _rows(w_norm), "wo": w_out.astype(BF16),
        "expand": (heads == head_of_lane[None, :]).astype(BF16),
        "expand_n": (heads == head_of_lane_n[None, :]).astype(BF16),
    }


def _prompt_trunk(x, gla0, ssm0, conv0, gw, sw, *, n_pad, gla_tile, ssd_tile):
    gla_s = ssm_conv_s = None
    for j in range(N_LAYERS_PER_MIXER):
        x, gla_s = _gla_prompt_layer(x, gla0, gla_s, j, gw, n_pad=n_pad, tile=gla_tile)
        x, *ssm_conv_s = _ssd_prompt_layer(x, ssm0, conv0, ssm_conv_s, j, sw,
                                           n_pad=n_pad, tile=ssd_tile)
    return x, gla_s, ssm_conv_s[0], ssm_conv_s[1]


def kernel(x_prompt, x_sample, state_gla, state_ssm, state_conv, meta_tokens, pre_norm, post_norm,
           gla_w_in, gla_w_gate_up, gla_b_gate, gla_w_norm, gla_w_out,
           ssd_w_in, ssd_conv_w, ssd_conv_b, ssd_dt_bias, ssd_a_log, ssd_d_skip, ssd_w_norm, ssd_w_out):
    gw = _gla_weights(pre_norm[0::2], post_norm[0::2], gla_w_in, gla_w_gate_up, gla_b_gate,
                      gla_w_norm, gla_w_out)
    sw = _ssd_weights(pre_norm[1::2], post_norm[1::2], ssd_w_in, ssd_conv_w, ssd_conv_b,
                      ssd_dt_bias, ssd_a_log, ssd_d_skip, ssd_w_norm, ssd_w_out)

    n_pad = CHUNK - N_META
    x_meta = jnp.pad(meta_tokens.astype(F32), ((n_pad, 0), (0, 0)))[None]
    zg = jnp.zeros((N_LAYERS_PER_MIXER, 1, GLA_HEADS, GLA_HK, GLA_HV), F32)
    zs = jnp.zeros((N_LAYERS_PER_MIXER, 1, SSD_HEADS, SSD_HEADDIM, SSD_STATE), F32)
    zc = jnp.zeros((N_LAYERS_PER_MIXER, 1, SUBLANES, SSD_CONV_DIM), F32)
    _, mg, ms, mc = _prompt_trunk(x_meta, zg, zs, zc, gw, sw,
                                  n_pad=n_pad, gla_tile=CHUNK, ssd_tile=CHUNK)
    y_prompt, gla_p, ssm_p, conv_p = _prompt_trunk(x_prompt, mg, ms, mc, gw, sw, n_pad=0,
                                                   gla_tile=GLA_TILE, ssd_tile=SSD_TILE)
    conv_p = conv_p[:, :, SUBLANES - (SSD_CONV - 1):, :]

    xs = x_sample[:, 0, :]
    sbsz = xs.shape[0]
    convs = state_conv.reshape(N_LAYERS_PER_MIXER, sbsz, (SSD_CONV - 1) * SSD_CONV_DIM)
    gla_s = ssm_conv_s = None
    for j in range(N_LAYERS_PER_MIXER):
        xs, gla_s = _gla_step_layer(xs, state_gla, gla_s, j, gw)
        xs, *ssm_conv_s = _ssd_step_layer(xs, state_ssm, convs, ssm_conv_s, j, sw)
    ssm_s, conv_s = ssm_conv_s
    y_sample = xs[:, None, :]
    return (y_prompt, y_sample, gla_p, ssm_p, conv_p, gla_s, ssm_s,
            conv_s.reshape(N_LAYERS_PER_MIXER, sbsz, SSD_CONV - 1, SSD_CONV_DIM))
```

```python
import functools

import jax
import jax.numpy as jnp
from jax import lax
from jax.experimental import pallas as pl
from jax.experimental.pallas import tpu as pltpu

F32 = jnp.float32
BF16 = jnp.bfloat16

D_MODEL = 1024
N_META = 16
NORM_EPS = 1e-6
N_LAYERS_PER_MIXER = 2

GLA_HEADS = 4
GLA_DK = 512
GLA_DV = 1024
GLA_HK = 128
GLA_HV = 256
GLA_RANK = 16
GLA_TAU = 16.0
GLA_SAFE_DROP = 40.0

SSD_DINNER = 2048
SSD_HEADDIM = 64
SSD_HEADS = 32
SSD_GROUPS = 4
SSD_HPG = 8
SSD_STATE = 128
SSD_CONV = 4
SSD_CONV_DIM = 3072
SSD_GW = SSD_HPG * SSD_HEADDIM

LANES = 128
SUBLANES = 8
CHUNK = 128
GLA_TILE = 1024
SSD_TILE = 512
VMEM_LIMIT = 56 * 1024 * 1024

GLA_Q0, GLA_K0, GLA_V0, GLA_R0, GLA_G0 = 0, GLA_DK, 2 * GLA_DK, 2 * GLA_DK + GLA_DV, 2 * GLA_DK + 2 * GLA_DV
SSD_Z0, SSD_X0, SSD_DT0 = 0, SSD_DINNER, SSD_DINNER + SSD_CONV_DIM


def _narrow_proj(hb, w_ref, start, width):
    res = jnp.dot(hb, w_ref[:, start:start + width], preferred_element_type=F32)
    return jnp.concatenate([res, jnp.zeros((res.shape[0], LANES - width), F32)], axis=1)


def _dot(a, b):
    return jnp.dot(a, b, preferred_element_type=F32)


def _dot_nt(a, b):
    return lax.dot_general(a, b, (((1,), (1,)), ((), ())), preferred_element_type=F32)


def _dot_tn(a, b):
    return lax.dot_general(a, b, (((0,), (0,)), ((), ())), preferred_element_type=F32)


def _rms(x, w):
    return x * lax.rsqrt(jnp.mean(x * x, axis=-1, keepdims=True) + NORM_EPS) * w


def _silu(x):
    h = 0.5 * x
    return h + h * jnp.tanh(h)


def _softplus(x):
    return jnp.maximum(x, 0.0) + jnp.log1p(jnp.exp(-jnp.abs(x)))


def _log_sigmoid(x):
    return -_softplus(-x)


def _tri(n):
    r = lax.broadcasted_iota(jnp.int32, (n, n), 0)
    c = lax.broadcasted_iota(jnp.int32, (n, n), 1)
    return r >= c


def _split3(x):
    hi = x.astype(BF16)
    r1 = x - hi.astype(F32)
    mid = r1.astype(BF16)
    lo = (r1 - mid.astype(F32)).astype(BF16)
    return hi, mid, lo


def _dot_exact_rhs(x, m_bf16):
    hi, mid, lo = _split3(x)
    return _dot(hi, m_bf16) + _dot(mid, m_bf16) + _dot(lo, m_bf16)


def _cumsum_rows(tril_bf16, x):
    hi, mid, lo = _split3(x)
    return _dot(tril_bf16, hi) + _dot(tril_bf16, mid) + _dot(tril_bf16, lo)


def _full(shape):
    nd = len(shape)
    return pl.BlockSpec(shape, lambda *_: (0,) * nd, pipeline_mode=pl.Buffered(1))


def _of_layer(a, layer):
    nz = a.ndim - 1
    return pl.BlockSpec((None,) + a.shape[1:], lambda *_: (layer,) + (0,) * nz,
                        pipeline_mode=pl.Buffered(1))


def _weight_specs(weights, layer):
    return [_of_layer(a, layer) if stacked else _full(a.shape) for a, stacked in weights]


def _batch_block(layer, bb, tail, own=True):
    nz = len(tail)
    return pl.BlockSpec((None, bb) + tail,
                        lambda i, *_: (layer, i if own else 0) + (0,) * nz)


def _gla_attn_pairwise(qf, kf, bc, causal, kbuf_ref, bbuf_ref):
    n = qf.shape[0]
    kbuf_ref[...] = kf
    bbuf_ref[...] = bc
    lane = lax.broadcasted_iota(jnp.int32, (n, n), 1)

    def body(j, acc):
        kj = kbuf_ref[pl.ds(j, 1), :]
        bj = bbuf_ref[pl.ds(j, 1), :]
        col = jnp.sum(qf * jnp.exp(jnp.minimum(bc - bj, 0.0)) * kj, axis=1, keepdims=True)
        return jnp.where(lane == j, col, acc)

    acc = lax.fori_loop(0, n, body, jnp.zeros((n, n), F32))
    return jnp.where(causal, acc, 0.0)


def _gla_scan_tile(q, k, v, g, states, pairwise, kbuf_ref, bbuf_ref):
    causal = _tri(CHUNK)
    tril = causal.astype(BF16)
    states = list(states)
    outs = []
    for c in range(q.shape[0] // CHUNK):
        rows = slice(c * CHUNK, (c + 1) * CHUNK)
        bc = _cumsum_rows(tril, g[rows])
        b_last = bc[-1:, :]
        e_last = jnp.exp(b_last)
        qh = (q[rows] * jnp.exp(bc)).astype(BF16)
        if pairwise:
            kd = (k[rows] * jnp.exp(b_last - bc)).astype(BF16)
        else:
            kh = k[rows] * jnp.exp(-bc)
            kd = (kh * e_last).astype(BF16)
            kh = kh.astype(BF16)
        vb = v[rows].astype(BF16)
        parts = []
        for h in range(GLA_HEADS):
            ks = slice(h * GLA_HK, (h + 1) * GLA_HK)
            vs = slice(h * GLA_HV, (h + 1) * GLA_HV)
            if pairwise:
                attn = _gla_attn_pairwise(q[rows, ks], k[rows, ks], bc[:, ks], causal,
                                          kbuf_ref, bbuf_ref)
            else:
                attn = jnp.where(causal, _dot_nt(qh[:, ks], kh[:, ks]), 0.0)
            st = states[h]
            parts.append(_dot(attn.astype(BF16), vb[:, vs]) + _dot_nt(qh[:, ks], st.astype(BF16)))
            states[h] = st * e_last[:, ks] + _dot_tn(vb[:, vs], kd[:, ks])
        outs.append(jnp.concatenate(parts, axis=1))
    o = jnp.concatenate(outs, axis=0) if len(outs) > 1 else outs[0]
    return o, states


def _gla_tail(o, r, wn, wo, postw, x):
    parts = []
    for h in range(GLA_HEADS):
        vs = slice(h * GLA_HV, (h + 1) * GLA_HV)
        parts.append(_rms(o[:, vs], wn[:, vs]))
    on = jnp.concatenate(parts, axis=1) * _silu(r)
    y = _dot(on.astype(BF16), wo)
    return x + _rms(y, postw)


def _gla_gates(hb, win_ref, wgu_ref, bg_ref):
    gl = _narrow_proj(hb, win_ref, GLA_G0, GLA_RANK)
    return _log_sigmoid(_dot(gl.astype(BF16), wgu_ref[...]) + bg_ref[...]) * (1.0 / GLA_TAU)


def _gla_prompt_kernel(x_ref, s0_ref, prew_ref, postw_ref, win_ref, wgu_ref, bg_ref, wn_ref,
                       wo_ref, *rest, n_pad, tile, has_prev):
    y_ref, sout_ref, st_ref, o_ref, kbuf_ref, bbuf_ref = rest[1:] if has_prev else rest
    t = pl.program_id(1)

    @pl.when(t == 0)
    def _():
        for h in range(GLA_HEADS):
            st_ref[h] = s0_ref[0, h].T

    x = x_ref[0]
    hn = _rms(x, prew_ref[...])
    if n_pad:
        row = lax.broadcasted_iota(jnp.int32, (tile, 1), 0) + t * tile
        hn = jnp.where(row >= n_pad, hn, 0.0)
    hb = hn.astype(BF16)
    q = _dot(hb, win_ref[:, GLA_Q0:GLA_K0]) * (GLA_HK ** -0.5)
    k = _dot(hb, win_ref[:, GLA_K0:GLA_V0])
    v = _dot(hb, win_ref[:, GLA_V0:GLA_R0])
    g = _gla_gates(hb, win_ref, wgu_ref, bg_ref)

    drop = None
    for c in range(tile // CHUNK):
        tot = jnp.sum(g[c * CHUNK:(c + 1) * CHUNK], axis=0, keepdims=True)
        drop = tot if drop is None else jnp.minimum(drop, tot)
    safe = jnp.min(drop) >= -GLA_SAFE_DROP

    def scan(pairwise):
        o, states = _gla_scan_tile(q, k, v, g, [st_ref[h] for h in range(GLA_HEADS)], pairwise,
                                   kbuf_ref, bbuf_ref)
        o_ref[...] = o
        for h in range(GLA_HEADS):
            st_ref[h] = states[h]

    pl.when(safe)(functools.partial(scan, False))
    pl.when(jnp.logical_not(safe))(functools.partial(scan, True))
    r = _dot(hb, win_ref[:, GLA_R0:GLA_G0])
    y_ref[0] = _gla_tail(o_ref[...], r, wn_ref[...], wo_ref[...], postw_ref[...], x)

    @pl.when(t == pl.num_programs(1) - 1)
    def _():
        for h in range(GLA_HEADS):
            sout_ref[0, h] = st_ref[h].T


def _gla_weight_list(w):
    return [(w[k], True) for k in ("pre", "post", "win", "wgu", "bg", "wn", "wo")]


def _gla_prompt_layer(x, s0, prev_out, layer, w, *, n_pad, tile):
    bsz, seq, _ = x.shape
    tail = (GLA_HEADS, GLA_HK, GLA_HV)
    weights = _gla_weight_list(w)
    carried = [] if prev_out is None else [prev_out]
    return pl.pallas_call(
        functools.partial(_gla_prompt_kernel, n_pad=n_pad, tile=tile, has_prev=bool(carried)),
        grid=(bsz, seq // tile),
        in_specs=[pl.BlockSpec((1, tile, D_MODEL), lambda b, t: (b, t, 0)),
                  _batch_block(layer, 1, tail, own=s0.shape[1] == bsz)]
                 + _weight_specs(weights, layer)
                 + [pl.BlockSpec(memory_space=pl.ANY)] * len(carried),
        out_specs=[pl.BlockSpec((1, tile, D_MODEL), lambda b, t: (b, t, 0)),
                   _batch_block(layer, 1, tail)],
        out_shape=[jax.ShapeDtypeStruct(x.shape, F32),
                   jax.ShapeDtypeStruct((N_LAYERS_PER_MIXER, bsz) + tail, F32)],
        scratch_shapes=[pltpu.VMEM((GLA_HEADS, GLA_HV, GLA_HK), F32),
                        pltpu.VMEM((tile, GLA_DV), F32),
                        pltpu.VMEM((CHUNK, GLA_HK), F32),
                        pltpu.VMEM((CHUNK, GLA_HK), F32)],
        input_output_aliases={2 + len(weights): 1} if carried else {},
        compiler_params=pltpu.CompilerParams(
            dimension_semantics=("arbitrary", "arbitrary"), vmem_limit_bytes=VMEM_LIMIT),
        name="gla_prompt",
    )(x, s0, *[a for a, _ in weights], *carried)


UNIT_GROUPS = 4
UNIT_ROWS = UNIT_GROUPS * SUBLANES
WRAP_ROWS = (SSD_CONV - 1) * SUBLANES
D_SLABS = D_MODEL // LANES
assert SSD_CONV - 1 < UNIT_GROUPS and CHUNK % UNIT_ROWS == 0


def _position_time(n):
    p = lax.iota(jnp.int32, n)
    q = p % UNIT_ROWS
    return (p - q) + (q % SUBLANES) * UNIT_GROUPS + q // SUBLANES


def _load_interleaved(slab_refs, tile):
    groups = []
    for u in range(tile // UNIT_ROWS):
        for g in range(UNIT_GROUPS):
            rows = pl.ds(u * UNIT_ROWS + g, SUBLANES, stride=UNIT_GROUPS)
            groups.append(jnp.concatenate([r[0, rows, :] for r in slab_refs], axis=1))
    return jnp.concatenate(groups, axis=0)


def _store_interleaved(y_ref, slab_ref, y, tile):
    for u in range(tile // UNIT_ROWS):
        for g in range(UNIT_GROUPS):
            r0 = u * UNIT_ROWS + g * SUBLANES
            rows = pl.ds(u * UNIT_ROWS + g, SUBLANES, stride=UNIT_GROUPS)
            for j in range(D_SLABS):
                slab_ref[j, rows, :] = y[r0:r0 + SUBLANES, j * LANES:(j + 1) * LANES]
    y_ref[0] = jnp.concatenate([slab_ref[j] for j in range(D_SLABS)], axis=1)


def _ssd_conv(xbc, cbuf_ref, cw, cb, tile):
    sub = lax.broadcasted_iota(jnp.int32, (SUBLANES, 1), 0)
    outs = []
    prev = cbuf_ref[...]
    for u in range(tile // UNIT_ROWS):
        xb = xbc[u * UNIT_ROWS:(u + 1) * UNIT_ROWS]
        last = xb[UNIT_ROWS - WRAP_ROWS:]
        wrapped = []
        for v in range(SSD_CONV - 1):
            rs = slice(v * SUBLANES, (v + 1) * SUBLANES)
            mix = jnp.where(sub == SUBLANES - 1, prev[rs], last[rs])
            wrapped.append(pltpu.roll(mix, 1, axis=0))
        ext = jnp.concatenate(wrapped + [xb], axis=0)
        conv = cb + xb * cw[SSD_CONV - 1:SSD_CONV, :]
        for i in range(SSD_CONV - 1):
            conv = conv + ext[i * SUBLANES:i * SUBLANES + UNIT_ROWS] * cw[i:i + 1, :]
        outs.append(conv)
        prev = last
    cbuf_ref[...] = prev
    return _silu(jnp.concatenate(outs, axis=0))


def _ssd_scan_chunk(xs, bm, cm, dt, a, expand, st_ref, y_ref, row0, causal, tril, lo_half):
    ac = _cumsum_rows(tril, a)
    ac_t = ac.T
    dt_t = dt.T
    a_last = ac[-1:, :]
    wgt = jnp.exp(a_last - ac) * dt
    xw = (xs * _dot(wgt.astype(BF16), expand)).astype(BF16)
    pair = 2 * SSD_HEADDIM
    for gi in range(SSD_GROUPS):
        ns = slice(gi * SSD_STATE, (gi + 1) * SSD_STATE)
        gs = slice(gi * SSD_GW, (gi + 1) * SSD_GW)
        bg = bm[:, ns].astype(BF16)
        cg = cm[:, ns].astype(BF16)
        cb = _dot_nt(cg, bg)
        st = st_ref[gi]
        y_int = _dot(cg, st.astype(BF16))
        e_last = []
        for qi in range(SSD_HPG // 2):
            h0 = gi * SSD_HPG + 2 * qi
            ps = slice(h0 * SSD_HEADDIM, h0 * SSD_HEADDIM + pair)
            ms, eas = [], []
            for e in range(2):
                h = h0 + e
                col = jnp.broadcast_to(ac[:, h:h + 1], (CHUNK, CHUNK))
                seg = col - ac_t[h:h + 1, :]
                ms.append(jnp.where(causal, cb * jnp.exp(seg) * dt_t[h:h + 1, :], 0.0).astype(BF16))
                eas.append(jnp.exp(col))
            xp = xs[:, ps]
            rhs = jnp.concatenate([jnp.where(lo_half, xp, 0.0).astype(BF16),
                                   jnp.where(lo_half, 0.0, xp).astype(BF16)], axis=0)
            ea = jnp.where(lo_half, eas[0], eas[1])
            e_last.append(ea[CHUNK - 1:CHUNK, :])
            y_ref[row0:row0 + CHUNK, ps] = (_dot(jnp.concatenate(ms, axis=1), rhs)
                                            + y_int[:, qi * pair:(qi + 1) * pair] * ea)
        st_ref[gi] = st * jnp.concatenate(e_last, axis=1) + _dot_tn(bg, xw[:, gs])


def _ssd_tail(y, z, wn, wo, postw, x):
    y = y * _silu(z)
    parts = []
    for gi in range(SSD_GROUPS):
        gs = slice(gi * SSD_GW, (gi + 1) * SSD_GW)
        parts.append(_rms(y[:, gs], wn[:, gs]))
    yn = jnp.concatenate(parts, axis=1)
    out = _dot(yn.astype(BF16), wo)
    return x + _rms(out, postw)


def _ssd_prompt_kernel(*refs, n_pad, tile, has_prev):
    x_refs, refs = refs[:D_SLABS], refs[D_SLABS:]
    (s0_ref, c0_ref, prew_ref, postw_ref, win_ref, cw_ref, cb_ref, dtb_ref, alog_ref, dskip_ref,
     wn_ref, wo_ref, expand_ref) = refs[:13]
    y_ref, sout_ref, cout_ref, st_ref, cbuf_ref, ys_ref, slab_ref = refs[13 + (2 if has_prev else 0):]
    t = pl.program_id(1)
    first_buffered = SUBLANES - (SSD_CONV - 1)

    @pl.when(t == 0)
    def _():
        for gi in range(SSD_GROUPS):
            s0 = s0_ref[0, gi * SSD_HPG:(gi + 1) * SSD_HPG].reshape(SSD_GW, SSD_STATE)
            st_ref[gi] = s0.T
        cbuf_ref[...] = jnp.zeros(cbuf_ref.shape, F32)
        for v in range(SSD_CONV - 1):
            r = (v + 1) * SUBLANES - 1
            cbuf_ref[r:r + 1, :] = c0_ref[0, first_buffered + v:first_buffered + v + 1, :]

    x = _load_interleaved(x_refs, tile)
    hn = _rms(x, prew_ref[...])
    if n_pad:
        row = t * tile + _position_time(tile)[:, None]
        hn = jnp.where(row >= n_pad, hn, 0.0)
    hb = hn.astype(BF16)
    xbc = _dot(hb, win_ref[:, SSD_X0:SSD_DT0])
    dtr = _narrow_proj(hb, win_ref, SSD_DT0, SSD_HEADS)
    xc = _ssd_conv(xbc, cbuf_ref, cw_ref[...], cb_ref[...], tile)
    z = _dot(hb, win_ref[:, SSD_Z0:SSD_X0])
    xs = xc[:, :SSD_DINNER]
    bm = xc[:, SSD_DINNER:SSD_DINNER + SSD_GROUPS * SSD_STATE]
    cm = xc[:, SSD_DINNER + SSD_GROUPS * SSD_STATE:]
    dt = _softplus(dtr + dtb_ref[...])
    if n_pad:
        dt = jnp.where(row >= n_pad, dt, 0.0)
    a = dt * (-jnp.exp(alog_ref[...]))

    tm = _position_time(CHUNK)
    causal = tm[:, None] >= tm[None, :]
    tril = causal.astype(BF16)
    lo_half = lax.broadcasted_iota(jnp.int32, (CHUNK, 2 * SSD_HEADDIM), 1) < SSD_HEADDIM
    expand = expand_ref[...]
    for c in range(tile // CHUNK):
        rows = slice(c * CHUNK, (c + 1) * CHUNK)
        _ssd_scan_chunk(xs[rows], bm[rows], cm[rows], dt[rows], a[rows], expand,
                        st_ref, ys_ref, c * CHUNK, causal, tril, lo_half)
    y = ys_ref[...] + xs * dskip_ref[...]
    out = _ssd_tail(y, z, wn_ref[...], wo_ref[...], postw_ref[...], x)
    _store_interleaved(y_ref, slab_ref, out, tile)

    @pl.when(t == pl.num_programs(1) - 1)
    def _():
        for gi in range(SSD_GROUPS):
            sout_ref[0, gi * SSD_HPG:(gi + 1) * SSD_HPG] = (
                st_ref[gi].T.reshape(SSD_HPG, SSD_HEADDIM, SSD_STATE))
        cout_ref[0] = jnp.zeros(cout_ref.shape[1:], F32)
        for v in range(SSD_CONV - 1):
            r = (v + 1) * SUBLANES - 1
            cout_ref[0, first_buffered + v:first_buffered + v + 1, :] = cbuf_ref[r:r + 1, :]


def _ssd_weight_list(w, step):
    names = ("pre", "post", "win", "cw", "cb", "dtb", "alog", "dskip", "wn", "wo")
    consts = ("expand", "expand_n") if step else ("expand",)
    return [(w[k], True) for k in names] + [(w[k], False) for k in consts]


def _ssd_prompt_layer(x, s0, c0, prev_out, layer, w, *, n_pad, tile):
    bsz, seq, _ = x.shape
    s_tail = (SSD_HEADS, SSD_HEADDIM, SSD_STATE)
    c_tail = (SUBLANES, SSD_CONV_DIM)
    own = s0.shape[1] == bsz
    weights = _ssd_weight_list(w, step=False)
    carried = [] if prev_out is None else list(prev_out)
    n_in = D_SLABS + 2 + len(weights)
    x_slabs = [pl.BlockSpec((1, tile, LANES), lambda b, t, j=j: (b, t, j)) for j in range(D_SLABS)]
    return pl.pallas_call(
        functools.partial(_ssd_prompt_kernel, n_pad=n_pad, tile=tile, has_prev=bool(carried)),
        grid=(bsz, seq // tile),
        in_specs=x_slabs
                 + [_batch_block(layer, 1, s_tail, own), _batch_block(layer, 1, c_tail, own)]
                 + _weight_specs(weights, layer)
                 + [pl.BlockSpec(memory_space=pl.ANY)] * len(carried),
        out_specs=[pl.BlockSpec((1, tile, D_MODEL), lambda b, t: (b, t, 0)),
                   _batch_block(layer, 1, s_tail), _batch_block(layer, 1, c_tail)],
        out_shape=[jax.ShapeDtypeStruct(x.shape, F32),
                   jax.ShapeDtypeStruct((N_LAYERS_PER_MIXER, bsz) + s_tail, F32),
                   jax.ShapeDtypeStruct((N_LAYERS_PER_MIXER, bsz) + c_tail, F32)],
        scratch_shapes=[pltpu.VMEM((SSD_GROUPS, SSD_STATE, SSD_GW), F32),
                        pltpu.VMEM((WRAP_ROWS, SSD_CONV_DIM), F32),
                        pltpu.VMEM((tile, SSD_DINNER), F32),
                        pltpu.VMEM((D_SLABS, tile, LANES), F32)],
        input_output_aliases={n_in: 1, n_in + 1: 2} if carried else {},
        compiler_params=pltpu.CompilerParams(
            dimension_semantics=("arbitrary", "arbitrary"), vmem_limit_bytes=VMEM_LIMIT),
        name="ssd_prompt",
    )(*([x] * D_SLABS), s0, c0, *[a for a, _ in weights], *carried)


def _column(row, width, lane0=0):
    seg = row[:, lane0:lane0 + width]
    r = lax.broadcasted_iota(jnp.int32, (width, width), 0)
    c = lax.broadcasted_iota(jnp.int32, (width, width), 1)
    return jnp.sum(jnp.where(r == c, seg, 0.0), axis=1, keepdims=True)


def _gla_step_kernel(x_ref, s_ref, prew_ref, postw_ref, win_ref, wgu_ref, bg_ref, wn_ref, wo_ref,
                     *rest, bb, has_prev):
    y_ref, sout_ref, q_s, kl_s, v_s, e_s, o_s = rest[1:] if has_prev else rest
    i = pl.program_id(0)
    bsz = x_ref.shape[0]

    @pl.when(i == 0)
    def _():
        hb = _rms(x_ref[...], prew_ref[...]).astype(BF16)
        q_s[...] = _dot(hb, win_ref[:, GLA_Q0:GLA_K0]) * (GLA_HK ** -0.5)
        o_s[:, 0:GLA_DK] = _dot(hb, win_ref[:, GLA_K0:GLA_V0])
        k_t = o_s[:, 0:GLA_DK].T
        for h in range(GLA_HEADS):
            kl_s[h // 2, :, (h % 2) * bsz:(h % 2 + 1) * bsz] = (
                k_t[h * GLA_HK:(h + 1) * GLA_HK].astype(BF16))
        v_s[...] = _dot(hb, win_ref[:, GLA_V0:GLA_R0])
        e_s[...] = jnp.exp(_gla_gates(hb, win_ref, wgu_ref, bg_ref))

    sub = lax.broadcasted_iota(jnp.int32, (bsz, GLA_HV), 0)
    zero_bv = jnp.zeros((bsz, GLA_HV), BF16)

    for j in range(bb):
        row = i * bb + j
        qr = q_s[pl.ds(row, 1), :]
        er = e_s[pl.ds(row, 1), :]
        for p in range(GLA_HEADS // 2):
            vsel = [jnp.where(sub == row, v_s[:, (2 * p + e) * GLA_HV:(2 * p + e + 1) * GLA_HV],
                              0.0).astype(BF16) for e in range(2)]
            rhs = jnp.concatenate([jnp.concatenate([vsel[0], zero_bv], axis=1),
                                   jnp.concatenate([zero_bv, vsel[1]], axis=1)], axis=0)
            kv = _dot(kl_s[p], rhs)
            for e in range(2):
                h = 2 * p + e
                vs = slice(h * GLA_HV, (h + 1) * GLA_HV)
                ecol = _column(er, GLA_HK, h * GLA_HK)
                qcol = _column(qr, GLA_HK, h * GLA_HK)
                s_new = s_ref[j, h] * ecol + kv[:, e * GLA_HV:(e + 1) * GLA_HV]
                sout_ref[j, h] = s_new
                o_s[pl.ds(row, 1), vs] = jnp.sum(qcol * s_new, axis=0, keepdims=True)

    @pl.when(i == pl.num_programs(0) - 1)
    def _():
        x = x_ref[...]
        hb = _rms(x, prew_ref[...]).astype(BF16)
        r = _dot(hb, win_ref[:, GLA_R0:GLA_G0])
        y_ref[...] = _gla_tail(o_s[...], r, wn_ref[...], wo_ref[...], postw_ref[...], x)


def _gla_step_layer(x, states, prev_out, layer, w, *, bb=8):
    bsz = x.shape[0]
    weights = _gla_weight_list(w)
    sblk = _batch_block(layer, bb, (GLA_HEADS, GLA_HK, GLA_HV))
    carried = [] if prev_out is None else [prev_out]
    return pl.pallas_call(
        functools.partial(_gla_step_kernel, bb=bb, has_prev=bool(carried)),
        grid=(bsz // bb,),
        in_specs=[_full(x.shape), sblk] + _weight_specs(weights, layer)
                 + [pl.BlockSpec(memory_space=pl.ANY)] * len(carried),
        out_specs=[pl.BlockSpec(x.shape, lambda i: (0, 0)), sblk],
        out_shape=[jax.ShapeDtypeStruct(x.shape, F32), jax.ShapeDtypeStruct(states.shape, F32)],
        scratch_shapes=[pltpu.VMEM((bsz, GLA_DK), F32),
                        pltpu.VMEM((GLA_HEADS // 2, GLA_HK, 2 * bsz), BF16),
                        pltpu.VMEM((bsz, GLA_DV), F32),
                        pltpu.VMEM((bsz, GLA_DK), F32),
                        pltpu.VMEM((bsz, GLA_DV), F32)],
        input_output_aliases={2 + len(weights): 1} if carried else {},
        compiler_params=pltpu.CompilerParams(
            dimension_semantics=("arbitrary",), vmem_limit_bytes=VMEM_LIMIT),
        name="gla_step",
    )(x, states, *[a for a, _ in weights], *carried)


def _ssd_step_kernel(x_ref, s_ref, cv_ref, prew_ref, postw_ref, win_ref, cw_ref, cb_ref, dtb_ref,
                     alog_ref, dskip_ref, wn_ref, wo_ref, expand_ref, expand_n_ref, *rest,
                     bb, has_prev):
    (y_ref, sout_ref, cvout_ref, xs_s, xl_s, ct_s, b_s, e_s, yt_s) = rest[2:] if has_prev else rest
    i = pl.program_id(0)
    bsz = x_ref.shape[0]
    cd = SSD_CONV_DIM

    @pl.when(i == 0)
    def _():
        hb = _rms(x_ref[...], prew_ref[...]).astype(BF16)
        xbc = _dot(hb, win_ref[:, SSD_X0:SSD_DT0])
        dtr = _narrow_proj(hb, win_ref, SSD_DT0, SSD_HEADS)
        cw = cw_ref[...]
        conv = cb_ref[...] + xbc * cw[3:4, :]
        for t in range(SSD_CONV - 1):
            conv = conv + cv_ref[:, t * cd:(t + 1) * cd] * cw[t:t + 1, :]
        cvout_ref[:, 0:cd] = cv_ref[:, cd:2 * cd]
        cvout_ref[:, cd:2 * cd] = cv_ref[:, 2 * cd:3 * cd]
        cvout_ref[:, 2 * cd:3 * cd] = xbc
        xc = _silu(conv)
        xs = xc[:, :SSD_DINNER]
        dt = _softplus(dtr + dtb_ref[...])
        ea = jnp.exp(dt * (-jnp.exp(alog_ref[...])))
        xs_s[...] = xs
        xd_t = (xs * _dot_exact_rhs(dt, expand_ref[...])).T
        c_t = xc[:, SSD_DINNER + SSD_GROUPS * SSD_STATE:].T
        for p in range(SSD_GROUPS // 2):
            for e in range(2):
                gi = 2 * p + e
                xl_s[p, :, e * bsz:(e + 1) * bsz] = (
                    xd_t[gi * SSD_GW:(gi + 1) * SSD_GW].astype(BF16))
        ct_s[...] = c_t
        b_s[...] = xc[:, SSD_DINNER:SSD_DINNER + SSD_GROUPS * SSD_STATE]
        e_s[...] = _dot_exact_rhs(ea, expand_n_ref[...])
        yt_s[...] = jnp.zeros(yt_s.shape, F32)

    sub = lax.broadcasted_iota(jnp.int32, (bsz, SSD_STATE), 0)
    lane = lax.broadcasted_iota(jnp.int32, (SSD_STATE, bsz), 1)
    zero_bn = jnp.zeros((bsz, SSD_STATE), BF16)
    zero_nb = jnp.zeros((SSD_STATE, bsz), BF16)

    for p in range(SSD_GROUPS // 2):
        y_acc = None
        for j in range(bb):
            row = i * bb + j
            er = e_s[pl.ds(row, 1), :]
            bsel, csel = [], []
            for e in range(2):
                ns = slice((2 * p + e) * SSD_STATE, (2 * p + e + 1) * SSD_STATE)
                bsel.append(jnp.where(sub == row, b_s[:, ns], 0.0).astype(BF16))
                csel.append(jnp.where(lane == row, ct_s[ns, :], 0.0).astype(BF16))
            rhs_b = jnp.concatenate([jnp.concatenate([bsel[0], zero_bn], axis=1),
                                     jnp.concatenate([zero_bn, bsel[1]], axis=1)], axis=0)
            rhs_c = jnp.concatenate([jnp.concatenate([csel[0], zero_nb], axis=1),
                                     jnp.concatenate([zero_nb, csel[1]], axis=1)], axis=0)
            outer = _dot(xl_s[p], rhs_b)
            halves = []
            for e in range(2):
                pieces = []
                for r in range(SSD_HPG):
                    h = (2 * p + e) * SSD_HPG + r
                    s_new = (s_ref[j, h] * er[:, h * SSD_STATE:(h + 1) * SSD_STATE]
                             + outer[r * SSD_HEADDIM:(r + 1) * SSD_HEADDIM,
                                     e * SSD_STATE:(e + 1) * SSD_STATE])
                    sout_ref[j, h] = s_new
                    pieces.append(s_new.astype(BF16))
                halves.append(jnp.concatenate(pieces, axis=0))
            y_j = _dot(jnp.concatenate(halves, axis=1), rhs_c)
            y_acc = y_j if y_acc is None else y_acc + y_j
        yt_s[p] += y_acc

    @pl.when(i == pl.num_programs(0) - 1)
    def _():
        x = x_ref[...]
        hb = _rms(x, prew_ref[...]).astype(BF16)
        z = _dot(hb, win_ref[:, SSD_Z0:SSD_X0])
        y = jnp.concatenate([yt_s[p, :, e * bsz:(e + 1) * bsz].T
                             for p in range(SSD_GROUPS // 2) for e in range(2)], axis=1)
        y = y + xs_s[...] * dskip_ref[...]
        y_ref[...] = _ssd_tail(y, z, wn_ref[...], wo_ref[...], postw_ref[...], x)


def _ssd_step_layer(x, states, convs, prev_out, layer, w, *, bb=4):
    bsz = x.shape[0]
    weights = _ssd_weight_list(w, step=True)
    sblk = _batch_block(layer, bb, (SSD_HEADS, SSD_HEADDIM, SSD_STATE))
    cblk = pl.BlockSpec((None,) + convs.shape[1:], lambda i: (layer, 0, 0))
    carried = [] if prev_out is None else list(prev_out)
    n_in = 3 + len(weights)
    return pl.pallas_call(
        functools.partial(_ssd_step_kernel, bb=bb, has_prev=bool(carried)),
        grid=(bsz // bb,),
        in_specs=[_full(x.shape), sblk,
                  pl.BlockSpec(cblk.block_shape, cblk.index_map, pipeline_mode=pl.Buffered(1))]
                 + _weight_specs(weights, layer)
                 + [pl.BlockSpec(memory_space=pl.ANY)] * len(carried),
        out_specs=[pl.BlockSpec(x.shape, lambda i: (0, 0)), sblk, cblk],
        out_shape=[jax.ShapeDtypeStruct(x.shape, F32), jax.ShapeDtypeStruct(states.shape, F32),
                   jax.ShapeDtypeStruct(convs.shape, F32)],
        input_output_aliases={n_in: 1, n_in + 1: 2} if carried else {},
        scratch_shapes=[pltpu.VMEM((bsz, SSD_DINNER), F32),
                        pltpu.VMEM((SSD_GROUPS // 2, SSD_GW, 2 * bsz), BF16),
                        pltpu.VMEM((SSD_GROUPS * SSD_STATE, bsz), F32),
                        pltpu.VMEM((bsz, SSD_GROUPS * SSD_STATE), F32),
                        pltpu.VMEM((bsz, SSD_HEADS * SSD_STATE), F32),
                        pltpu.VMEM((SSD_GROUPS // 2, SSD_GW, 2 * bsz), F32)],
        compiler_params=pltpu.CompilerParams(
            dimension_semantics=("arbitrary",), vmem_limit_bytes=VMEM_LIMIT),
        name="ssd_step",
    )(x, states, convs, *[a for a, _ in weights], *carried)


def _pad_last(a, n):
    return jnp.pad(a, [(0, 0)] * (a.ndim - 1) + [(0, n - a.shape[-1])])


def _rows(a):
    return a[:, None, :]


def _gla_weights(pre, post, w_in, w_gate_up, b_gate, w_norm, w_out):
    return {
        "pre": _rows(pre), "post": _rows(post),
        "win": w_in.astype(BF16),
        "wgu": jnp.pad(w_gate_up, ((0, 0), (0, LANES - GLA_RANK), (0, 0))).astype(BF16),
        "bg": _rows(b_gate), "wn": _rows(w_norm), "wo": w_out.astype(BF16),
    }


def _ssd_weights(pre, post, w_in, conv_w, conv_b, dt_bias, a_log, d_skip, w_norm, w_out):
    head_of_lane = jnp.arange(SSD_DINNER) // SSD_HEADDIM
    head_of_lane_n = jnp.arange(SSD_HEADS * SSD_STATE) // SSD_STATE
    heads = jnp.arange(LANES)[:, None]
    return {
        "pre": _rows(pre), "post": _rows(post),
        "win": w_in.astype(BF16),
        "cw": jnp.pad(conv_w, ((0, 0), (0, SUBLANES - SSD_CONV), (0, 0))), "cb": _rows(conv_b),
        "dtb": _rows(_pad_last(dt_bias, LANES)), "alog": _rows(_pad_last(a_log, LANES)),
        "dskip": _rows(jnp.repeat(d_skip, SSD_HEADDIM, axis=1)),
        "wn": _rows(w_norm), "wo": w_out.astype(BF16),
        "expand": (heads == head_of_lane[None, :]).astype(BF16),
        "expand_n": (heads == head_of_lane_n[None, :]).astype(BF16),
    }


def _prompt_trunk(x, gla0, ssm0, conv0, gw, sw, *, n_pad, gla_tile, ssd_tile):
    gla_s = ssm_conv_s = None
    for j in range(N_LAYERS_PER_MIXER):
        x, gla_s = _gla_prompt_layer(x, gla0, gla_s, j, gw, n_pad=n_pad, tile=gla_tile)
        x, *ssm_conv_s = _ssd_prompt_layer(x, ssm0, conv0, ssm_conv_s, j, sw,
                                           n_pad=n_pad, tile=ssd_tile)
    return x, gla_s, ssm_conv_s[0], ssm_conv_s[1]


def kernel(x_prompt, x_sample, state_gla, state_ssm, state_conv, meta_tokens, pre_norm, post_norm,
           gla_w_in, gla_w_gate_up, gla_b_gate, gla_w_norm, gla_w_out,
           ssd_w_in, ssd_conv_w, ssd_conv_b, ssd_dt_bias, ssd_a_log, ssd_d_skip, ssd_w_norm, ssd_w_out):
    gw = _gla_weights(pre_norm[0::2], post_norm[0::2], gla_w_in, gla_w_gate_up, gla_b_gate,
                      gla_w_norm, gla_w_out)
    sw = _ssd_weights(pre_norm[1::2], post_norm[1::2], ssd_w_in, ssd_conv_w, ssd_conv_b,
                      ssd_dt_bias, ssd_a_log, ssd_d_skip, ssd_w_norm, ssd_w_out)

    n_pad = CHUNK - N_META
    x_meta = jnp.pad(meta_tokens.astype(F32), ((n_pad, 0), (0, 0)))[None]
    zg = jnp.zeros((N_LAYERS_PER_MIXER, 1, GLA_HEADS, GLA_HK, GLA_HV), F32)
    zs = jnp.zeros((N_LAYERS_PER_MIXER, 1, SSD_HEADS, SSD_HEADDIM, SSD_STATE), F32)
    zc = jnp.zeros((N_LAYERS_PER_MIXER, 1, SUBLANES, SSD_CONV_DIM), F32)
    _, mg, ms, mc = _prompt_trunk(x_meta, zg, zs, zc, gw, sw,
                                  n_pad=n_pad, gla_tile=CHUNK, ssd_tile=CHUNK)
    y_prompt, gla_p, ssm_p, conv_p = _prompt_trunk(x_prompt, mg, ms, mc, gw, sw, n_pad=0,
                                                   gla_tile=GLA_TILE, ssd_tile=SSD_TILE)
    conv_p = conv_p[:, :, SUBLANES - (SSD_CONV - 1):, :]

    xs = x_sample[:, 0, :]
    sbsz = xs.shape[0]
    convs = state_conv.reshape(N_LAYERS_PER_MIXER, sbsz, (SSD_CONV - 1) * SSD_CONV_DIM)
    gla_s = ssm_conv_s = None
    for j in range(N_LAYERS_PER_MIXER):
        xs, gla_s = _gla_step_layer(xs, state_gla, gla_s, j, gw)
        xs, *ssm_conv_s = _ssd_step_layer(xs, state_ssm, convs, ssm_conv_s, j, sw)
    ssm_s, conv_s = ssm_conv_s
    y_sample = xs[:, None, :]
    return (y_prompt, y_sample, gla_p, ssm_p, conv_p, gla_s, ssm_s,
            conv_s.reshape(N_LAYERS_PER_MIXER, sbsz, SSD_CONV - 1, SSD_CONV_DIM))
```

```python
import functools

import jax
import jax.numpy as jnp
from jax import lax
from jax.experimental import pallas as pl
from jax.experimental.pallas import tpu as pltpu

F32 = jnp.float32
BF16 = jnp.bfloat16

D_MODEL = 1024
N_META = 16
NORM_EPS = 1e-6
N_LAYERS_PER_MIXER = 2

GLA_HEADS = 4
GLA_DK = 512
GLA_DV = 1024
GLA_HK = 128
GLA_HV = 256
GLA_RANK = 16
GLA_TAU = 16.0
GLA_SAFE_DROP = 40.0

SSD_DINNER = 2048
SSD_HEADDIM = 64
SSD_HEADS = 32
SSD_GROUPS = 4
SSD_HPG = 8
SSD_STATE = 128
SSD_CONV = 4
SSD_CONV_DIM = 3072
SSD_GW = SSD_HPG * SSD_HEADDIM

LANES = 128
SUBLANES = 8
CHUNK = 128
GLA_TILE = 1024
SSD_TILE = 512
VMEM_LIMIT = 56 * 1024 * 1024

GLA_Q0, GLA_K0, GLA_V0, GLA_R0, GLA_G0 = 0, GLA_DK, 2 * GLA_DK, 2 * GLA_DK + GLA_DV, 2 * GLA_DK + 2 * GLA_DV
SSD_Z0, SSD_X0, SSD_DT0 = 0, SSD_DINNER, SSD_DINNER + SSD_CONV_DIM


def _narrow_proj(hb, w_ref, start, width):
    res = jnp.dot(hb, w_ref[:, start:start + width], preferred_element_type=F32)
    return jnp.concatenate([res, jnp.zeros((res.shape[0], LANES - width), F32)], axis=1)


def _dot(a, b):
    return jnp.dot(a, b, preferred_element_type=F32)


def _dot_nt(a, b):
    return lax.dot_general(a, b, (((1,), (1,)), ((), ())), preferred_element_type=F32)


def _dot_tn(a, b):
    return lax.dot_general(a, b, (((0,), (0,)), ((), ())), preferred_element_type=F32)


def _rms(x, w):
    return x * lax.rsqrt(jnp.mean(x * x, axis=-1, keepdims=True) + NORM_EPS) * w


def _silu(x):
    h = 0.5 * x
    return h + h * jnp.tanh(h)


def _softplus(x):
    return jnp.maximum(x, 0.0) + jnp.log1p(jnp.exp(-jnp.abs(x)))


def _log_sigmoid(x):
    return -_softplus(-x)


def _tri(n):
    r = lax.broadcasted_iota(jnp.int32, (n, n), 0)
    c = lax.broadcasted_iota(jnp.int32, (n, n), 1)
    return r >= c


def _split3(x):
    hi = x.astype(BF16)
    r1 = x - hi.astype(F32)
    mid = r1.astype(BF16)
    lo = (r1 - mid.astype(F32)).astype(BF16)
    return hi, mid, lo


def _dot_exact_rhs(x, m_bf16):
    hi, mid, lo = _split3(x)
    return _dot(hi, m_bf16) + _dot(mid, m_bf16) + _dot(lo, m_bf16)


def _cumsum_rows(tril_bf16, x):
    hi, mid, lo = _split3(x)
    return _dot(tril_bf16, hi) + _dot(tril_bf16, mid) + _dot(tril_bf16, lo)


def _full(shape):
    nd = len(shape)
    return pl.BlockSpec(shape, lambda *_: (0,) * nd, pipeline_mode=pl.Buffered(1))


def _of_layer(a, layer):
    nz = a.ndim - 1
    return pl.BlockSpec((None,) + a.shape[1:], lambda *_: (layer,) + (0,) * nz,
                        pipeline_mode=pl.Buffered(1))


def _weight_specs(weights, layer):
    return [_of_layer(a, layer) if stacked else _full(a.shape) for a, stacked in weights]


def _batch_block(layer, bb, tail, own=True):
    nz = len(tail)
    return pl.BlockSpec((None, bb) + tail,
                        lambda i, *_: (layer, i if own else 0) + (0,) * nz)


def _gla_attn_pairwise(qf, kf, bc, causal, kbuf_ref, bbuf_ref):
    n = qf.shape[0]
    kbuf_ref[...] = kf
    bbuf_ref[...] = bc
    lane = lax.broadcasted_iota(jnp.int32, (n, n), 1)

    def body(j, acc):
        kj = kbuf_ref[pl.ds(j, 1), :]
        bj = bbuf_ref[pl.ds(j, 1), :]
        col = jnp.sum(qf * jnp.exp(jnp.minimum(bc - bj, 0.0)) * kj, axis=1, keepdims=True)
        return jnp.where(lane == j, col, acc)

    acc = lax.fori_loop(0, n, body, jnp.zeros((n, n), F32))
    return jnp.where(causal, acc, 0.0)


def _gla_scan_tile(q, k, v, g, states, pairwise, kbuf_ref, bbuf_ref):
    causal = _tri(CHUNK)
    tril = causal.astype(BF16)
    states = list(states)
    outs = []
    for c in range(q.shape[0] // CHUNK):
        rows = slice(c * CHUNK, (c + 1) * CHUNK)
        bc = _cumsum_rows(tril, g[rows])
        b_last = bc[-1:, :]
        e_last = jnp.exp(b_last)
        qh = (q[rows] * jnp.exp(bc)).astype(BF16)
        if pairwise:
            kd = (k[rows] * jnp.exp(b_last - bc)).astype(BF16)
        else:
            kh = k[rows] * jnp.exp(-bc)
            kd = (kh * e_last).astype(BF16)
            kh = kh.astype(BF16)
        vb = v[rows].astype(BF16)
        zk = jnp.zeros((CHUNK, GLA_HK), BF16)
        parts = []
        for h0 in range(0, GLA_HEADS, 2):
            ks = [slice((h0 + e) * GLA_HK, (h0 + e + 1) * GLA_HK) for e in range(2)]
            vs = [slice((h0 + e) * GLA_HV, (h0 + e + 1) * GLA_HV) for e in range(2)]
            ks2 = slice(ks[0].start, ks[1].stop)
            if pairwise:
                attn = [_gla_attn_pairwise(q[rows, ks[e]], k[rows, ks[e]], bc[:, ks[e]], causal,
                                           kbuf_ref, bbuf_ref) for e in range(2)]
            else:
                k_bd = jnp.concatenate([jnp.concatenate([kh[:, ks[0]], zk], axis=1),
                                        jnp.concatenate([zk, kh[:, ks[1]]], axis=1)], axis=0)
                scores = _dot_nt(qh[:, ks2], k_bd)
                attn = [jnp.where(causal, scores[:, e * CHUNK:(e + 1) * CHUNK], 0.0)
                        for e in range(2)]
            kd_bd = jnp.concatenate([jnp.concatenate([kd[:, ks[0]], zk], axis=1),
                                     jnp.concatenate([zk, kd[:, ks[1]]], axis=1)], axis=0)
            upd = _dot_tn(jnp.concatenate([vb[:, vs[0]], vb[:, vs[1]]], axis=0), kd_bd)
            for e in range(2):
                st = states[h0 + e]
                parts.append(_dot(attn[e].astype(BF16), vb[:, vs[e]])
                             + _dot_nt(qh[:, ks[e]], st.astype(BF16)))
                states[h0 + e] = st * e_last[:, ks[e]] + upd[:, e * GLA_HK:(e + 1) * GLA_HK]
        outs.append(jnp.concatenate(parts, axis=1))
    o = jnp.concatenate(outs, axis=0) if len(outs) > 1 else outs[0]
    return o, states


def _gla_tail(o, r, wn, wo, postw, x):
    parts = []
    for h in range(GLA_HEADS):
        vs = slice(h * GLA_HV, (h + 1) * GLA_HV)
        parts.append(_rms(o[:, vs], wn[:, vs]))
    on = jnp.concatenate(parts, axis=1) * _silu(r)
    y = _dot(on.astype(BF16), wo)
    return x + _rms(y, postw)


def _gla_gates(hb, win_ref, wgu_ref, bg_ref):
    gl = _narrow_proj(hb, win_ref, GLA_G0, GLA_RANK)
    return _log_sigmoid(_dot(gl.astype(BF16), wgu_ref[...]) + bg_ref[...]) * (1.0 / GLA_TAU)


def _gla_prompt_kernel(x_ref, s0_ref, prew_ref, postw_ref, win_ref, wgu_ref, bg_ref, wn_ref,
                       wo_ref, *rest, n_pad, tile, has_prev):
    y_ref, sout_ref, st_ref, o_ref, kbuf_ref, bbuf_ref = rest[1:] if has_prev else rest
    t = pl.program_id(1)

    @pl.when(t == 0)
    def _():
        for h in range(GLA_HEADS):
            st_ref[h] = s0_ref[0, h].T

    x = x_ref[0]
    hn = _rms(x, prew_ref[...])
    if n_pad:
        row = lax.broadcasted_iota(jnp.int32, (tile, 1), 0) + t * tile
        hn = jnp.where(row >= n_pad, hn, 0.0)
    hb = hn.astype(BF16)
    q = _dot(hb, win_ref[:, GLA_Q0:GLA_K0]) * (GLA_HK ** -0.5)
    k = _dot(hb, win_ref[:, GLA_K0:GLA_V0])
    v = _dot(hb, win_ref[:, GLA_V0:GLA_R0])
    g = _gla_gates(hb, win_ref, wgu_ref, bg_ref)

    drop = None
    for c in range(tile // CHUNK):
        tot = jnp.sum(g[c * CHUNK:(c + 1) * CHUNK], axis=0, keepdims=True)
        drop = tot if drop is None else jnp.minimum(drop, tot)
    safe = jnp.min(drop) >= -GLA_SAFE_DROP

    def scan(pairwise):
        o, states = _gla_scan_tile(q, k, v, g, [st_ref[h] for h in range(GLA_HEADS)], pairwise,
                                   kbuf_ref, bbuf_ref)
        o_ref[...] = o
        for h in range(GLA_HEADS):
            st_ref[h] = states[h]

    pl.when(safe)(functools.partial(scan, False))
    pl.when(jnp.logical_not(safe))(functools.partial(scan, True))
    r = _dot(hb, win_ref[:, GLA_R0:GLA_G0])
    y_ref[0] = _gla_tail(o_ref[...], r, wn_ref[...], wo_ref[...], postw_ref[...], x)

    @pl.when(t == pl.num_programs(1) - 1)
    def _():
        for h in range(GLA_HEADS):
            sout_ref[0, h] = st_ref[h].T


def _gla_weight_list(w):
    return [(w[k], True) for k in ("pre", "post", "win", "wgu", "bg", "wn", "wo")]


def _gla_prompt_layer(x, s0, prev_out, layer, w, *, n_pad, tile):
    bsz, seq, _ = x.shape
    tail = (GLA_HEADS, GLA_HK, GLA_HV)
    weights = _gla_weight_list(w)
    carried = [] if prev_out is None else [prev_out]
    return pl.pallas_call(
        functools.partial(_gla_prompt_kernel, n_pad=n_pad, tile=tile, has_prev=bool(carried)),
        grid=(bsz, seq // tile),
        in_specs=[pl.BlockSpec((1, tile, D_MODEL), lambda b, t: (b, t, 0)),
                  _batch_block(layer, 1, tail, own=s0.shape[1] == bsz)]
                 + _weight_specs(weights, layer)
                 + [pl.BlockSpec(memory_space=pl.ANY)] * len(carried),
        out_specs=[pl.BlockSpec((1, tile, D_MODEL), lambda b, t: (b, t, 0)),
                   _batch_block(layer, 1, tail)],
        out_shape=[jax.ShapeDtypeStruct(x.shape, F32),
                   jax.ShapeDtypeStruct((N_LAYERS_PER_MIXER, bsz) + tail, F32)],
        scratch_shapes=[pltpu.VMEM((GLA_HEADS, GLA_HV, GLA_HK), F32),
                        pltpu.VMEM((tile, GLA_DV), F32),
                        pltpu.VMEM((CHUNK, GLA_HK), F32),
                        pltpu.VMEM((CHUNK, GLA_HK), F32)],
        input_output_aliases={2 + len(weights): 1} if carried else {},
        compiler_params=pltpu.CompilerParams(
            dimension_semantics=("arbitrary", "arbitrary"), vmem_limit_bytes=VMEM_LIMIT),
        name="gla_prompt",
    )(x, s0, *[a for a, _ in weights], *carried)


UNIT_GROUPS = 4
UNIT_ROWS = UNIT_GROUPS * SUBLANES
WRAP_ROWS = (SSD_CONV - 1) * SUBLANES
D_SLABS = D_MODEL // LANES
assert SSD_CONV - 1 < UNIT_GROUPS and CHUNK % UNIT_ROWS == 0


def _position_time(n):
    p = lax.iota(jnp.int32, n)
    q = p % UNIT_ROWS
    return (p - q) + (q % SUBLANES) * UNIT_GROUPS + q // SUBLANES


def _load_interleaved(slab_refs, tile):
    groups = []
    for u in range(tile // UNIT_ROWS):
        for g in range(UNIT_GROUPS):
            rows = pl.ds(u * UNIT_ROWS + g, SUBLANES, stride=UNIT_GROUPS)
            groups.append(jnp.concatenate([r[0, rows, :] for r in slab_refs], axis=1))
    return jnp.concatenate(groups, axis=0)


def _store_interleaved(y_ref, slab_ref, y, tile):
    for u in range(tile // UNIT_ROWS):
        for g in range(UNIT_GROUPS):
            r0 = u * UNIT_ROWS + g * SUBLANES
            rows = pl.ds(u * UNIT_ROWS + g, SUBLANES, stride=UNIT_GROUPS)
            for j in range(D_SLABS):
                slab_ref[j, rows, :] = y[r0:r0 + SUBLANES, j * LANES:(j + 1) * LANES]
    y_ref[0] = jnp.concatenate([slab_ref[j] for j in range(D_SLABS)], axis=1)


def _ssd_conv(xbc, cbuf_ref, cw, cb, tile):
    sub = lax.broadcasted_iota(jnp.int32, (SUBLANES, 1), 0)
    outs = []
    prev = cbuf_ref[...]
    for u in range(tile // UNIT_ROWS):
        xb = xbc[u * UNIT_ROWS:(u + 1) * UNIT_ROWS]
        last = xb[UNIT_ROWS - WRAP_ROWS:]
        wrapped = []
        for v in range(SSD_CONV - 1):
            rs = slice(v * SUBLANES, (v + 1) * SUBLANES)
            mix = jnp.where(sub == SUBLANES - 1, prev[rs], last[rs])
            wrapped.append(pltpu.roll(mix, 1, axis=0))
        ext = jnp.concatenate(wrapped + [xb], axis=0)
        conv = cb + xb * cw[SSD_CONV - 1:SSD_CONV, :]
        for i in range(SSD_CONV - 1):
            conv = conv + ext[i * SUBLANES:i * SUBLANES + UNIT_ROWS] * cw[i:i + 1, :]
        outs.append(conv)
        prev = last
    cbuf_ref[...] = prev
    return _silu(jnp.concatenate(outs, axis=0))


def _ssd_scan_chunk(xs, bm, cm, dt, a, expand, st_ref, y_ref, row0, causal, tril, lo_half):
    ac = _cumsum_rows(tril, a)
    ac_t = ac.T
    dt_t = dt.T
    a_last = ac[-1:, :]
    wgt = jnp.exp(a_last - ac) * dt
    xw = (xs * _dot(wgt.astype(BF16), expand)).astype(BF16)
    pair = 2 * SSD_HEADDIM
    for gi in range(SSD_GROUPS):
        ns = slice(gi * SSD_STATE, (gi + 1) * SSD_STATE)
        gs = slice(gi * SSD_GW, (gi + 1) * SSD_GW)
        bg = bm[:, ns].astype(BF16)
        cg = cm[:, ns].astype(BF16)
        cb = _dot_nt(cg, bg)
        st = st_ref[gi]
        y_int = _dot(cg, st.astype(BF16))
        e_last = []
        for qi in range(SSD_HPG // 2):
            h0 = gi * SSD_HPG + 2 * qi
            ps = slice(h0 * SSD_HEADDIM, h0 * SSD_HEADDIM + pair)
            ms, eas = [], []
            for e in range(2):
                h = h0 + e
                col = jnp.broadcast_to(ac[:, h:h + 1], (CHUNK, CHUNK))
                seg = col - ac_t[h:h + 1, :]
                ms.append(jnp.where(causal, cb * jnp.exp(seg) * dt_t[h:h + 1, :], 0.0).astype(BF16))
                eas.append(jnp.exp(col))
            xp = xs[:, ps]
            rhs = jnp.concatenate([jnp.where(lo_half, xp, 0.0).astype(BF16),
                                   jnp.where(lo_half, 0.0, xp).astype(BF16)], axis=0)
            ea = jnp.where(lo_half, eas[0], eas[1])
            e_last.append(ea[CHUNK - 1:CHUNK, :])
            y_ref[row0:row0 + CHUNK, ps] = (_dot(jnp.concatenate(ms, axis=1), rhs)
                                            + y_int[:, qi * pair:(qi + 1) * pair] * ea)
        st_ref[gi] = st * jnp.concatenate(e_last, axis=1) + _dot_tn(bg, xw[:, gs])


def _ssd_tail(y, z, wn, wo, postw, x):
    y = y * _silu(z)
    parts = []
    for gi in range(SSD_GROUPS):
        gs = slice(gi * SSD_GW, (gi + 1) * SSD_GW)
        parts.append(_rms(y[:, gs], wn[:, gs]))
    yn = jnp.concatenate(parts, axis=1)
    out = _dot(yn.astype(BF16), wo)
    return x + _rms(out, postw)


def _ssd_prompt_kernel(*refs, n_pad, tile, has_prev):
    x_refs, refs = refs[:D_SLABS], refs[D_SLABS:]
    (s0_ref, c0_ref, prew_ref, postw_ref, win_ref, cw_ref, cb_ref, dtb_ref, alog_ref, dskip_ref,
     wn_ref, wo_ref, expand_ref) = refs[:13]
    y_ref, sout_ref, cout_ref, st_ref, cbuf_ref, ys_ref, slab_ref = refs[13 + (2 if has_prev else 0):]
    t = pl.program_id(1)
    first_buffered = SUBLANES - (SSD_CONV - 1)

    @pl.when(t == 0)
    def _():
        for gi in range(SSD_GROUPS):
            s0 = s0_ref[0, gi * SSD_HPG:(gi + 1) * SSD_HPG].reshape(SSD_GW, SSD_STATE)
            st_ref[gi] = s0.T
        cbuf_ref[...] = jnp.zeros(cbuf_ref.shape, F32)
        for v in range(SSD_CONV - 1):
            r = (v + 1) * SUBLANES - 1
            cbuf_ref[r:r + 1, :] = c0_ref[0, first_buffered + v:first_buffered + v + 1, :]

    x = _load_interleaved(x_refs, tile)
    hn = _rms(x, prew_ref[...])
    if n_pad:
        row = t * tile + _position_time(tile)[:, None]
        hn = jnp.where(row >= n_pad, hn, 0.0)
    hb = hn.astype(BF16)
    xbc = _dot(hb, win_ref[:, SSD_X0:SSD_DT0])
    dtr = _narrow_proj(hb, win_ref, SSD_DT0, SSD_HEADS)
    xc = _ssd_conv(xbc, cbuf_ref, cw_ref[...], cb_ref[...], tile)
    z = _dot(hb, win_ref[:, SSD_Z0:SSD_X0])
    xs = xc[:, :SSD_DINNER]
    bm = xc[:, SSD_DINNER:SSD_DINNER + SSD_GROUPS * SSD_STATE]
    cm = xc[:, SSD_DINNER + SSD_GROUPS * SSD_STATE:]
    dt = _softplus(dtr + dtb_ref[...])
    if n_pad:
        dt = jnp.where(row >= n_pad, dt, 0.0)
    a = dt * (-jnp.exp(alog_ref[...]))

    tm = _position_time(CHUNK)
    causal = tm[:, None] >= tm[None, :]
    tril = causal.astype(BF16)
    lo_half = lax.broadcasted_iota(jnp.int32, (CHUNK, 2 * SSD_HEADDIM), 1) < SSD_HEADDIM
    expand = expand_ref[...]
    for c in range(tile // CHUNK):
        rows = slice(c * CHUNK, (c + 1) * CHUNK)
        _ssd_scan_chunk(xs[rows], bm[rows], cm[rows], dt[rows], a[rows], expand,
                        st_ref, ys_ref, c * CHUNK, causal, tril, lo_half)
    y = ys_ref[...] + xs * dskip_ref[...]
    out = _ssd_tail(y, z, wn_ref[...], wo_ref[...], postw_ref[...], x)
    _store_interleaved(y_ref, slab_ref, out, tile)

    @pl.when(t == pl.num_programs(1) - 1)
    def _():
        for gi in range(SSD_GROUPS):
            sout_ref[0, gi * SSD_HPG:(gi + 1) * SSD_HPG] = (
                st_ref[gi].T.reshape(SSD_HPG, SSD_HEADDIM, SSD_STATE))
        cout_ref[0] = jnp.zeros(cout_ref.shape[1:], F32)
        for v in range(SSD_CONV - 1):
            r = (v + 1) * SUBLANES - 1
            cout_ref[0, first_buffered + v:first_buffered + v + 1, :] = cbuf_ref[r:r + 1, :]


def _ssd_weight_list(w, step):
    names = ("pre", "post", "win", "cw", "cb", "dtb", "alog", "dskip", "wn", "wo")
    consts = ("expand", "expand_n") if step else ("expand",)
    return [(w[k], True) for k in names] + [(w[k], False) for k in consts]


def _ssd_prompt_layer(x, s0, c0, prev_out, layer, w, *, n_pad, tile):
    bsz, seq, _ = x.shape
    s_tail = (SSD_HEADS, SSD_HEADDIM, SSD_STATE)
    c_tail = (SUBLANES, SSD_CONV_DIM)
    own = s0.shape[1] == bsz
    weights = _ssd_weight_list(w, step=False)
    carried = [] if prev_out is None else list(prev_out)
    n_in = D_SLABS + 2 + len(weights)
    x_slabs = [pl.BlockSpec((1, tile, LANES), lambda b, t, j=j: (b, t, j)) for j in range(D_SLABS)]
    return pl.pallas_call(
        functools.partial(_ssd_prompt_kernel, n_pad=n_pad, tile=tile, has_prev=bool(carried)),
        grid=(bsz, seq // tile),
        in_specs=x_slabs
                 + [_batch_block(layer, 1, s_tail, own), _batch_block(layer, 1, c_tail, own)]
                 + _weight_specs(weights, layer)
                 + [pl.BlockSpec(memory_space=pl.ANY)] * len(carried),
        out_specs=[pl.BlockSpec((1, tile, D_MODEL), lambda b, t: (b, t, 0)),
                   _batch_block(layer, 1, s_tail), _batch_block(layer, 1, c_tail)],
        out_shape=[jax.ShapeDtypeStruct(x.shape, F32),
                   jax.ShapeDtypeStruct((N_LAYERS_PER_MIXER, bsz) + s_tail, F32),
                   jax.ShapeDtypeStruct((N_LAYERS_PER_MIXER, bsz) + c_tail, F32)],
        scratch_shapes=[pltpu.VMEM((SSD_GROUPS, SSD_STATE, SSD_GW), F32),
                        pltpu.VMEM((WRAP_ROWS, SSD_CONV_DIM), F32),
                        pltpu.VMEM((tile, SSD_DINNER), F32),
                        pltpu.VMEM((D_SLABS, tile, LANES), F32)],
        input_output_aliases={n_in: 1, n_in + 1: 2} if carried else {},
        compiler_params=pltpu.CompilerParams(
            dimension_semantics=("arbitrary", "arbitrary"), vmem_limit_bytes=VMEM_LIMIT),
        name="ssd_prompt",
    )(*([x] * D_SLABS), s0, c0, *[a for a, _ in weights], *carried)


def _column(row, width, lane0=0):
    seg = row[:, lane0:lane0 + width]
    r = lax.broadcasted_iota(jnp.int32, (width, width), 0)
    c = lax.broadcasted_iota(jnp.int32, (width, width), 1)
    return jnp.sum(jnp.where(r == c, seg, 0.0), axis=1, keepdims=True)


def _gla_step_kernel(x_ref, s_ref, prew_ref, postw_ref, win_ref, wgu_ref, bg_ref, wn_ref, wo_ref,
                     *rest, bb, has_prev):
    y_ref, sout_ref, q_s, kl_s, v_s, e_s, o_s = rest[1:] if has_prev else rest
    i = pl.program_id(0)
    bsz = x_ref.shape[0]

    @pl.when(i == 0)
    def _():
        hb = _rms(x_ref[...], prew_ref[...]).astype(BF16)
        q_s[...] = _dot(hb, win_ref[:, GLA_Q0:GLA_K0]) * (GLA_HK ** -0.5)
        o_s[:, 0:GLA_DK] = _dot(hb, win_ref[:, GLA_K0:GLA_V0])
        k_t = o_s[:, 0:GLA_DK].T
        for h in range(GLA_HEADS):
            kl_s[h // 2, :, (h % 2) * bsz:(h % 2 + 1) * bsz] = (
                k_t[h * GLA_HK:(h + 1) * GLA_HK].astype(BF16))
        v_s[...] = _dot(hb, win_ref[:, GLA_V0:GLA_R0])
        e_s[...] = jnp.exp(_gla_gates(hb, win_ref, wgu_ref, bg_ref))

    sub = lax.broadcasted_iota(jnp.int32, (bsz, GLA_HV), 0)
    zero_bv = jnp.zeros((bsz, GLA_HV), BF16)

    for j in range(bb):
        row = i * bb + j
        qr = q_s[pl.ds(row, 1), :]
        er = e_s[pl.ds(row, 1), :]
        for p in range(GLA_HEADS // 2):
            vsel = [jnp.where(sub == row, v_s[:, (2 * p + e) * GLA_HV:(2 * p + e + 1) * GLA_HV],
                              0.0).astype(BF16) for e in range(2)]
            rhs = jnp.concatenate([jnp.concatenate([vsel[0], zero_bv], axis=1),
                                   jnp.concatenate([zero_bv, vsel[1]], axis=1)], axis=0)
            kv = _dot(kl_s[p], rhs)
            for e in range(2):
                h = 2 * p + e
                vs = slice(h * GLA_HV, (h + 1) * GLA_HV)
                ecol = _column(er, GLA_HK, h * GLA_HK)
                qcol = _column(qr, GLA_HK, h * GLA_HK)
                s_new = s_ref[j, h] * ecol + kv[:, e * GLA_HV:(e + 1) * GLA_HV]
                sout_ref[j, h] = s_new
                o_s[pl.ds(row, 1), vs] = jnp.sum(qcol * s_new, axis=0, keepdims=True)

    @pl.when(i == pl.num_programs(0) - 1)
    def _():
        x = x_ref[...]
        hb = _rms(x, prew_ref[...]).astype(BF16)
        r = _dot(hb, win_ref[:, GLA_R0:GLA_G0])
        y_ref[...] = _gla_tail(o_s[...], r, wn_ref[...], wo_ref[...], postw_ref[...], x)


def _gla_step_layer(x, states, prev_out, layer, w, *, bb=8):
    bsz = x.shape[0]
    weights = _gla_weight_list(w)
    sblk = _batch_block(layer, bb, (GLA_HEADS, GLA_HK, GLA_HV))
    carried = [] if prev_out is None else [prev_out]
    return pl.pallas_call(
        functools.partial(_gla_step_kernel, bb=bb, has_prev=bool(carried)),
        grid=(bsz // bb,),
        in_specs=[_full(x.shape), sblk] + _weight_specs(weights, layer)
                 + [pl.BlockSpec(memory_space=pl.ANY)] * len(carried),
        out_specs=[pl.BlockSpec(x.shape, lambda i: (0, 0)), sblk],
        out_shape=[jax.ShapeDtypeStruct(x.shape, F32), jax.ShapeDtypeStruct(states.shape, F32)],
        scratch_shapes=[pltpu.VMEM((bsz, GLA_DK), F32),
                        pltpu.VMEM((GLA_HEADS // 2, GLA_HK, 2 * bsz), BF16),
                        pltpu.VMEM((bsz, GLA_DV), F32),
                        pltpu.VMEM((bsz, GLA_DK), F32),
                        pltpu.VMEM((bsz, GLA_DV), F32)],
        input_output_aliases={2 + len(weights): 1} if carried else {},
        compiler_params=pltpu.CompilerParams(
            dimension_semantics=("arbitrary",), vmem_limit_bytes=VMEM_LIMIT),
        name="gla_step",
    )(x, states, *[a for a, _ in weights], *carried)


def _ssd_step_kernel(x_ref, s_ref, cv_ref, prew_ref, postw_ref, win_ref, cw_ref, cb_ref, dtb_ref,
                     alog_ref, dskip_ref, wn_ref, wo_ref, expand_ref, expand_n_ref, *rest,
                     bb, has_prev):
    (y_ref, sout_ref, cvout_ref, xs_s, xl_s, ct_s, b_s, e_s, yt_s) = rest[2:] if has_prev else rest
    i = pl.program_id(0)
    bsz = x_ref.shape[0]
    cd = SSD_CONV_DIM

    @pl.when(i == 0)
    def _():
        hb = _rms(x_ref[...], prew_ref[...]).astype(BF16)
        xbc = _dot(hb, win_ref[:, SSD_X0:SSD_DT0])
        dtr = _narrow_proj(hb, win_ref, SSD_DT0, SSD_HEADS)
        cw = cw_ref[...]
        conv = cb_ref[...] + xbc * cw[3:4, :]
        for t in range(SSD_CONV - 1):
            conv = conv + cv_ref[:, t * cd:(t + 1) * cd] * cw[t:t + 1, :]
        cvout_ref[:, 0:cd] = cv_ref[:, cd:2 * cd]
        cvout_ref[:, cd:2 * cd] = cv_ref[:, 2 * cd:3 * cd]
        cvout_ref[:, 2 * cd:3 * cd] = xbc
        xc = _silu(conv)
        xs = xc[:, :SSD_DINNER]
        dt = _softplus(dtr + dtb_ref[...])
        ea = jnp.exp(dt * (-jnp.exp(alog_ref[...])))
        xs_s[...] = xs
        xd_t = (xs * _dot_exact_rhs(dt, expand_ref[...])).T
        c_t = xc[:, SSD_DINNER + SSD_GROUPS * SSD_STATE:].T
        for p in range(SSD_GROUPS // 2):
            for e in range(2):
                gi = 2 * p + e
                xl_s[p, :, e * bsz:(e + 1) * bsz] = (
                    xd_t[gi * SSD_GW:(gi + 1) * SSD_GW].astype(BF16))
        ct_s[...] = c_t
        b_s[...] = xc[:, SSD_DINNER:SSD_DINNER + SSD_GROUPS * SSD_STATE]
        e_s[...] = _dot_exact_rhs(ea, expand_n_ref[...])
        yt_s[...] = jnp.zeros(yt_s.shape, F32)

    sub = lax.broadcasted_iota(jnp.int32, (bsz, SSD_STATE), 0)
    lane = lax.broadcasted_iota(jnp.int32, (SSD_STATE, bsz), 1)
    zero_bn = jnp.zeros((bsz, SSD_STATE), BF16)
    zero_nb = jnp.zeros((SSD_STATE, bsz), BF16)

    for p in range(SSD_GROUPS // 2):
        y_acc = None
        for j in range(bb):
            row = i * bb + j
            er = e_s[pl.ds(row, 1), :]
            bsel, csel = [], []
            for e in range(2):
                ns = slice((2 * p + e) * SSD_STATE, (2 * p + e + 1) * SSD_STATE)
                bsel.append(jnp.where(sub == row, b_s[:, ns], 0.0).astype(BF16))
                csel.append(jnp.where(lane == row, ct_s[ns, :], 0.0).astype(BF16))
            rhs_b = jnp.concatenate([jnp.concatenate([bsel[0], zero_bn], axis=1),
                                     jnp.concatenate([zero_bn, bsel[1]], axis=1)], axis=0)
            rhs_c = jnp.concatenate([jnp.concatenate([csel[0], zero_nb], axis=1),
                                     jnp.concatenate([zero_nb, csel[1]], axis=1)], axis=0)
            outer = _dot(xl_s[p], rhs_b)
            halves = []
            for e in range(2):
                pieces = []
                for r in range(SSD_HPG):
                    h = (2 * p + e) * SSD_HPG + r
                    s_new = (s_ref[j, h] * er[:, h * SSD_STATE:(h + 1) * SSD_STATE]
                             + outer[r * SSD_HEADDIM:(r + 1) * SSD_HEADDIM,
                                     e * SSD_STATE:(e + 1) * SSD_STATE])
                    sout_ref[j, h] = s_new
                    pieces.append(s_new.astype(BF16))
                halves.append(jnp.concatenate(pieces, axis=0))
            y_j = _dot(jnp.concatenate(halves, axis=1), rhs_c)
            y_acc = y_j if y_acc is None else y_acc + y_j
        yt_s[p] += y_acc

    @pl.when(i == pl.num_programs(0) - 1)
    def _():
        x = x_ref[...]
        hb = _rms(x, prew_ref[...]).astype(BF16)
        z = _dot(hb, win_ref[:, SSD_Z0:SSD_X0])
        y = jnp.concatenate([yt_s[p, :, e * bsz:(e + 1) * bsz].T
                             for p in range(SSD_GROUPS // 2) for e in range(2)], axis=1)
        y = y + xs_s[...] * dskip_ref[...]
        y_ref[...] = _ssd_tail(y, z, wn_ref[...], wo_ref[...], postw_ref[...], x)


def _ssd_step_layer(x, states, convs, prev_out, layer, w, *, bb=4):
    bsz = x.shape[0]
    weights = _ssd_weight_list(w, step=True)
    sblk = _batch_block(layer, bb, (SSD_HEADS, SSD_HEADDIM, SSD_STATE))
    cblk = pl.BlockSpec((None,) + convs.shape[1:], lambda i: (layer, 0, 0))
    carried = [] if prev_out is None else list(prev_out)
    n_in = 3 + len(weights)
    return pl.pallas_call(
        functools.partial(_ssd_step_kernel, bb=bb, has_prev=bool(carried)),
        grid=(bsz // bb,),
        in_specs=[_full(x.shape), sblk,
                  pl.BlockSpec(cblk.block_shape, cblk.index_map, pipeline_mode=pl.Buffered(1))]
                 + _weight_specs(weights, layer)
                 + [pl.BlockSpec(memory_space=pl.ANY)] * len(carried),
        out_specs=[pl.BlockSpec(x.shape, lambda i: (0, 0)), sblk, cblk],
        out_shape=[jax.ShapeDtypeStruct(x.shape, F32), jax.ShapeDtypeStruct(states.shape, F32),
                   jax.ShapeDtypeStruct(convs.shape, F32)],
        input_output_aliases={n_in: 1, n_in + 1: 2} if carried else {},
        scratch_shapes=[pltpu.VMEM((bsz, SSD_DINNER), F32),
                        pltpu.VMEM((SSD_GROUPS // 2, SSD_GW, 2 * bsz), BF16),
                        pltpu.VMEM((SSD_GROUPS * SSD_STATE, bsz), F32),
                        pltpu.VMEM((bsz, SSD_GROUPS * SSD_STATE), F32),
                        pltpu.VMEM((bsz, SSD_HEADS * SSD_STATE), F32),
                        pltpu.VMEM((SSD_GROUPS // 2, SSD_GW, 2 * bsz), F32)],
        compiler_params=pltpu.CompilerParams(
            dimension_semantics=("arbitrary",), vmem_limit_bytes=VMEM_LIMIT),
        name="ssd_step",
    )(x, states, convs, *[a for a, _ in weights], *carried)


def _pad_last(a, n):
    return jnp.pad(a, [(0, 0)] * (a.ndim - 1) + [(0, n - a.shape[-1])])


def _rows(a):
    return a[:, None, :]


def _gla_weights(pre, post, w_in, w_gate_up, b_gate, w_norm, w_out):
    return {
        "pre": _rows(pre), "post": _rows(post),
        "win": w_in.astype(BF16),
        "wgu": jnp.pad(w_gate_up, ((0, 0), (0, LANES - GLA_RANK), (0, 0))).astype(BF16),
        "bg": _rows(b_gate), "wn": _rows(w_norm), "wo": w_out.astype(BF16),
    }


def _ssd_weights(pre, post, w_in, conv_w, conv_b, dt_bias, a_log, d_skip, w_norm, w_out):
    head_of_lane = jnp.arange(SSD_DINNER) // SSD_HEADDIM
    head_of_lane_n = jnp.arange(SSD_HEADS * SSD_STATE) // SSD_STATE
    heads = jnp.arange(LANES)[:, None]
    return {
        "pre": _rows(pre), "post": _rows(post),
        "win": w_in.astype(BF16),
        "cw": jnp.pad(conv_w, ((0, 0), (0, SUBLANES - SSD_CONV), (0, 0))), "cb": _rows(conv_b),
        "dtb": _rows(_pad_last(dt_bias, LANES)), "alog": _rows(_pad_last(a_log, LANES)),
        "dskip": _rows(jnp.repeat(d_skip, SSD_HEADDIM, axis=1)),
        "wn": _rows(w_norm), "wo": w_out.astype(BF16),
        "expand": (heads == head_of_lane[None, :]).astype(BF16),
        "expand_n": (heads == head_of_lane_n[None, :]).astype(BF16),
    }


def _prompt_trunk(x, gla0, ssm0, conv0, gw, sw, *, n_pad, gla_tile, ssd_tile):
    gla_s = ssm_conv_s = None
    for j in range(N_LAYERS_PER_MIXER):
        x, gla_s = _gla_prompt_layer(x, gla0, gla_s, j, gw, n_pad=n_pad, tile=gla_tile)
        x, *ssm_conv_s = _ssd_prompt_layer(x, ssm0, conv0, ssm_conv_s, j, sw,
                                           n_pad=n_pad, tile=ssd_tile)
    return x, gla_s, ssm_conv_s[0], ssm_conv_s[1]


def kernel(x_prompt, x_sample, state_gla, state_ssm, state_conv, meta_tokens, pre_norm, post_norm,
           gla_w_in, gla_w_gate_up, gla_b_gate, gla_w_norm, gla_w_out,
           ssd_w_in, ssd_conv_w, ssd_conv_b, ssd_dt_bias, ssd_a_log, ssd_d_skip, ssd_w_norm, ssd_w_out):
    gw = _gla_weights(pre_norm[0::2], post_norm[0::2], gla_w_in, gla_w_gate_up, gla_b_gate,
                      gla_w_norm, gla_w_out)
    sw = _ssd_weights(pre_norm[1::2], post_norm[1::2], ssd_w_in, ssd_conv_w, ssd_conv_b,
                      ssd_dt_bias, ssd_a_log, ssd_d_skip, ssd_w_norm, ssd_w_out)

    n_pad = CHUNK - N_META
    x_meta = jnp.pad(meta_tokens.astype(F32), ((n_pad, 0), (0, 0)))[None]
    zg = jnp.zeros((N_LAYERS_PER_MIXER, 1, GLA_HEADS, GLA_HK, GLA_HV), F32)
    zs = jnp.zeros((N_LAYERS_PER_MIXER, 1, SSD_HEADS, SSD_HEADDIM, SSD_STATE), F32)
    zc = jnp.zeros((N_LAYERS_PER_MIXER, 1, SUBLANES, SSD_CONV_DIM), F32)
    _, mg, ms, mc = _prompt_trunk(x_meta, zg, zs, zc, gw, sw,
                                  n_pad=n_pad, gla_tile=CHUNK, ssd_tile=CHUNK)
    y_prompt, gla_p, ssm_p, conv_p = _prompt_trunk(x_prompt, mg, ms, mc, gw, sw, n_pad=0,
                                                   gla_tile=GLA_TILE, ssd_tile=SSD_TILE)
    conv_p = conv_p[:, :, SUBLANES - (SSD_CONV - 1):, :]

    xs = x_sample[:, 0, :]
    sbsz = xs.shape[0]
    convs = state_conv.reshape(N_LAYERS_PER_MIXER, sbsz, (SSD_CONV - 1) * SSD_CONV_DIM)
    gla_s = ssm_conv_s = None
    for j in range(N_LAYERS_PER_MIXER):
        xs, gla_s = _gla_step_layer(xs, state_gla, gla_s, j, gw)
        xs, *ssm_conv_s = _ssd_step_layer(xs, state_ssm, convs, ssm_conv_s, j, sw)
    ssm_s, conv_s = ssm_conv_s
    y_sample = xs[:, None, :]
    return (y_prompt, y_sample, gla_p, ssm_p, conv_p, gla_s, ssm_s,
            conv_s.reshape(N_LAYERS_PER_MIXER, sbsz, SSD_CONV - 1, SSD_CONV_DIM))
```

```python
import functools

import jax
import jax.numpy as jnp
from jax import lax
from jax.experimental import pallas as pl
from jax.experimental.pallas import tpu as pltpu

F32 = jnp.float32
BF16 = jnp.bfloat16

D_MODEL = 1024
N_META = 16
NORM_EPS = 1e-6
N_LAYERS_PER_MIXER = 2

GLA_HEADS = 4
GLA_DK = 512
GLA_DV = 1024
GLA_HK = 128
GLA_HV = 256
GLA_RANK = 16
GLA_TAU = 16.0
GLA_SAFE_DROP = 40.0

SSD_DINNER = 2048
SSD_HEADDIM = 64
SSD_HEADS = 32
SSD_GROUPS = 4
SSD_HPG = 8
SSD_STATE = 128
SSD_CONV = 4
SSD_CONV_DIM = 3072
SSD_GW = SSD_HPG * SSD_HEADDIM

LANES = 128
SUBLANES = 8
CHUNK = 128
GLA_TILE = 1024
SSD_TILE = 512
VMEM_LIMIT = 56 * 1024 * 1024

GLA_Q0, GLA_K0, GLA_V0, GLA_R0, GLA_G0 = 0, GLA_DK, 2 * GLA_DK, 2 * GLA_DK + GLA_DV, 2 * GLA_DK + 2 * GLA_DV
SSD_Z0, SSD_X0, SSD_DT0 = 0, SSD_DINNER, SSD_DINNER + SSD_CONV_DIM


def _narrow_proj(hb, w_ref, start, width):
    res = jnp.dot(hb, w_ref[:, start:start + width], preferred_element_type=F32)
    return jnp.concatenate([res, jnp.zeros((res.shape[0], LANES - width), F32)], axis=1)


def _dot(a, b):
    return jnp.dot(a, b, preferred_element_type=F32)


def _dot_nt(a, b):
    return lax.dot_general(a, b, (((1,), (1,)), ((), ())), preferred_element_type=F32)


def _dot_tn(a, b):
    return lax.dot_general(a, b, (((0,), (0,)), ((), ())), preferred_element_type=F32)


def _rms(x, w):
    return x * lax.rsqrt(jnp.mean(x * x, axis=-1, keepdims=True) + NORM_EPS) * w


def _silu(x):
    h = 0.5 * x
    return h + h * jnp.tanh(h)


def _softplus(x):
    return jnp.maximum(x, 0.0) + jnp.log1p(jnp.exp(-jnp.abs(x)))


def _log_sigmoid(x):
    return -_softplus(-x)


def _tri(n):
    r = lax.broadcasted_iota(jnp.int32, (n, n), 0)
    c = lax.broadcasted_iota(jnp.int32, (n, n), 1)
    return r >= c


def _split3(x):
    hi = x.astype(BF16)
    r1 = x - hi.astype(F32)
    mid = r1.astype(BF16)
    lo = (r1 - mid.astype(F32)).astype(BF16)
    return hi, mid, lo


def _dot_exact_rhs(x, m_bf16):
    hi, mid, lo = _split3(x)
    return _dot(hi, m_bf16) + _dot(mid, m_bf16) + _dot(lo, m_bf16)


def _cumsum_rows(tril_bf16, x):
    hi, mid, lo = _split3(x)
    if x.shape[1] < 2 * LANES:
        return _dot(tril_bf16, hi) + _dot(tril_bf16, mid) + _dot(tril_bf16, lo)
    two = jnp.concatenate([tril_bf16, tril_bf16], axis=1)
    return _dot(two, jnp.concatenate([hi, mid], axis=0)) + _dot(tril_bf16, lo)


def _full(shape):
    nd = len(shape)
    return pl.BlockSpec(shape, lambda *_: (0,) * nd, pipeline_mode=pl.Buffered(1))


def _of_layer(a, layer):
    nz = a.ndim - 1
    return pl.BlockSpec((None,) + a.shape[1:], lambda *_: (layer,) + (0,) * nz,
                        pipeline_mode=pl.Buffered(1))


def _weight_specs(weights, layer):
    return [_of_layer(a, layer) if stacked else _full(a.shape) for a, stacked in weights]


def _batch_block(layer, bb, tail, own=True):
    nz = len(tail)
    return pl.BlockSpec((None, bb) + tail,
                        lambda i, *_: (layer, i if own else 0) + (0,) * nz)


def _gla_attn_pairwise(qf, kf, bc, causal, kbuf_ref, bbuf_ref):
    n = qf.shape[0]
    kbuf_ref[...] = kf
    bbuf_ref[...] = bc
    lane = lax.broadcasted_iota(jnp.int32, (n, n), 1)

    def body(j, acc):
        kj = kbuf_ref[pl.ds(j, 1), :]
        bj = bbuf_ref[pl.ds(j, 1), :]
        col = jnp.sum(qf * jnp.exp(jnp.minimum(bc - bj, 0.0)) * kj, axis=1, keepdims=True)
        return jnp.where(lane == j, col, acc)

    acc = lax.fori_loop(0, n, body, jnp.zeros((n, n), F32))
    return jnp.where(causal, acc, 0.0)


def _gla_scan_tile(q, k, v, g, states, pairwise, kbuf_ref, bbuf_ref):
    causal = _tri(CHUNK)
    tril = causal.astype(BF16)
    states = list(states)
    outs = []
    for c in range(q.shape[0] // CHUNK):
        rows = slice(c * CHUNK, (c + 1) * CHUNK)
        bc = _cumsum_rows(tril, g[rows])
        b_last = bc[-1:, :]
        e_last = jnp.exp(b_last)
        qh = (q[rows] * jnp.exp(bc)).astype(BF16)
        if pairwise:
            kd = (k[rows] * jnp.exp(b_last - bc)).astype(BF16)
        else:
            kh = k[rows] * jnp.exp(-bc)
            kd = (kh * e_last).astype(BF16)
            kh = kh.astype(BF16)
        vb = v[rows].astype(BF16)
        zk = jnp.zeros((CHUNK, GLA_HK), BF16)
        parts = []
        for h0 in range(0, GLA_HEADS, 2):
            ks = [slice((h0 + e) * GLA_HK, (h0 + e + 1) * GLA_HK) for e in range(2)]
            vs = [slice((h0 + e) * GLA_HV, (h0 + e + 1) * GLA_HV) for e in range(2)]
            ks2 = slice(ks[0].start, ks[1].stop)
            if pairwise:
                attn = [_gla_attn_pairwise(q[rows, ks[e]], k[rows, ks[e]], bc[:, ks[e]], causal,
                                           kbuf_ref, bbuf_ref) for e in range(2)]
            else:
                k_bd = jnp.concatenate([jnp.concatenate([kh[:, ks[0]], zk], axis=1),
                                        jnp.concatenate([zk, kh[:, ks[1]]], axis=1)], axis=0)
                scores = _dot_nt(qh[:, ks2], k_bd)
                attn = [jnp.where(causal, scores[:, e * CHUNK:(e + 1) * CHUNK], 0.0)
                        for e in range(2)]
            kd_bd = jnp.concatenate([jnp.concatenate([kd[:, ks[0]], zk], axis=1),
                                     jnp.concatenate([zk, kd[:, ks[1]]], axis=1)], axis=0)
            upd = _dot_tn(jnp.concatenate([vb[:, vs[0]], vb[:, vs[1]]], axis=0), kd_bd)
            for e in range(2):
                st = states[h0 + e]
                parts.append(_dot(attn[e].astype(BF16), vb[:, vs[e]])
                             + _dot_nt(qh[:, ks[e]], st.astype(BF16)))
                states[h0 + e] = st * e_last[:, ks[e]] + upd[:, e * GLA_HK:(e + 1) * GLA_HK]
        outs.append(jnp.concatenate(parts, axis=1))
    o = jnp.concatenate(outs, axis=0) if len(outs) > 1 else outs[0]
    return o, states


def _gla_tail(o, r, wn, wo, postw, x):
    parts = []
    for h in range(GLA_HEADS):
        vs = slice(h * GLA_HV, (h + 1) * GLA_HV)
        parts.append(_rms(o[:, vs], wn[:, vs]))
    on = jnp.concatenate(parts, axis=1) * _silu(r)
    y = _dot(on.astype(BF16), wo)
    return x + _rms(y, postw)


def _gla_gates(hb, win_ref, wgu_ref, bg_ref):
    gl = _narrow_proj(hb, win_ref, GLA_G0, GLA_RANK)
    return _log_sigmoid(_dot(gl.astype(BF16), wgu_ref[...]) + bg_ref[...]) * (1.0 / GLA_TAU)


def _gla_prompt_kernel(x_ref, s0_ref, prew_ref, postw_ref, win_ref, wgu_ref, bg_ref, wn_ref,
                       wo_ref, *rest, n_pad, tile, has_prev):
    y_ref, sout_ref, st_ref, o_ref, kbuf_ref, bbuf_ref = rest[1:] if has_prev else rest
    t = pl.program_id(1)

    @pl.when(t == 0)
    def _():
        for h in range(GLA_HEADS):
            st_ref[h] = s0_ref[0, h].T

    x = x_ref[0]
    hn = _rms(x, prew_ref[...])
    if n_pad:
        row = lax.broadcasted_iota(jnp.int32, (tile, 1), 0) + t * tile
        hn = jnp.where(row >= n_pad, hn, 0.0)
    hb = hn.astype(BF16)
    q = _dot(hb, win_ref[:, GLA_Q0:GLA_K0]) * (GLA_HK ** -0.5)
    k = _dot(hb, win_ref[:, GLA_K0:GLA_V0])
    v = _dot(hb, win_ref[:, GLA_V0:GLA_R0])
    g = _gla_gates(hb, win_ref, wgu_ref, bg_ref)

    drop = None
    for c in range(tile // CHUNK):
        tot = jnp.sum(g[c * CHUNK:(c + 1) * CHUNK], axis=0, keepdims=True)
        drop = tot if drop is None else jnp.minimum(drop, tot)
    safe = jnp.min(drop) >= -GLA_SAFE_DROP

    def scan(pairwise):
        o, states = _gla_scan_tile(q, k, v, g, [st_ref[h] for h in range(GLA_HEADS)], pairwise,
                                   kbuf_ref, bbuf_ref)
        o_ref[...] = o
        for h in range(GLA_HEADS):
            st_ref[h] = states[h]

    pl.when(safe)(functools.partial(scan, False))
    pl.when(jnp.logical_not(safe))(functools.partial(scan, True))
    r = _dot(hb, win_ref[:, GLA_R0:GLA_G0])
    y_ref[0] = _gla_tail(o_ref[...], r, wn_ref[...], wo_ref[...], postw_ref[...], x)

    @pl.when(t == pl.num_programs(1) - 1)
    def _():
        for h in range(GLA_HEADS):
            sout_ref[0, h] = st_ref[h].T


def _gla_weight_list(w):
    return [(w[k], True) for k in ("pre", "post", "win", "wgu", "bg", "wn", "wo")]


def _gla_prompt_layer(x, s0, prev_out, layer, w, *, n_pad, tile):
    bsz, seq, _ = x.shape
    tail = (GLA_HEADS, GLA_HK, GLA_HV)
    weights = _gla_weight_list(w)
    carried = [] if prev_out is None else [prev_out]
    return pl.pallas_call(
        functools.partial(_gla_prompt_kernel, n_pad=n_pad, tile=tile, has_prev=bool(carried)),
        grid=(bsz, seq // tile),
        in_specs=[pl.BlockSpec((1, tile, D_MODEL), lambda b, t: (b, t, 0)),
                  _batch_block(layer, 1, tail, own=s0.shape[1] == bsz)]
                 + _weight_specs(weights, layer)
                 + [pl.BlockSpec(memory_space=pl.ANY)] * len(carried),
        out_specs=[pl.BlockSpec((1, tile, D_MODEL), lambda b, t: (b, t, 0)),
                   _batch_block(layer, 1, tail)],
        out_shape=[jax.ShapeDtypeStruct(x.shape, F32),
                   jax.ShapeDtypeStruct((N_LAYERS_PER_MIXER, bsz) + tail, F32)],
        scratch_shapes=[pltpu.VMEM((GLA_HEADS, GLA_HV, GLA_HK), F32),
                        pltpu.VMEM((tile, GLA_DV), F32),
                        pltpu.VMEM((CHUNK, GLA_HK), F32),
                        pltpu.VMEM((CHUNK, GLA_HK), F32)],
        input_output_aliases={2 + len(weights): 1} if carried else {},
        compiler_params=pltpu.CompilerParams(
            dimension_semantics=("arbitrary", "arbitrary"), vmem_limit_bytes=VMEM_LIMIT),
        name="gla_prompt",
    )(x, s0, *[a for a, _ in weights], *carried)


UNIT_GROUPS = 4
UNIT_ROWS = UNIT_GROUPS * SUBLANES
WRAP_ROWS = (SSD_CONV - 1) * SUBLANES
D_SLABS = D_MODEL // LANES
assert SSD_CONV - 1 < UNIT_GROUPS and CHUNK % UNIT_ROWS == 0


def _position_time(n):
    p = lax.iota(jnp.int32, n)
    q = p % UNIT_ROWS
    return (p - q) + (q % SUBLANES) * UNIT_GROUPS + q // SUBLANES


def _load_interleaved(slab_refs, tile):
    groups = []
    for u in range(tile // UNIT_ROWS):
        for g in range(UNIT_GROUPS):
            rows = pl.ds(u * UNIT_ROWS + g, SUBLANES, stride=UNIT_GROUPS)
            groups.append(jnp.concatenate([r[0, rows, :] for r in slab_refs], axis=1))
    return jnp.concatenate(groups, axis=0)


def _store_interleaved(y_ref, slab_ref, y, tile):
    for u in range(tile // UNIT_ROWS):
        for g in range(UNIT_GROUPS):
            r0 = u * UNIT_ROWS + g * SUBLANES
            rows = pl.ds(u * UNIT_ROWS + g, SUBLANES, stride=UNIT_GROUPS)
            for j in range(D_SLABS):
                slab_ref[j, rows, :] = y[r0:r0 + SUBLANES, j * LANES:(j + 1) * LANES]
    y_ref[0] = jnp.concatenate([slab_ref[j] for j in range(D_SLABS)], axis=1)


def _ssd_conv(xbc, cbuf_ref, cw, cb, tile):
    sub = lax.broadcasted_iota(jnp.int32, (SUBLANES, 1), 0)
    outs = []
    prev = cbuf_ref[...]
    for u in range(tile // UNIT_ROWS):
        xb = xbc[u * UNIT_ROWS:(u + 1) * UNIT_ROWS]
        last = xb[UNIT_ROWS - WRAP_ROWS:]
        wrapped = []
        for v in range(SSD_CONV - 1):
            rs = slice(v * SUBLANES, (v + 1) * SUBLANES)
            mix = jnp.where(sub == SUBLANES - 1, prev[rs], last[rs])
            wrapped.append(pltpu.roll(mix, 1, axis=0))
        ext = jnp.concatenate(wrapped + [xb], axis=0)
        conv = cb + xb * cw[SSD_CONV - 1:SSD_CONV, :]
        for i in range(SSD_CONV - 1):
            conv = conv + ext[i * SUBLANES:i * SUBLANES + UNIT_ROWS] * cw[i:i + 1, :]
        outs.append(conv)
        prev = last
    cbuf_ref[...] = prev
    return _silu(jnp.concatenate(outs, axis=0))


def _ssd_scan_chunk(xs, bm, cm, dt, a, expand, st_ref, y_ref, row0, causal, tril, lo_half):
    ac = _cumsum_rows(tril, a)
    ac_t = ac.T
    dt_t = dt.T
    a_last = ac[-1:, :]
    wgt = jnp.exp(a_last - ac) * dt
    xw = (xs * _dot(wgt.astype(BF16), expand)).astype(BF16)
    pair = 2 * SSD_HEADDIM
    zn = jnp.zeros((CHUNK, SSD_STATE), BF16)
    cbs = []
    for gp in range(0, SSD_GROUPS, 2):
        n2 = slice(gp * SSD_STATE, (gp + 2) * SSD_STATE)
        b0 = bm[:, gp * SSD_STATE:(gp + 1) * SSD_STATE].astype(BF16)
        b1 = bm[:, (gp + 1) * SSD_STATE:(gp + 2) * SSD_STATE].astype(BF16)
        b_bd = jnp.concatenate([jnp.concatenate([b0, zn], axis=1),
                                jnp.concatenate([zn, b1], axis=1)], axis=0)
        both = _dot_nt(cm[:, n2].astype(BF16), b_bd)
        cbs += [both[:, :CHUNK], both[:, CHUNK:]]
    for gi in range(SSD_GROUPS):
        ns = slice(gi * SSD_STATE, (gi + 1) * SSD_STATE)
        gs = slice(gi * SSD_GW, (gi + 1) * SSD_GW)
        bg = bm[:, ns].astype(BF16)
        cg = cm[:, ns].astype(BF16)
        cb = cbs[gi]
        st = st_ref[gi]
        y_int = _dot(cg, st.astype(BF16))
        e_last = []
        for qi in range(SSD_HPG // 2):
            h0 = gi * SSD_HPG + 2 * qi
            ps = slice(h0 * SSD_HEADDIM, h0 * SSD_HEADDIM + pair)
            ms, eas = [], []
            for e in range(2):
                h = h0 + e
                col = jnp.broadcast_to(ac[:, h:h + 1], (CHUNK, CHUNK))
                seg = col - ac_t[h:h + 1, :]
                ms.append(jnp.where(causal, cb * jnp.exp(seg) * dt_t[h:h + 1, :], 0.0).astype(BF16))
                eas.append(jnp.exp(col))
            xp = xs[:, ps]
            rhs = jnp.concatenate([jnp.where(lo_half, xp, 0.0).astype(BF16),
                                   jnp.where(lo_half, 0.0, xp).astype(BF16)], axis=0)
            ea = jnp.where(lo_half, eas[0], eas[1])
            e_last.append(ea[CHUNK - 1:CHUNK, :])
            y_ref[row0:row0 + CHUNK, ps] = (_dot(jnp.concatenate(ms, axis=1), rhs)
                                            + y_int[:, qi * pair:(qi + 1) * pair] * ea)
        st_ref[gi] = st * jnp.concatenate(e_last, axis=1) + _dot_tn(bg, xw[:, gs])


def _ssd_tail(y, z, wn, wo, postw, x):
    y = y * _silu(z)
    parts = []
    for gi in range(SSD_GROUPS):
        gs = slice(gi * SSD_GW, (gi + 1) * SSD_GW)
        parts.append(_rms(y[:, gs], wn[:, gs]))
    yn = jnp.concatenate(parts, axis=1)
    out = _dot(yn.astype(BF16), wo)
    return x + _rms(out, postw)


def _ssd_prompt_kernel(*refs, n_pad, tile, has_prev):
    x_refs, refs = refs[:D_SLABS], refs[D_SLABS:]
    (s0_ref, c0_ref, prew_ref, postw_ref, win_ref, cw_ref, cb_ref, dtb_ref, alog_ref, dskip_ref,
     wn_ref, wo_ref, expand_ref) = refs[:13]
    y_ref, sout_ref, cout_ref, st_ref, cbuf_ref, ys_ref, slab_ref = refs[13 + (2 if has_prev else 0):]
    t = pl.program_id(1)
    first_buffered = SUBLANES - (SSD_CONV - 1)

    @pl.when(t == 0)
    def _():
        for gi in range(SSD_GROUPS):
            s0 = s0_ref[0, gi * SSD_HPG:(gi + 1) * SSD_HPG].reshape(SSD_GW, SSD_STATE)
            st_ref[gi] = s0.T
        cbuf_ref[...] = jnp.zeros(cbuf_ref.shape, F32)
        for v in range(SSD_CONV - 1):
            r = (v + 1) * SUBLANES - 1
            cbuf_ref[r:r + 1, :] = c0_ref[0, first_buffered + v:first_buffered + v + 1, :]

    x = _load_interleaved(x_refs, tile)
    hn = _rms(x, prew_ref[...])
    if n_pad:
        row = t * tile + _position_time(tile)[:, None]
        hn = jnp.where(row >= n_pad, hn, 0.0)
    hb = hn.astype(BF16)
    xbc = _dot(hb, win_ref[:, SSD_X0:SSD_DT0])
    dtr = _narrow_proj(hb, win_ref, SSD_DT0, SSD_HEADS)
    xc = _ssd_conv(xbc, cbuf_ref, cw_ref[...], cb_ref[...], tile)
    z = _dot(hb, win_ref[:, SSD_Z0:SSD_X0])
    xs = xc[:, :SSD_DINNER]
    bm = xc[:, SSD_DINNER:SSD_DINNER + SSD_GROUPS * SSD_STATE]
    cm = xc[:, SSD_DINNER + SSD_GROUPS * SSD_STATE:]
    dt = _softplus(dtr + dtb_ref[...])
    if n_pad:
        dt = jnp.where(row >= n_pad, dt, 0.0)
    a = dt * (-jnp.exp(alog_ref[...]))

    tm = _position_time(CHUNK)
    causal = tm[:, None] >= tm[None, :]
    tril = causal.astype(BF16)
    lo_half = lax.broadcasted_iota(jnp.int32, (CHUNK, 2 * SSD_HEADDIM), 1) < SSD_HEADDIM
    expand = expand_ref[...]
    for c in range(tile // CHUNK):
        rows = slice(c * CHUNK, (c + 1) * CHUNK)
        _ssd_scan_chunk(xs[rows], bm[rows], cm[rows], dt[rows], a[rows], expand,
                        st_ref, ys_ref, c * CHUNK, causal, tril, lo_half)
    y = ys_ref[...] + xs * dskip_ref[...]
    out = _ssd_tail(y, z, wn_ref[...], wo_ref[...], postw_ref[...], x)
    _store_interleaved(y_ref, slab_ref, out, tile)

    @pl.when(t == pl.num_programs(1) - 1)
    def _():
        for gi in range(SSD_GROUPS):
            sout_ref[0, gi * SSD_HPG:(gi + 1) * SSD_HPG] = (
                st_ref[gi].T.reshape(SSD_HPG, SSD_HEADDIM, SSD_STATE))
        cout_ref[0] = jnp.zeros(cout_ref.shape[1:], F32)
        for v in range(SSD_CONV - 1):
            r = (v + 1) * SUBLANES - 1
            cout_ref[0, first_buffered + v:first_buffered + v + 1, :] = cbuf_ref[r:r + 1, :]


def _ssd_weight_list(w, step):
    names = ("pre", "post", "win", "cw", "cb", "dtb", "alog", "dskip", "wn", "wo")
    consts = ("expand", "expand_n") if step else ("expand",)
    return [(w[k], True) for k in names] + [(w[k], False) for k in consts]


def _ssd_prompt_layer(x, s0, c0, prev_out, layer, w, *, n_pad, tile):
    bsz, seq, _ = x.shape
    s_tail = (SSD_HEADS, SSD_HEADDIM, SSD_STATE)
    c_tail = (SUBLANES, SSD_CONV_DIM)
    own = s0.shape[1] == bsz
    weights = _ssd_weight_list(w, step=False)
    carried = [] if prev_out is None else list(prev_out)
    n_in = D_SLABS + 2 + len(weights)
    x_slabs = [pl.BlockSpec((1, tile, LANES), lambda b, t, j=j: (b, t, j)) for j in range(D_SLABS)]
    return pl.pallas_call(
        functools.partial(_ssd_prompt_kernel, n_pad=n_pad, tile=tile, has_prev=bool(carried)),
        grid=(bsz, seq // tile),
        in_specs=x_slabs
                 + [_batch_block(layer, 1, s_tail, own), _batch_block(layer, 1, c_tail, own)]
                 + _weight_specs(weights, layer)
                 + [pl.BlockSpec(memory_space=pl.ANY)] * len(carried),
        out_specs=[pl.BlockSpec((1, tile, D_MODEL), lambda b, t: (b, t, 0)),
                   _batch_block(layer, 1, s_tail), _batch_block(layer, 1, c_tail)],
        out_shape=[jax.ShapeDtypeStruct(x.shape, F32),
                   jax.ShapeDtypeStruct((N_LAYERS_PER_MIXER, bsz) + s_tail, F32),
                   jax.ShapeDtypeStruct((N_LAYERS_PER_MIXER, bsz) + c_tail, F32)],
        scratch_shapes=[pltpu.VMEM((SSD_GROUPS, SSD_STATE, SSD_GW), F32),
                        pltpu.VMEM((WRAP_ROWS, SSD_CONV_DIM), F32),
                        pltpu.VMEM((tile, SSD_DINNER), F32),
                        pltpu.VMEM((D_SLABS, tile, LANES), F32)],
        input_output_aliases={n_in: 1, n_in + 1: 2} if carried else {},
        compiler_params=pltpu.CompilerParams(
            dimension_semantics=("arbitrary", "arbitrary"), vmem_limit_bytes=VMEM_LIMIT),
        name="ssd_prompt",
    )(*([x] * D_SLABS), s0, c0, *[a for a, _ in weights], *carried)


def _column(row, width, lane0=0):
    seg = row[:, lane0:lane0 + width]
    r = lax.broadcasted_iota(jnp.int32, (width, width), 0)
    c = lax.broadcasted_iota(jnp.int32, (width, width), 1)
    return jnp.sum(jnp.where(r == c, seg, 0.0), axis=1, keepdims=True)


def _gla_step_kernel(x_ref, s_ref, prew_ref, postw_ref, win_ref, wgu_ref, bg_ref, wn_ref, wo_ref,
                     *rest, bb, has_prev):
    y_ref, sout_ref, q_s, kl_s, v_s, e_s, o_s = rest[1:] if has_prev else rest
    i = pl.program_id(0)
    bsz = x_ref.shape[0]

    @pl.when(i == 0)
    def _():
        hb = _rms(x_ref[...], prew_ref[...]).astype(BF16)
        q_s[...] = _dot(hb, win_ref[:, GLA_Q0:GLA_K0]) * (GLA_HK ** -0.5)
        o_s[:, 0:GLA_DK] = _dot(hb, win_ref[:, GLA_K0:GLA_V0])
        k_t = o_s[:, 0:GLA_DK].T
        for h in range(GLA_HEADS):
            kl_s[h // 2, :, (h % 2) * bsz:(h % 2 + 1) * bsz] = (
                k_t[h * GLA_HK:(h + 1) * GLA_HK].astype(BF16))
        v_s[...] = _dot(hb, win_ref[:, GLA_V0:GLA_R0])
        e_s[...] = jnp.exp(_gla_gates(hb, win_ref, wgu_ref, bg_ref))

    sub = lax.broadcasted_iota(jnp.int32, (bsz, GLA_HV), 0)
    zero_bv = jnp.zeros((bsz, GLA_HV), BF16)

    for j in range(bb):
        row = i * bb + j
        qr = q_s[pl.ds(row, 1), :]
        er = e_s[pl.ds(row, 1), :]
        for p in range(GLA_HEADS // 2):
            vsel = [jnp.where(sub == row, v_s[:, (2 * p + e) * GLA_HV:(2 * p + e + 1) * GLA_HV],
                              0.0).astype(BF16) for e in range(2)]
            rhs = jnp.concatenate([jnp.concatenate([vsel[0], zero_bv], axis=1),
                                   jnp.concatenate([zero_bv, vsel[1]], axis=1)], axis=0)
            kv = _dot(kl_s[p], rhs)
            for e in range(2):
                h = 2 * p + e
                vs = slice(h * GLA_HV, (h + 1) * GLA_HV)
                ecol = _column(er, GLA_HK, h * GLA_HK)
                qcol = _column(qr, GLA_HK, h * GLA_HK)
                s_new = s_ref[j, h] * ecol + kv[:, e * GLA_HV:(e + 1) * GLA_HV]
                sout_ref[j, h] = s_new
                o_s[pl.ds(row, 1), vs] = jnp.sum(qcol * s_new, axis=0, keepdims=True)

    @pl.when(i == pl.num_programs(0) - 1)
    def _():
        x = x_ref[...]
        hb = _rms(x, prew_ref[...]).astype(BF16)
        r = _dot(hb, win_ref[:, GLA_R0:GLA_G0])
        y_ref[...] = _gla_tail(o_s[...], r, wn_ref[...], wo_ref[...], postw_ref[...], x)


def _gla_step_layer(x, states, prev_out, layer, w, *, bb=8):
    bsz = x.shape[0]
    weights = _gla_weight_list(w)
    sblk = _batch_block(layer, bb, (GLA_HEADS, GLA_HK, GLA_HV))
    carried = [] if prev_out is None else [prev_out]
    return pl.pallas_call(
        functools.partial(_gla_step_kernel, bb=bb, has_prev=bool(carried)),
        grid=(bsz // bb,),
        in_specs=[_full(x.shape), sblk] + _weight_specs(weights, layer)
                 + [pl.BlockSpec(memory_space=pl.ANY)] * len(carried),
        out_specs=[pl.BlockSpec(x.shape, lambda i: (0, 0)), sblk],
        out_shape=[jax.ShapeDtypeStruct(x.shape, F32), jax.ShapeDtypeStruct(states.shape, F32)],
        scratch_shapes=[pltpu.VMEM((bsz, GLA_DK), F32),
                        pltpu.VMEM((GLA_HEADS // 2, GLA_HK, 2 * bsz), BF16),
                        pltpu.VMEM((bsz, GLA_DV), F32),
                        pltpu.VMEM((bsz, GLA_DK), F32),
                        pltpu.VMEM((bsz, GLA_DV), F32)],
        input_output_aliases={2 + len(weights): 1} if carried else {},
        compiler_params=pltpu.CompilerParams(
            dimension_semantics=("arbitrary",), vmem_limit_bytes=VMEM_LIMIT),
        name="gla_step",
    )(x, states, *[a for a, _ in weights], *carried)


def _ssd_step_kernel(x_ref, s_ref, cv_ref, prew_ref, postw_ref, win_ref, cw_ref, cb_ref, dtb_ref,
                     alog_ref, dskip_ref, wn_ref, wo_ref, expand_ref, expand_n_ref, *rest,
                     bb, has_prev):
    (y_ref, sout_ref, cvout_ref, xs_s, xl_s, ct_s, b_s, e_s, yt_s) = rest[2:] if has_prev else rest
    i = pl.program_id(0)
    bsz = x_ref.shape[0]
    cd = SSD_CONV_DIM

    @pl.when(i == 0)
    def _():
        hb = _rms(x_ref[...], prew_ref[...]).astype(BF16)
        xbc = _dot(hb, win_ref[:, SSD_X0:SSD_DT0])
        dtr = _narrow_proj(hb, win_ref, SSD_DT0, SSD_HEADS)
        cw = cw_ref[...]
        conv = cb_ref[...] + xbc * cw[3:4, :]
        for t in range(SSD_CONV - 1):
            conv = conv + cv_ref[:, t * cd:(t + 1) * cd] * cw[t:t + 1, :]
        cvout_ref[:, 0:cd] = cv_ref[:, cd:2 * cd]
        cvout_ref[:, cd:2 * cd] = cv_ref[:, 2 * cd:3 * cd]
        cvout_ref[:, 2 * cd:3 * cd] = xbc
        xc = _silu(conv)
        xs = xc[:, :SSD_DINNER]
        dt = _softplus(dtr + dtb_ref[...])
        ea = jnp.exp(dt * (-jnp.exp(alog_ref[...])))
        xs_s[...] = xs
        xd_t = (xs * _dot_exact_rhs(dt, expand_ref[...])).T
        c_t = xc[:, SSD_DINNER + SSD_GROUPS * SSD_STATE:].T
        for p in range(SSD_GROUPS // 2):
            for e in range(2):
                gi = 2 * p + e
                xl_s[p, :, e * bsz:(e + 1) * bsz] = (
                    xd_t[gi * SSD_GW:(gi + 1) * SSD_GW].astype(BF16))
        ct_s[...] = c_t
        b_s[...] = xc[:, SSD_DINNER:SSD_DINNER + SSD_GROUPS * SSD_STATE]
        e_s[...] = _dot_exact_rhs(ea, expand_n_ref[...])
        yt_s[...] = jnp.zeros(yt_s.shape, F32)

    sub = lax.broadcasted_iota(jnp.int32, (bsz, SSD_STATE), 0)
    lane = lax.broadcasted_iota(jnp.int32, (SSD_STATE, bsz), 1)
    zero_bn = jnp.zeros((bsz, SSD_STATE), BF16)
    zero_nb = jnp.zeros((SSD_STATE, bsz), BF16)

    for p in range(SSD_GROUPS // 2):
        y_acc = None
        for j in range(bb):
            row = i * bb + j
            er = e_s[pl.ds(row, 1), :]
            bsel, csel = [], []
            for e in range(2):
                ns = slice((2 * p + e) * SSD_STATE, (2 * p + e + 1) * SSD_STATE)
                bsel.append(jnp.where(sub == row, b_s[:, ns], 0.0).astype(BF16))
                csel.append(jnp.where(lane == row, ct_s[ns, :], 0.0).astype(BF16))
            rhs_b = jnp.concatenate([jnp.concatenate([bsel[0], zero_bn], axis=1),
                                     jnp.concatenate([zero_bn, bsel[1]], axis=1)], axis=0)
            rhs_c = jnp.concatenate([jnp.concatenate([csel[0], zero_nb], axis=1),
                                     jnp.concatenate([zero_nb, csel[1]], axis=1)], axis=0)
            outer = _dot(xl_s[p], rhs_b)
            halves = []
            for e in range(2):
                pieces = []
                for r in range(SSD_HPG):
                    h = (2 * p + e) * SSD_HPG + r
                    s_new = (s_ref[j, h] * er[:, h * SSD_STATE:(h + 1) * SSD_STATE]
                             + outer[r * SSD_HEADDIM:(r + 1) * SSD_HEADDIM,
                                     e * SSD_STATE:(e + 1) * SSD_STATE])
                    sout_ref[j, h] = s_new
                    pieces.append(s_new.astype(BF16))
                halves.append(jnp.concatenate(pieces, axis=0))
            y_j = _dot(jnp.concatenate(halves, axis=1), rhs_c)
            y_acc = y_j if y_acc is None else y_acc + y_j
        yt_s[p] += y_acc

    @pl.when(i == pl.num_programs(0) - 1)
    def _():
        x = x_ref[...]
        hb = _rms(x, prew_ref[...]).astype(BF16)
        z = _dot(hb, win_ref[:, SSD_Z0:SSD_X0])
        y = jnp.concatenate([yt_s[p, :, e * bsz:(e + 1) * bsz].T
                             for p in range(SSD_GROUPS // 2) for e in range(2)], axis=1)
        y = y + xs_s[...] * dskip_ref[...]
        y_ref[...] = _ssd_tail(y, z, wn_ref[...], wo_ref[...], postw_ref[...], x)


def _ssd_step_layer(x, states, convs, prev_out, layer, w, *, bb=4):
    bsz = x.shape[0]
    weights = _ssd_weight_list(w, step=True)
    sblk = _batch_block(layer, bb, (SSD_HEADS, SSD_HEADDIM, SSD_STATE))
    cblk = pl.BlockSpec((None,) + convs.shape[1:], lambda i: (layer, 0, 0))
    carried = [] if prev_out is None else list(prev_out)
    n_in = 3 + len(weights)
    return pl.pallas_call(
        functools.partial(_ssd_step_kernel, bb=bb, has_prev=bool(carried)),
        grid=(bsz // bb,),
        in_specs=[_full(x.shape), sblk,
                  pl.BlockSpec(cblk.block_shape, cblk.index_map, pipeline_mode=pl.Buffered(1))]
                 + _weight_specs(weights, layer)
                 + [pl.BlockSpec(memory_space=pl.ANY)] * len(carried),
        out_specs=[pl.BlockSpec(x.shape, lambda i: (0, 0)), sblk, cblk],
        out_shape=[jax.ShapeDtypeStruct(x.shape, F32), jax.ShapeDtypeStruct(states.shape, F32),
                   jax.ShapeDtypeStruct(convs.shape, F32)],
        input_output_aliases={n_in: 1, n_in + 1: 2} if carried else {},
        scratch_shapes=[pltpu.VMEM((bsz, SSD_DINNER), F32),
                        pltpu.VMEM((SSD_GROUPS // 2, SSD_GW, 2 * bsz), BF16),
                        pltpu.VMEM((SSD_GROUPS * SSD_STATE, bsz), F32),
                        pltpu.VMEM((bsz, SSD_GROUPS * SSD_STATE), F32),
                        pltpu.VMEM((bsz, SSD_HEADS * SSD_STATE), F32),
                        pltpu.VMEM((SSD_GROUPS // 2, SSD_GW, 2 * bsz), F32)],
        compiler_params=pltpu.CompilerParams(
            dimension_semantics=("arbitrary",), vmem_limit_bytes=VMEM_LIMIT),
        name="ssd_step",
    )(x, states, convs, *[a for a, _ in weights], *carried)


def _pad_last(a, n):
    return jnp.pad(a, [(0, 0)] * (a.ndim - 1) + [(0, n - a.shape[-1])])


def _rows(a):
    return a[:, None, :]


def _gla_weights(pre, post, w_in, w_gate_up, b_gate, w_norm, w_out):
    return {
        "pre": _rows(pre), "post": _rows(post),
        "win": w_in.astype(BF16),
        "wgu": jnp.pad(w_gate_up, ((0, 0), (0, LANES - GLA_RANK), (0, 0))).astype(BF16),
        "bg": _rows(b_gate), "wn": _rows(w_norm), "wo": w_out.astype(BF16),
    }


def _ssd_weights(pre, post, w_in, conv_w, conv_b, dt_bias, a_log, d_skip, w_norm, w_out):
    head_of_lane = jnp.arange(SSD_DINNER) // SSD_HEADDIM
    head_of_lane_n = jnp.arange(SSD_HEADS * SSD_STATE) // SSD_STATE
    heads = jnp.arange(LANES)[:, None]
    return {
        "pre": _rows(pre), "post": _rows(post),
        "win": w_in.astype(BF16),
        "cw": jnp.pad(conv_w, ((0, 0), (0, SUBLANES - SSD_CONV), (0, 0))), "cb": _rows(conv_b),
        "dtb": _rows(_pad_last(dt_bias, LANES)), "alog": _rows(_pad_last(a_log, LANES)),
        "dskip": _rows(jnp.repeat(d_skip, SSD_HEADDIM, axis=1)),
        "wn": _rows(w_norm), "wo": w_out.astype(BF16),
        "expand": (heads == head_of_lane[None, :]).astype(BF16),
        "expand_n": (heads == head_of_lane_n[None, :]).astype(BF16),
    }


def _prompt_trunk(x, gla0, ssm0, conv0, gw, sw, *, n_pad, gla_tile, ssd_tile):
    gla_s = ssm_conv_s = None
    for j in range(N_LAYERS_PER_MIXER):
        x, gla_s = _gla_prompt_layer(x, gla0, gla_s, j, gw, n_pad=n_pad, tile=gla_tile)
        x, *ssm_conv_s = _ssd_prompt_layer(x, ssm0, conv0, ssm_conv_s, j, sw,
                                           n_pad=n_pad, tile=ssd_tile)
    return x, gla_s, ssm_conv_s[0], ssm_conv_s[1]


def kernel(x_prompt, x_sample, state_gla, state_ssm, state_conv, meta_tokens, pre_norm, post_norm,
           gla_w_in, gla_w_gate_up, gla_b_gate, gla_w_norm, gla_w_out,
           ssd_w_in, ssd_conv_w, ssd_conv_b, ssd_dt_bias, ssd_a_log, ssd_d_skip, ssd_w_norm, ssd_w_out):
    gw = _gla_weights(pre_norm[0::2], post_norm[0::2], gla_w_in, gla_w_gate_up, gla_b_gate,
                      gla_w_norm, gla_w_out)
    sw = _ssd_weights(pre_norm[1::2], post_norm[1::2], ssd_w_in, ssd_conv_w, ssd_conv_b,
                      ssd_dt_bias, ssd_a_log, ssd_d_skip, ssd_w_norm, ssd_w_out)

    n_pad = CHUNK - N_META
    x_meta = jnp.pad(meta_tokens.astype(F32), ((n_pad, 0), (0, 0)))[None]
    zg = jnp.zeros((N_LAYERS_PER_MIXER, 1, GLA_HEADS, GLA_HK, GLA_HV), F32)
    zs = jnp.zeros((N_LAYERS_PER_MIXER, 1, SSD_HEADS, SSD_HEADDIM, SSD_STATE), F32)
    zc = jnp.zeros((N_LAYERS_PER_MIXER, 1, SUBLANES, SSD_CONV_DIM), F32)
    _, mg, ms, mc = _prompt_trunk(x_meta, zg, zs, zc, gw, sw,
                                  n_pad=n_pad, gla_tile=CHUNK, ssd_tile=CHUNK)
    y_prompt, gla_p, ssm_p, conv_p = _prompt_trunk(x_prompt, mg, ms, mc, gw, sw, n_pad=0,
                                                   gla_tile=GLA_TILE, ssd_tile=SSD_TILE)
    conv_p = conv_p[:, :, SUBLANES - (SSD_CONV - 1):, :]

    xs = x_sample[:, 0, :]
    sbsz = xs.shape[0]
    convs = state_conv.reshape(N_LAYERS_PER_MIXER, sbsz, (SSD_CONV - 1) * SSD_CONV_DIM)
    gla_s = ssm_conv_s = None
    for j in range(N_LAYERS_PER_MIXER):
        xs, gla_s = _gla_step_layer(xs, state_gla, gla_s, j, gw)
        xs, *ssm_conv_s = _ssd_step_layer(xs, state_ssm, convs, ssm_conv_s, j, sw)
    ssm_s, conv_s = ssm_conv_s
    y_sample = xs[:, None, :]
    return (y_prompt, y_sample, gla_p, ssm_p, conv_p, gla_s, ssm_s,
            conv_s.reshape(N_LAYERS_PER_MIXER, sbsz, SSD_CONV - 1, SSD_CONV_DIM))
```

```python
import functools

import jax
import jax.numpy as jnp
from jax import lax
from jax.experimental import pallas as pl
from jax.experimental.pallas import tpu as pltpu

F32 = jnp.float32
BF16 = jnp.bfloat16

D_MODEL = 1024
N_META = 16
NORM_EPS = 1e-6
N_LAYERS_PER_MIXER = 2

GLA_HEADS = 4
GLA_DK = 512
GLA_DV = 1024
GLA_HK = 128
GLA_HV = 256
GLA_RANK = 16
GLA_TAU = 16.0
GLA_SAFE_DROP = 40.0

SSD_DINNER = 2048
SSD_HEADDIM = 64
SSD_HEADS = 32
SSD_GROUPS = 4
SSD_HPG = 8
SSD_STATE = 128
SSD_CONV = 4
SSD_CONV_DIM = 3072
SSD_GW = SSD_HPG * SSD_HEADDIM

LANES = 128
SUBLANES = 8
CHUNK = 128
GLA_TILE = 1024
SSD_TILE = 512
VMEM_LIMIT = 56 * 1024 * 1024

GLA_Q0, GLA_K0, GLA_V0, GLA_R0, GLA_G0 = 0, GLA_DK, 2 * GLA_DK, 2 * GLA_DK + GLA_DV, 2 * GLA_DK + 2 * GLA_DV
SSD_Z0, SSD_X0, SSD_DT0 = 0, SSD_DINNER, SSD_DINNER + SSD_CONV_DIM


def _narrow_proj(hb, w_ref, start, width):
    res = jnp.dot(hb, w_ref[:, start:start + width], preferred_element_type=F32)
    return jnp.concatenate([res, jnp.zeros((res.shape[0], LANES - width), F32)], axis=1)


def _dot(a, b):
    return jnp.dot(a, b, preferred_element_type=F32)


def _dot_nt(a, b):
    return lax.dot_general(a, b, (((1,), (1,)), ((), ())), preferred_element_type=F32)


def _dot_tn(a, b):
    return lax.dot_general(a, b, (((0,), (0,)), ((), ())), preferred_element_type=F32)


def _rms(x, w):
    return x * lax.rsqrt(jnp.mean(x * x, axis=-1, keepdims=True) + NORM_EPS) * w


def _silu(x):
    h = 0.5 * x
    return h + h * jnp.tanh(h)


def _softplus(x):
    return jnp.maximum(x, 0.0) + jnp.log1p(jnp.exp(-jnp.abs(x)))


def _log_sigmoid(x):
    return -_softplus(-x)


def _tri(n):
    r = lax.broadcasted_iota(jnp.int32, (n, n), 0)
    c = lax.broadcasted_iota(jnp.int32, (n, n), 1)
    return r >= c


def _split3(x):
    hi = x.astype(BF16)
    r1 = x - hi.astype(F32)
    mid = r1.astype(BF16)
    lo = (r1 - mid.astype(F32)).astype(BF16)
    return hi, mid, lo


def _dot_exact_rhs(x, m_bf16):
    hi, mid, lo = _split3(x)
    return _dot(hi, m_bf16) + _dot(mid, m_bf16) + _dot(lo, m_bf16)


def _cumsum_rows(tril_bf16, x):
    hi, mid, lo = _split3(x)
    if x.shape[1] < 2 * LANES:
        return _dot(tril_bf16, hi) + _dot(tril_bf16, mid) + _dot(tril_bf16, lo)
    two = jnp.concatenate([tril_bf16, tril_bf16], axis=1)
    return _dot(two, jnp.concatenate([hi, mid], axis=0)) + _dot(tril_bf16, lo)


def _full(shape):
    nd = len(shape)
    return pl.BlockSpec(shape, lambda *_: (0,) * nd, pipeline_mode=pl.Buffered(1))


def _of_layer(a, layer):
    nz = a.ndim - 1
    return pl.BlockSpec((None,) + a.shape[1:], lambda *_: (layer,) + (0,) * nz,
                        pipeline_mode=pl.Buffered(1))


def _weight_specs(weights, layer):
    return [_of_layer(a, layer) if stacked else _full(a.shape) for a, stacked in weights]


def _batch_block(layer, bb, tail, own=True):
    nz = len(tail)
    return pl.BlockSpec((None, bb) + tail,
                        lambda i, *_: (layer, i if own else 0) + (0,) * nz)


def _gla_attn_pairwise(qf, kf, bc, causal, kbuf_ref, bbuf_ref):
    n = qf.shape[0]
    kbuf_ref[...] = kf
    bbuf_ref[...] = bc
    lane = lax.broadcasted_iota(jnp.int32, (n, n), 1)

    def body(j, acc):
        kj = kbuf_ref[pl.ds(j, 1), :]
        bj = bbuf_ref[pl.ds(j, 1), :]
        col = jnp.sum(qf * jnp.exp(jnp.minimum(bc - bj, 0.0)) * kj, axis=1, keepdims=True)
        return jnp.where(lane == j, col, acc)

    acc = lax.fori_loop(0, n, body, jnp.zeros((n, n), F32))
    return jnp.where(causal, acc, 0.0)


def _gla_scan_tile(q, k, v, g, states, pairwise, kbuf_ref, bbuf_ref):
    causal = _tri(CHUNK)
    tril = causal.astype(BF16)
    states = list(states)
    outs = []
    for c in range(q.shape[0] // CHUNK):
        rows = slice(c * CHUNK, (c + 1) * CHUNK)
        bc = _cumsum_rows(tril, g[rows])
        b_last = bc[-1:, :]
        e_last = jnp.exp(b_last)
        qh = (q[rows] * jnp.exp(bc)).astype(BF16)
        if pairwise:
            kd = (k[rows] * jnp.exp(b_last - bc)).astype(BF16)
        else:
            kh = k[rows] * jnp.exp(-bc)
            kd = (kh * e_last).astype(BF16)
            kh = kh.astype(BF16)
        vb = v[rows].astype(BF16)
        zk = jnp.zeros((CHUNK, GLA_HK), BF16)
        parts = []
        for h0 in range(0, GLA_HEADS, 2):
            ks = [slice((h0 + e) * GLA_HK, (h0 + e + 1) * GLA_HK) for e in range(2)]
            vs = [slice((h0 + e) * GLA_HV, (h0 + e + 1) * GLA_HV) for e in range(2)]
            ks2 = slice(ks[0].start, ks[1].stop)
            if pairwise:
                attn = [_gla_attn_pairwise(q[rows, ks[e]], k[rows, ks[e]], bc[:, ks[e]], causal,
                                           kbuf_ref, bbuf_ref) for e in range(2)]
            else:
                k_bd = jnp.concatenate([jnp.concatenate([kh[:, ks[0]], zk], axis=1),
                                        jnp.concatenate([zk, kh[:, ks[1]]], axis=1)], axis=0)
                scores = _dot_nt(qh[:, ks2], k_bd)
                attn = [jnp.where(causal, scores[:, e * CHUNK:(e + 1) * CHUNK], 0.0)
                        for e in range(2)]
            kd_bd = jnp.concatenate([jnp.concatenate([kd[:, ks[0]], zk], axis=1),
                                     jnp.concatenate([zk, kd[:, ks[1]]], axis=1)], axis=0)
            upd = _dot_tn(jnp.concatenate([vb[:, vs[0]], vb[:, vs[1]]], axis=0), kd_bd)
            for e in range(2):
                st = states[h0 + e]
                parts.append(_dot(attn[e].astype(BF16), vb[:, vs[e]])
                             + _dot_nt(qh[:, ks[e]], st.astype(BF16)))
                states[h0 + e] = st * e_last[:, ks[e]] + upd[:, e * GLA_HK:(e + 1) * GLA_HK]
        outs.append(jnp.concatenate(parts, axis=1))
    o = jnp.concatenate(outs, axis=0) if len(outs) > 1 else outs[0]
    return o, states


def _gla_tail(o, r, wn, wo, postw, x):
    parts = []
    for h in range(GLA_HEADS):
        vs = slice(h * GLA_HV, (h + 1) * GLA_HV)
        parts.append(_rms(o[:, vs], wn[:, vs]))
    on = jnp.concatenate(parts, axis=1) * _silu(r)
    y = _dot(on.astype(BF16), wo)
    return x + _rms(y, postw)


def _gla_gates(hb, win_ref, wgu_ref, bg_ref):
    gl = _narrow_proj(hb, win_ref, GLA_G0, GLA_RANK)
    return _log_sigmoid(_dot(gl.astype(BF16), wgu_ref[...]) + bg_ref[...]) * (1.0 / GLA_TAU)


def _gla_prompt_kernel(x_ref, s0_ref, prew_ref, postw_ref, win_ref, wgu_ref, bg_ref, wn_ref,
                       wo_ref, *rest, n_pad, tile, has_prev):
    y_ref, sout_ref, st_ref, o_ref, kbuf_ref, bbuf_ref = rest[1:] if has_prev else rest
    t = pl.program_id(1)

    @pl.when(t == 0)
    def _():
        for h in range(GLA_HEADS):
            st_ref[h] = s0_ref[0, h].T

    x = x_ref[0]
    hn = _rms(x, prew_ref[...])
    if n_pad:
        row = lax.broadcasted_iota(jnp.int32, (tile, 1), 0) + t * tile
        hn = jnp.where(row >= n_pad, hn, 0.0)
    hb = hn.astype(BF16)
    q = _dot(hb, win_ref[:, GLA_Q0:GLA_K0]) * (GLA_HK ** -0.5)
    k = _dot(hb, win_ref[:, GLA_K0:GLA_V0])
    v = _dot(hb, win_ref[:, GLA_V0:GLA_R0])
    g = _gla_gates(hb, win_ref, wgu_ref, bg_ref)

    drop = None
    for c in range(tile // CHUNK):
        tot = jnp.sum(g[c * CHUNK:(c + 1) * CHUNK], axis=0, keepdims=True)
        drop = tot if drop is None else jnp.minimum(drop, tot)
    safe = jnp.min(drop) >= -GLA_SAFE_DROP

    def scan(pairwise):
        o, states = _gla_scan_tile(q, k, v, g, [st_ref[h] for h in range(GLA_HEADS)], pairwise,
                                   kbuf_ref, bbuf_ref)
        o_ref[...] = o
        for h in range(GLA_HEADS):
            st_ref[h] = states[h]

    pl.when(safe)(functools.partial(scan, False))
    pl.when(jnp.logical_not(safe))(functools.partial(scan, True))
    r = _dot(hb, win_ref[:, GLA_R0:GLA_G0])
    y_ref[0] = _gla_tail(o_ref[...], r, wn_ref[...], wo_ref[...], postw_ref[...], x)

    @pl.when(t == pl.num_programs(1) - 1)
    def _():
        for h in range(GLA_HEADS):
            sout_ref[0, h] = st_ref[h].T


def _gla_weight_list(w):
    return [(w[k], True) for k in ("pre", "post", "win", "wgu", "bg", "wn", "wo")]


def _gla_prompt_layer(x, s0, prev_out, layer, w, *, n_pad, tile):
    bsz, seq, _ = x.shape
    tail = (GLA_HEADS, GLA_HK, GLA_HV)
    weights = _gla_weight_list(w)
    carried = [] if prev_out is None else [prev_out]
    return pl.pallas_call(
        functools.partial(_gla_prompt_kernel, n_pad=n_pad, tile=tile, has_prev=bool(carried)),
        grid=(bsz, seq // tile),
        in_specs=[pl.BlockSpec((1, tile, D_MODEL), lambda b, t: (b, t, 0)),
                  _batch_block(layer, 1, tail, own=s0.shape[1] == bsz)]
                 + _weight_specs(weights, layer)
                 + [pl.BlockSpec(memory_space=pl.ANY)] * len(carried),
        out_specs=[pl.BlockSpec((1, tile, D_MODEL), lambda b, t: (b, t, 0)),
                   _batch_block(layer, 1, tail)],
        out_shape=[jax.ShapeDtypeStruct(x.shape, F32),
                   jax.ShapeDtypeStruct((N_LAYERS_PER_MIXER, bsz) + tail, F32)],
        scratch_shapes=[pltpu.VMEM((GLA_HEADS, GLA_HV, GLA_HK), F32),
                        pltpu.VMEM((tile, GLA_DV), F32),
                        pltpu.VMEM((CHUNK, GLA_HK), F32),
                        pltpu.VMEM((CHUNK, GLA_HK), F32)],
        input_output_aliases={2 + len(weights): 1} if carried else {},
        compiler_params=pltpu.CompilerParams(
            dimension_semantics=("arbitrary", "arbitrary"), vmem_limit_bytes=VMEM_LIMIT),
        name="gla_prompt",
    )(x, s0, *[a for a, _ in weights], *carried)


UNIT_GROUPS = 4
UNIT_ROWS = UNIT_GROUPS * SUBLANES
WRAP_ROWS = (SSD_CONV - 1) * SUBLANES
D_SLABS = D_MODEL // LANES
assert SSD_CONV - 1 < UNIT_GROUPS and CHUNK % UNIT_ROWS == 0


def _position_time(n):
    p = lax.iota(jnp.int32, n)
    q = p % UNIT_ROWS
    return (p - q) + (q % SUBLANES) * UNIT_GROUPS + q // SUBLANES


def _load_interleaved(slab_refs, tile):
    groups = []
    for u in range(tile // UNIT_ROWS):
        for g in range(UNIT_GROUPS):
            rows = pl.ds(u * UNIT_ROWS + g, SUBLANES, stride=UNIT_GROUPS)
            groups.append(jnp.concatenate([r[0, rows, :] for r in slab_refs], axis=1))
    return jnp.concatenate(groups, axis=0)


def _store_interleaved(y_ref, slab_ref, y, tile):
    for u in range(tile // UNIT_ROWS):
        for g in range(UNIT_GROUPS):
            r0 = u * UNIT_ROWS + g * SUBLANES
            rows = pl.ds(u * UNIT_ROWS + g, SUBLANES, stride=UNIT_GROUPS)
            for j in range(D_SLABS):
                slab_ref[j, rows, :] = y[r0:r0 + SUBLANES, j * LANES:(j + 1) * LANES]
    y_ref[0] = jnp.concatenate([slab_ref[j] for j in range(D_SLABS)], axis=1)


def _ssd_conv(xbc, cbuf_ref, cw, cb, tile):
    sub = lax.broadcasted_iota(jnp.int32, (SUBLANES, 1), 0)
    outs = []
    prev = cbuf_ref[...]
    for u in range(tile // UNIT_ROWS):
        xb = xbc[u * UNIT_ROWS:(u + 1) * UNIT_ROWS]
        last = xb[UNIT_ROWS - WRAP_ROWS:]
        wrapped = []
        for v in range(SSD_CONV - 1):
            rs = slice(v * SUBLANES, (v + 1) * SUBLANES)
            mix = jnp.where(sub == SUBLANES - 1, prev[rs], last[rs])
            wrapped.append(pltpu.roll(mix, 1, axis=0))
        ext = jnp.concatenate(wrapped + [xb], axis=0)
        conv = cb + xb * cw[SSD_CONV - 1:SSD_CONV, :]
        for i in range(SSD_CONV - 1):
            conv = conv + ext[i * SUBLANES:i * SUBLANES + UNIT_ROWS] * cw[i:i + 1, :]
        outs.append(conv)
        prev = last
    cbuf_ref[...] = prev
    return _silu(jnp.concatenate(outs, axis=0))


def _ssd_scan_chunk(xs, bm, cm, dt, a, expand, st_ref, y_ref, row0, causal, tril, lo_half):
    ac = _cumsum_rows(tril, a)
    ac_t = ac.T
    dt_t = dt.T
    a_last = ac[-1:, :]
    wgt = jnp.exp(a_last - ac) * dt
    xw = (xs * _dot(wgt.astype(BF16), expand)).astype(BF16)
    pair = 2 * SSD_HEADDIM
    zn = jnp.zeros((CHUNK, SSD_STATE), BF16)
    cbs = []
    for gp in range(0, SSD_GROUPS, 2):
        n2 = slice(gp * SSD_STATE, (gp + 2) * SSD_STATE)
        b0 = bm[:, gp * SSD_STATE:(gp + 1) * SSD_STATE].astype(BF16)
        b1 = bm[:, (gp + 1) * SSD_STATE:(gp + 2) * SSD_STATE].astype(BF16)
        b_bd = jnp.concatenate([jnp.concatenate([b0, zn], axis=1),
                                jnp.concatenate([zn, b1], axis=1)], axis=0)
        both = _dot_nt(cm[:, n2].astype(BF16), b_bd)
        cbs += [both[:, :CHUNK], both[:, CHUNK:]]
    for gi in range(SSD_GROUPS):
        ns = slice(gi * SSD_STATE, (gi + 1) * SSD_STATE)
        gs = slice(gi * SSD_GW, (gi + 1) * SSD_GW)
        bg = bm[:, ns].astype(BF16)
        cg = cm[:, ns].astype(BF16)
        cb = cbs[gi]
        st = st_ref[gi]
        y_int = _dot(cg, st.astype(BF16))
        e_last = []
        for qi in range(SSD_HPG // 2):
            h0 = gi * SSD_HPG + 2 * qi
            ps = slice(h0 * SSD_HEADDIM, h0 * SSD_HEADDIM + pair)
            ms, eas = [], []
            for e in range(2):
                h = h0 + e
                col = jnp.broadcast_to(ac[:, h:h + 1], (CHUNK, CHUNK))
                seg = col - ac_t[h:h + 1, :]
                ms.append(jnp.where(causal, cb * jnp.exp(seg) * dt_t[h:h + 1, :], 0.0).astype(BF16))
                eas.append(jnp.exp(col))
            xp = xs[:, ps]
            rhs = jnp.concatenate([jnp.where(lo_half, xp, 0.0).astype(BF16),
                                   jnp.where(lo_half, 0.0, xp).astype(BF16)], axis=0)
            ea = jnp.where(lo_half, eas[0], eas[1])
            e_last.append(ea[CHUNK - 1:CHUNK, :])
            y_ref[row0:row0 + CHUNK, ps] = (_dot(jnp.concatenate(ms, axis=1), rhs)
                                            + y_int[:, qi * pair:(qi + 1) * pair] * ea)
        st_ref[gi] = st * jnp.concatenate(e_last, axis=1) + _dot_tn(bg, xw[:, gs])


def _ssd_tail(y, z, wn, wo, postw, x):
    y = y * _silu(z)
    parts = []
    for gi in range(SSD_GROUPS):
        gs = slice(gi * SSD_GW, (gi + 1) * SSD_GW)
        parts.append(_rms(y[:, gs], wn[:, gs]))
    yn = jnp.concatenate(parts, axis=1)
    out = _dot(yn.astype(BF16), wo)
    return x + _rms(out, postw)


def _ssd_prompt_kernel(*refs, n_pad, tile, has_prev):
    x_refs, refs = refs[:D_SLABS], refs[D_SLABS:]
    (s0_ref, c0_ref, prew_ref, postw_ref, win_ref, cw_ref, cb_ref, dtb_ref, alog_ref, dskip_ref,
     wn_ref, wo_ref, expand_ref) = refs[:13]
    y_ref, sout_ref, cout_ref, st_ref, cbuf_ref, ys_ref, slab_ref = refs[13 + (2 if has_prev else 0):]
    t = pl.program_id(1)
    first_buffered = SUBLANES - (SSD_CONV - 1)

    @pl.when(t == 0)
    def _():
        for gi in range(SSD_GROUPS):
            s0 = s0_ref[0, gi * SSD_HPG:(gi + 1) * SSD_HPG].reshape(SSD_GW, SSD_STATE)
            st_ref[gi] = s0.T
        cbuf_ref[...] = jnp.zeros(cbuf_ref.shape, F32)
        for v in range(SSD_CONV - 1):
            r = (v + 1) * SUBLANES - 1
            cbuf_ref[r:r + 1, :] = c0_ref[0, first_buffered + v:first_buffered + v + 1, :]

    x = _load_interleaved(x_refs, tile)
    hn = _rms(x, prew_ref[...])
    if n_pad:
        row = t * tile + _position_time(tile)[:, None]
        hn = jnp.where(row >= n_pad, hn, 0.0)
    hb = hn.astype(BF16)
    xbc = _dot(hb, win_ref[:, SSD_X0:SSD_DT0])
    dtr = _narrow_proj(hb, win_ref, SSD_DT0, SSD_HEADS)
    xc = _ssd_conv(xbc, cbuf_ref, cw_ref[...], cb_ref[...], tile)
    z = _dot(hb, win_ref[:, SSD_Z0:SSD_X0])
    xs = xc[:, :SSD_DINNER]
    bm = xc[:, SSD_DINNER:SSD_DINNER + SSD_GROUPS * SSD_STATE]
    cm = xc[:, SSD_DINNER + SSD_GROUPS * SSD_STATE:]
    dt = _softplus(dtr + dtb_ref[...])
    if n_pad:
        dt = jnp.where(row >= n_pad, dt, 0.0)
    a = dt * (-jnp.exp(alog_ref[...]))

    tm = _position_time(CHUNK)
    causal = tm[:, None] >= tm[None, :]
    tril = causal.astype(BF16)
    lo_half = lax.broadcasted_iota(jnp.int32, (CHUNK, 2 * SSD_HEADDIM), 1) < SSD_HEADDIM
    expand = expand_ref[...]
    for c in range(tile // CHUNK):
        rows = slice(c * CHUNK, (c + 1) * CHUNK)
        _ssd_scan_chunk(xs[rows], bm[rows], cm[rows], dt[rows], a[rows], expand,
                        st_ref, ys_ref, c * CHUNK, causal, tril, lo_half)
    y = ys_ref[...] + xs * dskip_ref[...]
    out = _ssd_tail(y, z, wn_ref[...], wo_ref[...], postw_ref[...], x)
    _store_interleaved(y_ref, slab_ref, out, tile)

    @pl.when(t == pl.num_programs(1) - 1)
    def _():
        for gi in range(SSD_GROUPS):
            sout_ref[0, gi * SSD_HPG:(gi + 1) * SSD_HPG] = (
                st_ref[gi].T.reshape(SSD_HPG, SSD_HEADDIM, SSD_STATE))
        cout_ref[0] = jnp.zeros(cout_ref.shape[1:], F32)
        for v in range(SSD_CONV - 1):
            r = (v + 1) * SUBLANES - 1
            cout_ref[0, first_buffered + v:first_buffered + v + 1, :] = cbuf_ref[r:r + 1, :]


def _ssd_weight_list(w, step):
    names = ("pre", "post", "win", "cw", "cb", "dtb", "alog", "dskip", "wn", "wo")
    consts = ("expand", "expand_n") if step else ("expand",)
    return [(w[k], True) for k in names] + [(w[k], False) for k in consts]


def _ssd_prompt_layer(x, s0, c0, prev_out, layer, w, *, n_pad, tile):
    bsz, seq, _ = x.shape
    s_tail = (SSD_HEADS, SSD_HEADDIM, SSD_STATE)
    c_tail = (SUBLANES, SSD_CONV_DIM)
    own = s0.shape[1] == bsz
    weights = _ssd_weight_list(w, step=False)
    carried = [] if prev_out is None else list(prev_out)
    n_in = D_SLABS + 2 + len(weights)
    x_slabs = [pl.BlockSpec((1, tile, LANES), lambda b, t, j=j: (b, t, j)) for j in range(D_SLABS)]
    return pl.pallas_call(
        functools.partial(_ssd_prompt_kernel, n_pad=n_pad, tile=tile, has_prev=bool(carried)),
        grid=(bsz, seq // tile),
        in_specs=x_slabs
                 + [_batch_block(layer, 1, s_tail, own), _batch_block(layer, 1, c_tail, own)]
                 + _weight_specs(weights, layer)
                 + [pl.BlockSpec(memory_space=pl.ANY)] * len(carried),
        out_specs=[pl.BlockSpec((1, tile, D_MODEL), lambda b, t: (b, t, 0)),
                   _batch_block(layer, 1, s_tail), _batch_block(layer, 1, c_tail)],
        out_shape=[jax.ShapeDtypeStruct(x.shape, F32),
                   jax.ShapeDtypeStruct((N_LAYERS_PER_MIXER, bsz) + s_tail, F32),
                   jax.ShapeDtypeStruct((N_LAYERS_PER_MIXER, bsz) + c_tail, F32)],
        scratch_shapes=[pltpu.VMEM((SSD_GROUPS, SSD_STATE, SSD_GW), F32),
                        pltpu.VMEM((WRAP_ROWS, SSD_CONV_DIM), F32),
                        pltpu.VMEM((tile, SSD_DINNER), F32),
                        pltpu.VMEM((D_SLABS, tile, LANES), F32)],
        input_output_aliases={n_in: 1, n_in + 1: 2} if carried else {},
        compiler_params=pltpu.CompilerParams(
            dimension_semantics=("arbitrary", "arbitrary"), vmem_limit_bytes=VMEM_LIMIT),
        name="ssd_prompt",
    )(*([x] * D_SLABS), s0, c0, *[a for a, _ in weights], *carried)


def _column(row, width, lane0=0):
    seg = row[:, lane0:lane0 + width]
    r = lax.broadcasted_iota(jnp.int32, (width, width), 0)
    c = lax.broadcasted_iota(jnp.int32, (width, width), 1)
    return jnp.sum(jnp.where(r == c, seg, 0.0), axis=1, keepdims=True)


def _gla_step_kernel(x_ref, s_ref, prew_ref, postw_ref, win_ref, wgu_ref, bg_ref, wn_ref, wo_ref,
                     *rest, bb, has_prev):
    y_ref, sout_ref, q_s, kl_s, v_s, e_s, o_s = rest[1:] if has_prev else rest
    i = pl.program_id(0)
    bsz = x_ref.shape[0]

    @pl.when(i == 0)
    def _():
        hb = _rms(x_ref[...], prew_ref[...]).astype(BF16)
        q_s[...] = _dot(hb, win_ref[:, GLA_Q0:GLA_K0]) * (GLA_HK ** -0.5)
        o_s[:, 0:GLA_DK] = _dot(hb, win_ref[:, GLA_K0:GLA_V0])
        k_t = o_s[:, 0:GLA_DK].T
        for h in range(GLA_HEADS):
            kl_s[h // 2, :, (h % 2) * bsz:(h % 2 + 1) * bsz] = (
                k_t[h * GLA_HK:(h + 1) * GLA_HK].astype(BF16))
        v_s[...] = _dot(hb, win_ref[:, GLA_V0:GLA_R0])
        e_s[...] = jnp.exp(_gla_gates(hb, win_ref, wgu_ref, bg_ref))

    sub = lax.broadcasted_iota(jnp.int32, (bsz, GLA_HV), 0)
    zero_bv = jnp.zeros((bsz, GLA_HV), BF16)

    for j in range(bb):
        row = i * bb + j
        qr = q_s[pl.ds(row, 1), :]
        er = e_s[pl.ds(row, 1), :]
        for p in range(GLA_HEADS // 2):
            vsel = [jnp.where(sub == row, v_s[:, (2 * p + e) * GLA_HV:(2 * p + e + 1) * GLA_HV],
                              0.0).astype(BF16) for e in range(2)]
            rhs = jnp.concatenate([jnp.concatenate([vsel[0], zero_bv], axis=1),
                                   jnp.concatenate([zero_bv, vsel[1]], axis=1)], axis=0)
            kv = _dot(kl_s[p], rhs)
            for e in range(2):
                h = 2 * p + e
                vs = slice(h * GLA_HV, (h + 1) * GLA_HV)
                ecol = _column(er, GLA_HK, h * GLA_HK)
                qcol = _column(qr, GLA_HK, h * GLA_HK)
                s_new = s_ref[j, h] * ecol + kv[:, e * GLA_HV:(e + 1) * GLA_HV]
                sout_ref[j, h] = s_new
                o_s[pl.ds(row, 1), vs] = jnp.sum(qcol * s_new, axis=0, keepdims=True)

    @pl.when(i == pl.num_programs(0) - 1)
    def _():
        x = x_ref[...]
        hb = _rms(x, prew_ref[...]).astype(BF16)
        r = _dot(hb, win_ref[:, GLA_R0:GLA_G0])
        y_ref[...] = _gla_tail(o_s[...], r, wn_ref[...], wo_ref[...], postw_ref[...], x)


def _gla_step_layer(x, states, prev_out, layer, w, *, bb=8):
    bsz = x.shape[0]
    weights = _gla_weight_list(w)
    sblk = _batch_block(layer, bb, (GLA_HEADS, GLA_HK, GLA_HV))
    carried = [] if prev_out is None else [prev_out]
    return pl.pallas_call(
        functools.partial(_gla_step_kernel, bb=bb, has_prev=bool(carried)),
        grid=(bsz // bb,),
        in_specs=[_full(x.shape), sblk] + _weight_specs(weights, layer)
                 + [pl.BlockSpec(memory_space=pl.ANY)] * len(carried),
        out_specs=[pl.BlockSpec(x.shape, lambda i: (0, 0)), sblk],
        out_shape=[jax.ShapeDtypeStruct(x.shape, F32), jax.ShapeDtypeStruct(states.shape, F32)],
        scratch_shapes=[pltpu.VMEM((bsz, GLA_DK), F32),
                        pltpu.VMEM((GLA_HEADS // 2, GLA_HK, 2 * bsz), BF16),
                        pltpu.VMEM((bsz, GLA_DV), F32),
                        pltpu.VMEM((bsz, GLA_DK), F32),
                        pltpu.VMEM((bsz, GLA_DV), F32)],
        input_output_aliases={2 + len(weights): 1} if carried else {},
        compiler_params=pltpu.CompilerParams(
            dimension_semantics=("arbitrary",), vmem_limit_bytes=VMEM_LIMIT),
        name="gla_step",
    )(x, states, *[a for a, _ in weights], *carried)


def _ssd_step_kernel(x_ref, s_ref, cv_ref, prew_ref, postw_ref, win_ref, cw_ref, cb_ref, dtb_ref,
                     alog_ref, dskip_ref, wn_ref, wo_ref, expand_ref, expand_n_ref, *rest,
                     bb, has_prev):
    (y_ref, sout_ref, cvout_ref, xs_s, xl_s, ct_s, b_s, e_s, yt_s) = rest[2:] if has_prev else rest
    i = pl.program_id(0)
    bsz = x_ref.shape[0]
    cd = SSD_CONV_DIM

    @pl.when(i == 0)
    def _():
        hb = _rms(x_ref[...], prew_ref[...]).astype(BF16)
        xbc = _dot(hb, win_ref[:, SSD_X0:SSD_DT0])
        dtr = _narrow_proj(hb, win_ref, SSD_DT0, SSD_HEADS)
        cw = cw_ref[...]
        conv = cb_ref[...] + xbc * cw[3:4, :]
        for t in range(SSD_CONV - 1):
            conv = conv + cv_ref[:, t * cd:(t + 1) * cd] * cw[t:t + 1, :]
        cvout_ref[:, 0:cd] = cv_ref[:, cd:2 * cd]
        cvout_ref[:, cd:2 * cd] = cv_ref[:, 2 * cd:3 * cd]
        cvout_ref[:, 2 * cd:3 * cd] = xbc
        xc = _silu(conv)
        xs = xc[:, :SSD_DINNER]
        dt = _softplus(dtr + dtb_ref[...])
        ea = jnp.exp(dt * (-jnp.exp(alog_ref[...])))
        xs_s[...] = xs
        xd_t = (xs * _dot_exact_rhs(dt, expand_ref[...])).T
        c_t = xc[:, SSD_DINNER + SSD_GROUPS * SSD_STATE:].T
        for p in range(SSD_GROUPS // 2):
            for e in range(2):
                gi = 2 * p + e
                xl_s[p, :, e * bsz:(e + 1) * bsz] = (
                    xd_t[gi * SSD_GW:(gi + 1) * SSD_GW].astype(BF16))
        ct_s[...] = c_t
        b_s[...] = xc[:, SSD_DINNER:SSD_DINNER + SSD_GROUPS * SSD_STATE]
        e_s[...] = _dot_exact_rhs(ea, expand_n_ref[...])
        yt_s[...] = jnp.zeros(yt_s.shape, F32)

    sub = lax.broadcasted_iota(jnp.int32, (bsz, SSD_STATE), 0)
    lane = lax.broadcasted_iota(jnp.int32, (SSD_STATE, bsz), 1)
    zero_bn = jnp.zeros((bsz, SSD_STATE), BF16)
    zero_nb = jnp.zeros((SSD_STATE, bsz), BF16)

    for p in range(SSD_GROUPS // 2):
        y_acc = None
        for j in range(bb):
            row = i * bb + j
            er = e_s[pl.ds(row, 1), :]
            bsel, csel = [], []
            for e in range(2):
                ns = slice((2 * p + e) * SSD_STATE, (2 * p + e + 1) * SSD_STATE)
                bsel.append(jnp.where(sub == row, b_s[:, ns], 0.0).astype(BF16))
                csel.append(jnp.where(lane == row, ct_s[ns, :], 0.0).astype(BF16))
            rhs_b = jnp.concatenate([jnp.concatenate([bsel[0], zero_bn], axis=1),
                                     jnp.concatenate([zero_bn, bsel[1]], axis=1)], axis=0)
            rhs_c = jnp.concatenate([jnp.concatenate([csel[0], zero_nb], axis=1),
                                     jnp.concatenate([zero_nb, csel[1]], axis=1)], axis=0)
            outer = _dot(xl_s[p], rhs_b)
            halves = []
            for e in range(2):
                pieces = []
                for r in range(SSD_HPG):
                    h = (2 * p + e) * SSD_HPG + r
                    s_new = (s_ref[j, h] * er[:, h * SSD_STATE:(h + 1) * SSD_STATE]
                             + outer[r * SSD_HEADDIM:(r + 1) * SSD_HEADDIM,
                                     e * SSD_STATE:(e + 1) * SSD_STATE])
                    sout_ref[j, h] = s_new
                    pieces.append(s_new.astype(BF16))
                halves.append(jnp.concatenate(pieces, axis=0))
            y_j = _dot(jnp.concatenate(halves, axis=1), rhs_c)
            y_acc = y_j if y_acc is None else y_acc + y_j
        yt_s[p] += y_acc

    @pl.when(i == pl.num_programs(0) - 1)
    def _():
        x = x_ref[...]
        hb = _rms(x, prew_ref[...]).astype(BF16)
        z = _dot(hb, win_ref[:, SSD_Z0:SSD_X0])
        y = jnp.concatenate([yt_s[p, :, e * bsz:(e + 1) * bsz].T
                             for p in range(SSD_GROUPS // 2) for e in range(2)], axis=1)
        y = y + xs_s[...] * dskip_ref[...]
        y_ref[...] = _ssd_tail(y, z, wn_ref[...], wo_ref[...], postw_ref[...], x)


def _ssd_step_layer(x, states, convs, prev_out, layer, w, *, bb=4):
    bsz = x.shape[0]
    weights = _ssd_weight_list(w, step=True)
    sblk = _batch_block(layer, bb, (SSD_HEADS, SSD_HEADDIM, SSD_STATE))
    cblk = pl.BlockSpec((None,) + convs.shape[1:], lambda i: (layer, 0, 0))
    carried = [] if prev_out is None else list(prev_out)
    n_in = 3 + len(weights)
    return pl.pallas_call(
        functools.partial(_ssd_step_kernel, bb=bb, has_prev=bool(carried)),
        grid=(bsz // bb,),
        in_specs=[_full(x.shape), sblk,
                  pl.BlockSpec(cblk.block_shape, cblk.index_map, pipeline_mode=pl.Buffered(1))]
                 + _weight_specs(weights, layer)
                 + [pl.BlockSpec(memory_space=pl.ANY)] * len(carried),
        out_specs=[pl.BlockSpec(x.shape, lambda i: (0, 0)), sblk, cblk],
        out_shape=[jax.ShapeDtypeStruct(x.shape, F32), jax.ShapeDtypeStruct(states.shape, F32),
                   jax.ShapeDtypeStruct(convs.shape, F32)],
        input_output_aliases={n_in: 1, n_in + 1: 2} if carried else {},
        scratch_shapes=[pltpu.VMEM((bsz, SSD_DINNER), F32),
                        pltpu.VMEM((SSD_GROUPS // 2, SSD_GW, 2 * bsz), BF16),
                        pltpu.VMEM((SSD_GROUPS * SSD_STATE, bsz), F32),
                        pltpu.VMEM((bsz, SSD_GROUPS * SSD_STATE), F32),
                        pltpu.VMEM((bsz, SSD_HEADS * SSD_STATE), F32),
                        pltpu.VMEM((SSD_GROUPS // 2, SSD_GW, 2 * bsz), F32)],
        compiler_params=pltpu.CompilerParams(
            dimension_semantics=("arbitrary",), vmem_limit_bytes=VMEM_LIMIT),
        name="ssd_step",
    )(x, states, convs, *[a for a, _ in weights], *carried)


def _pad_last(a, n):
    return jnp.pad(a, [(0, 0)] * (a.ndim - 1) + [(0, n - a.shape[-1])])


def _rows(a):
    return a[:, None, :]


CAST_ROWS = 256


def _cast_kernel(x_ref, o_ref):
    o_ref[...] = x_ref[...].astype(o_ref.dtype)


def _to_bf16(w):
    layers, rows, cols = w.shape
    blk = pl.BlockSpec((1, CAST_ROWS, cols), lambda l, r: (l, r, 0))
    return pl.pallas_call(
        _cast_kernel, grid=(layers, rows // CAST_ROWS), in_specs=[blk], out_specs=blk,
        out_shape=jax.ShapeDtypeStruct(w.shape, BF16),
        compiler_params=pltpu.CompilerParams(dimension_semantics=("arbitrary", "arbitrary")),
        name="cast_bf16",
    )(w)


def _gla_weights(pre, post, w_in, w_gate_up, b_gate, w_norm, w_out):
    return {
        "pre": _rows(pre), "post": _rows(post),
        "win": _to_bf16(w_in),
        "wgu": jnp.pad(w_gate_up, ((0, 0), (0, LANES - GLA_RANK), (0, 0))).astype(BF16),
        "bg": _rows(b_gate), "wn": _rows(w_norm), "wo": w_out.astype(BF16),
    }


def _ssd_weights(pre, post, w_in, conv_w, conv_b, dt_bias, a_log, d_skip, w_norm, w_out):
    head_of_lane = jnp.arange(SSD_DINNER) // SSD_HEADDIM
    head_of_lane_n = jnp.arange(SSD_HEADS * SSD_STATE) // SSD_STATE
    heads = jnp.arange(LANES)[:, None]
    return {
        "pre": _rows(pre), "post": _rows(post),
        "win": _to_bf16(w_in),
        "cw": jnp.pad(conv_w, ((0, 0), (0, SUBLANES - SSD_CONV), (0, 0))), "cb": _rows(conv_b),
        "dtb": _rows(_pad_last(dt_bias, LANES)), "alog": _rows(_pad_last(a_log, LANES)),
        "dskip": _rows(jnp.repeat(d_skip, SSD_HEADDIM, axis=1)),
        "wn": _rows(w_norm), "wo": w_out.astype(BF16),
        "expand": (heads == head_of_lane[None, :]).astype(BF16),
        "expand_n": (heads == head_of_lane_n[None, :]).astype(BF16),
    }


def _prompt_trunk(x, gla0, ssm0, conv0, gw, sw, *, n_pad, gla_tile, ssd_tile):
    gla_s = ssm_conv_s = None
    for j in range(N_LAYERS_PER_MIXER):
        x, gla_s = _gla_prompt_layer(x, gla0, gla_s, j, gw, n_pad=n_pad, tile=gla_tile)
        x, *ssm_conv_s = _ssd_prompt_layer(x, ssm0, conv0, ssm_conv_s, j, sw,
                                           n_pad=n_pad, tile=ssd_tile)
    return x, gla_s, ssm_conv_s[0], ssm_conv_s[1]


def kernel(x_prompt, x_sample, state_gla, state_ssm, state_conv, meta_tokens, pre_norm, post_norm,
           gla_w_in, gla_w_gate_up, gla_b_gate, gla_w_norm, gla_w_out,
           ssd_w_in, ssd_conv_w, ssd_conv_b, ssd_dt_bias, ssd_a_log, ssd_d_skip, ssd_w_norm, ssd_w_out):
    gw = _gla_weights(pre_norm[0::2], post_norm[0::2], gla_w_in, gla_w_gate_up, gla_b_gate,
                      gla_w_norm, gla_w_out)
    sw = _ssd_weights(pre_norm[1::2], post_norm[1::2], ssd_w_in, ssd_conv_w, ssd_conv_b,
                      ssd_dt_bias, ssd_a_log, ssd_d_skip, ssd_w_norm, ssd_w_out)

    n_pad = CHUNK - N_META
    x_meta = jnp.pad(meta_tokens.astype(F32), ((n_pad, 0), (0, 0)))[None]
    zg = jnp.zeros((N_LAYERS_PER_MIXER, 1, GLA_HEADS, GLA_HK, GLA_HV), F32)
    zs = jnp.zeros((N_LAYERS_PER_MIXER, 1, SSD_HEADS, SSD_HEADDIM, SSD_STATE), F32)
    zc = jnp.zeros((N_LAYERS_PER_MIXER, 1, SUBLANES, SSD_CONV_DIM), F32)
    _, mg, ms, mc = _prompt_trunk(x_meta, zg, zs, zc, gw, sw,
                                  n_pad=n_pad, gla_tile=CHUNK, ssd_tile=CHUNK)
    y_prompt, gla_p, ssm_p, conv_p = _prompt_trunk(x_prompt, mg, ms, mc, gw, sw, n_pad=0,
                                                   gla_tile=GLA_TILE, ssd_tile=SSD_TILE)
    conv_p = conv_p[:, :, SUBLANES - (SSD_CONV - 1):, :]

    xs = x_sample[:, 0, :]
    sbsz = xs.shape[0]
    convs = state_conv.reshape(N_LAYERS_PER_MIXER, sbsz, (SSD_CONV - 1) * SSD_CONV_DIM)
    gla_s = ssm_conv_s = None
    for j in range(N_LAYERS_PER_MIXER):
        xs, gla_s = _gla_step_layer(xs, state_gla, gla_s, j, gw)
        xs, *ssm_conv_s = _ssd_step_layer(xs, state_ssm, convs, ssm_conv_s, j, sw)
    ssm_s, conv_s = ssm_conv_s
    y_sample = xs[:, None, :]
    return (y_prompt, y_sample, gla_p, ssm_p, conv_p, gla_s, ssm_s,
            conv_s.reshape(N_LAYERS_PER_MIXER, sbsz, SSD_CONV - 1, SSD_CONV_DIM))
```

```python
import functools

import jax
import jax.numpy as jnp
from jax import lax
from jax.experimental import pallas as pl
from jax.experimental.pallas import tpu as pltpu

F32 = jnp.float32
BF16 = jnp.bfloat16

D_MODEL = 1024
N_META = 16
NORM_EPS = 1e-6
N_LAYERS_PER_MIXER = 2

GLA_HEADS = 4
GLA_DK = 512
GLA_DV = 1024
GLA_HK = 128
GLA_HV = 256
GLA_RANK = 16
GLA_TAU = 16.0
GLA_SAFE_DROP = 40.0

SSD_DINNER = 2048
SSD_HEADDIM = 64
SSD_HEADS = 32
SSD_GROUPS = 4
SSD_HPG = 8
SSD_STATE = 128
SSD_CONV = 4
SSD_CONV_DIM = 3072
SSD_GW = SSD_HPG * SSD_HEADDIM

LANES = 128
SUBLANES = 8
CHUNK = 128
GLA_TILE = 1024
SSD_TILE = 512
VMEM_LIMIT = 56 * 1024 * 1024

GLA_Q0, GLA_K0, GLA_V0, GLA_R0, GLA_G0 = 0, GLA_DK, 2 * GLA_DK, 2 * GLA_DK + GLA_DV, 2 * GLA_DK + 2 * GLA_DV
SSD_Z0, SSD_X0, SSD_DT0 = 0, SSD_DINNER, SSD_DINNER + SSD_CONV_DIM


def _narrow_proj(hb, w_ref, start, width):
    res = jnp.dot(hb, w_ref[:, start:start + width], preferred_element_type=F32)
    return jnp.concatenate([res, jnp.zeros((res.shape[0], LANES - width), F32)], axis=1)


def _dot(a, b):
    return jnp.dot(a, b, preferred_element_type=F32)


def _dot_nt(a, b):
    return lax.dot_general(a, b, (((1,), (1,)), ((), ())), preferred_element_type=F32)


def _dot_tn(a, b):
    return lax.dot_general(a, b, (((0,), (0,)), ((), ())), preferred_element_type=F32)


def _rms(x, w):
    return x * lax.rsqrt(jnp.mean(x * x, axis=-1, keepdims=True) + NORM_EPS) * w


def _silu(x):
    h = 0.5 * x
    return h + h * jnp.tanh(h)


def _softplus(x):
    return jnp.maximum(x, 0.0) + jnp.log1p(jnp.exp(-jnp.abs(x)))


def _log_sigmoid(x):
    return -_softplus(-x)


def _tri(n):
    r = lax.broadcasted_iota(jnp.int32, (n, n), 0)
    c = lax.broadcasted_iota(jnp.int32, (n, n), 1)
    return r >= c


def _split3(x):
    hi = x.astype(BF16)
    r1 = x - hi.astype(F32)
    mid = r1.astype(BF16)
    lo = (r1 - mid.astype(F32)).astype(BF16)
    return hi, mid, lo


def _dot_exact_rhs(x, m_bf16):
    hi, mid, lo = _split3(x)
    return _dot(hi, m_bf16) + _dot(mid, m_bf16) + _dot(lo, m_bf16)


def _cumsum_rows(tril_bf16, x):
    hi, mid, lo = _split3(x)
    if x.shape[1] < 2 * LANES:
        return _dot(tril_bf16, hi) + _dot(tril_bf16, mid) + _dot(tril_bf16, lo)
    two = jnp.concatenate([tril_bf16, tril_bf16], axis=1)
    return _dot(two, jnp.concatenate([hi, mid], axis=0)) + _dot(tril_bf16, lo)


def _full(shape):
    nd = len(shape)
    return pl.BlockSpec(shape, lambda *_: (0,) * nd, pipeline_mode=pl.Buffered(1))


def _of_layer(a, layer):
    nz = a.ndim - 1
    return pl.BlockSpec((None,) + a.shape[1:], lambda *_: (layer,) + (0,) * nz,
                        pipeline_mode=pl.Buffered(1))


def _weight_specs(weights, layer):
    return [_of_layer(a, layer) if stacked else _full(a.shape) for a, stacked in weights]


def _batch_block(layer, bb, tail, own=True):
    nz = len(tail)
    return pl.BlockSpec((None, bb) + tail,
                        lambda i, *_: (layer, i if own else 0) + (0,) * nz)


def _gla_attn_pairwise(qf, kf, bc, causal, kbuf_ref, bbuf_ref):
    n = qf.shape[0]
    kbuf_ref[...] = kf
    bbuf_ref[...] = bc
    lane = lax.broadcasted_iota(jnp.int32, (n, n), 1)

    def body(j, acc):
        kj = kbuf_ref[pl.ds(j, 1), :]
        bj = bbuf_ref[pl.ds(j, 1), :]
        col = jnp.sum(qf * jnp.exp(jnp.minimum(bc - bj, 0.0)) * kj, axis=1, keepdims=True)
        return jnp.where(lane == j, col, acc)

    acc = lax.fori_loop(0, n, body, jnp.zeros((n, n), F32))
    return jnp.where(causal, acc, 0.0)


def _gla_scan_tile(q, k, v, g, states, pairwise, kbuf_ref, bbuf_ref):
    causal = _tri(CHUNK)
    tril = causal.astype(BF16)
    states = list(states)
    outs = []
    for c in range(q.shape[0] // CHUNK):
        rows = slice(c * CHUNK, (c + 1) * CHUNK)
        bc = _cumsum_rows(tril, g[rows])
        b_last = bc[-1:, :]
        e_last = jnp.exp(b_last)
        qh = (q[rows] * jnp.exp(bc)).astype(BF16)
        if pairwise:
            kd = (k[rows] * jnp.exp(b_last - bc)).astype(BF16)
        else:
            kh = k[rows] * jnp.exp(-bc)
            kd = (kh * e_last).astype(BF16)
            kh = kh.astype(BF16)
        vb = v[rows].astype(BF16)
        zk = jnp.zeros((CHUNK, GLA_HK), BF16)
        parts = []
        for h0 in range(0, GLA_HEADS, 2):
            ks = [slice((h0 + e) * GLA_HK, (h0 + e + 1) * GLA_HK) for e in range(2)]
            vs = [slice((h0 + e) * GLA_HV, (h0 + e + 1) * GLA_HV) for e in range(2)]
            ks2 = slice(ks[0].start, ks[1].stop)
            if pairwise:
                attn = [_gla_attn_pairwise(q[rows, ks[e]], k[rows, ks[e]], bc[:, ks[e]], causal,
                                           kbuf_ref, bbuf_ref) for e in range(2)]
            else:
                k_bd = jnp.concatenate([jnp.concatenate([kh[:, ks[0]], zk], axis=1),
                                        jnp.concatenate([zk, kh[:, ks[1]]], axis=1)], axis=0)
                scores = _dot_nt(qh[:, ks2], k_bd)
                attn = [jnp.where(causal, scores[:, e * CHUNK:(e + 1) * CHUNK], 0.0)
                        for e in range(2)]
            kd_bd = jnp.concatenate([jnp.concatenate([kd[:, ks[0]], zk], axis=1),
                                     jnp.concatenate([zk, kd[:, ks[1]]], axis=1)], axis=0)
            upd = _dot_tn(jnp.concatenate([vb[:, vs[0]], vb[:, vs[1]]], axis=0), kd_bd)
            for e in range(2):
                st = states[h0 + e]
                parts.append(_dot(attn[e].astype(BF16), vb[:, vs[e]])
                             + _dot_nt(qh[:, ks[e]], st.astype(BF16)))
                states[h0 + e] = st * e_last[:, ks[e]] + upd[:, e * GLA_HK:(e + 1) * GLA_HK]
        outs.append(jnp.concatenate(parts, axis=1))
    o = jnp.concatenate(outs, axis=0) if len(outs) > 1 else outs[0]
    return o, states


TAIL_ROWS = 256


def _by_row_blocks(fn, *row_args):
    n = row_args[0].shape[0]
    if n <= TAIL_ROWS:
        return fn(*row_args)
    return jnp.concatenate([fn(*[a[i:i + TAIL_ROWS] for a in row_args])
                            for i in range(0, n, TAIL_ROWS)], axis=0)


def _gla_tail(o, r, wn, wo, postw, x):
    return _by_row_blocks(lambda o, r, x: _gla_tail_rows(o, r, wn, wo, postw, x), o, r, x)


def _gla_tail_rows(o, r, wn, wo, postw, x):
    parts = []
    for h in range(GLA_HEADS):
        vs = slice(h * GLA_HV, (h + 1) * GLA_HV)
        parts.append(_rms(o[:, vs], wn[:, vs]))
    on = jnp.concatenate(parts, axis=1) * _silu(r)
    y = _dot(on.astype(BF16), wo)
    return x + _rms(y, postw)


def _gla_gates(hb, win_ref, wgu_ref, bg_ref):
    gl = _narrow_proj(hb, win_ref, GLA_G0, GLA_RANK)
    return _log_sigmoid(_dot(gl.astype(BF16), wgu_ref[...]) + bg_ref[...]) * (1.0 / GLA_TAU)


def _gla_prompt_kernel(x_ref, s0_ref, prew_ref, postw_ref, win_ref, wgu_ref, bg_ref, wn_ref,
                       wo_ref, *rest, n_pad, tile, has_prev):
    y_ref, sout_ref, st_ref, o_ref, kbuf_ref, bbuf_ref = rest[1:] if has_prev else rest
    t = pl.program_id(1)

    @pl.when(t == 0)
    def _():
        for h in range(GLA_HEADS):
            st_ref[h] = s0_ref[0, h].T

    x = x_ref[0]
    hn = _rms(x, prew_ref[...])
    if n_pad:
        row = lax.broadcasted_iota(jnp.int32, (tile, 1), 0) + t * tile
        hn = jnp.where(row >= n_pad, hn, 0.0)
    hb = hn.astype(BF16)
    q = _dot(hb, win_ref[:, GLA_Q0:GLA_K0]) * (GLA_HK ** -0.5)
    k = _dot(hb, win_ref[:, GLA_K0:GLA_V0])
    v = _dot(hb, win_ref[:, GLA_V0:GLA_R0])
    g = _gla_gates(hb, win_ref, wgu_ref, bg_ref)

    drop = None
    for c in range(tile // CHUNK):
        tot = jnp.sum(g[c * CHUNK:(c + 1) * CHUNK], axis=0, keepdims=True)
        drop = tot if drop is None else jnp.minimum(drop, tot)
    safe = jnp.min(drop) >= -GLA_SAFE_DROP

    def scan(pairwise):
        o, states = _gla_scan_tile(q, k, v, g, [st_ref[h] for h in range(GLA_HEADS)], pairwise,
                                   kbuf_ref, bbuf_ref)
        o_ref[...] = o
        for h in range(GLA_HEADS):
            st_ref[h] = states[h]

    pl.when(safe)(functools.partial(scan, False))
    pl.when(jnp.logical_not(safe))(functools.partial(scan, True))
    r = _dot(hb, win_ref[:, GLA_R0:GLA_G0])
    y_ref[0] = _gla_tail(o_ref[...], r, wn_ref[...], wo_ref[...], postw_ref[...], x)

    @pl.when(t == pl.num_programs(1) - 1)
    def _():
        for h in range(GLA_HEADS):
            sout_ref[0, h] = st_ref[h].T


def _gla_weight_list(w):
    return [(w[k], True) for k in ("pre", "post", "win", "wgu", "bg", "wn", "wo")]


def _gla_prompt_layer(x, s0, prev_out, layer, w, *, n_pad, tile):
    bsz, seq, _ = x.shape
    tail = (GLA_HEADS, GLA_HK, GLA_HV)
    weights = _gla_weight_list(w)
    carried = [] if prev_out is None else [prev_out]
    return pl.pallas_call(
        functools.partial(_gla_prompt_kernel, n_pad=n_pad, tile=tile, has_prev=bool(carried)),
        grid=(bsz, seq // tile),
        in_specs=[pl.BlockSpec((1, tile, D_MODEL), lambda b, t: (b, t, 0)),
                  _batch_block(layer, 1, tail, own=s0.shape[1] == bsz)]
                 + _weight_specs(weights, layer)
                 + [pl.BlockSpec(memory_space=pl.ANY)] * len(carried),
        out_specs=[pl.BlockSpec((1, tile, D_MODEL), lambda b, t: (b, t, 0)),
                   _batch_block(layer, 1, tail)],
        out_shape=[jax.ShapeDtypeStruct(x.shape, F32),
                   jax.ShapeDtypeStruct((N_LAYERS_PER_MIXER, bsz) + tail, F32)],
        scratch_shapes=[pltpu.VMEM((GLA_HEADS, GLA_HV, GLA_HK), F32),
                        pltpu.VMEM((tile, GLA_DV), F32),
                        pltpu.VMEM((CHUNK, GLA_HK), F32),
                        pltpu.VMEM((CHUNK, GLA_HK), F32)],
        input_output_aliases={2 + len(weights): 1} if carried else {},
        compiler_params=pltpu.CompilerParams(
            dimension_semantics=("arbitrary", "arbitrary"), vmem_limit_bytes=VMEM_LIMIT),
        name="gla_prompt",
    )(x, s0, *[a for a, _ in weights], *carried)


UNIT_GROUPS = 4
UNIT_ROWS = UNIT_GROUPS * SUBLANES
WRAP_ROWS = (SSD_CONV - 1) * SUBLANES
D_SLABS = D_MODEL // LANES
assert SSD_CONV - 1 < UNIT_GROUPS and CHUNK % UNIT_ROWS == 0


def _position_time(n):
    p = lax.iota(jnp.int32, n)
    q = p % UNIT_ROWS
    return (p - q) + (q % SUBLANES) * UNIT_GROUPS + q // SUBLANES


def _load_interleaved(slab_refs, tile):
    groups = []
    for u in range(tile // UNIT_ROWS):
        for g in range(UNIT_GROUPS):
            rows = pl.ds(u * UNIT_ROWS + g, SUBLANES, stride=UNIT_GROUPS)
            groups.append(jnp.concatenate([r[0, rows, :] for r in slab_refs], axis=1))
    return jnp.concatenate(groups, axis=0)


def _store_interleaved(y_ref, slab_ref, y, tile):
    for u in range(tile // UNIT_ROWS):
        for g in range(UNIT_GROUPS):
            r0 = u * UNIT_ROWS + g * SUBLANES
            rows = pl.ds(u * UNIT_ROWS + g, SUBLANES, stride=UNIT_GROUPS)
            for j in range(D_SLABS):
                slab_ref[j, rows, :] = y[r0:r0 + SUBLANES, j * LANES:(j + 1) * LANES]
    y_ref[0] = jnp.concatenate([slab_ref[j] for j in range(D_SLABS)], axis=1)


def _ssd_conv(xbc, cbuf_ref, cw, cb, tile):
    sub = lax.broadcasted_iota(jnp.int32, (SUBLANES, 1), 0)
    outs = []
    prev = cbuf_ref[...]
    for u in range(tile // UNIT_ROWS):
        xb = xbc[u * UNIT_ROWS:(u + 1) * UNIT_ROWS]
        last = xb[UNIT_ROWS - WRAP_ROWS:]
        wrapped = []
        for v in range(SSD_CONV - 1):
            rs = slice(v * SUBLANES, (v + 1) * SUBLANES)
            mix = jnp.where(sub == SUBLANES - 1, prev[rs], last[rs])
            wrapped.append(pltpu.roll(mix, 1, axis=0))
        ext = jnp.concatenate(wrapped + [xb], axis=0)
        conv = cb + xb * cw[SSD_CONV - 1:SSD_CONV, :]
        for i in range(SSD_CONV - 1):
            conv = conv + ext[i * SUBLANES:i * SUBLANES + UNIT_ROWS] * cw[i:i + 1, :]
        outs.append(conv)
        prev = last
    cbuf_ref[...] = prev
    return _silu(jnp.concatenate(outs, axis=0))


def _ssd_scan_chunk(xs, bm, cm, dt, a, expand, st_ref, y_ref, row0, causal, tril, lo_half):
    ac = _cumsum_rows(tril, a)
    ac_t = ac.T
    dt_t = dt.T
    a_last = ac[-1:, :]
    wgt = jnp.exp(a_last - ac) * dt
    xw = (xs * _dot(wgt.astype(BF16), expand)).astype(BF16)
    pair = 2 * SSD_HEADDIM
    zn = jnp.zeros((CHUNK, SSD_STATE), BF16)
    cbs = []
    for gp in range(0, SSD_GROUPS, 2):
        n2 = slice(gp * SSD_STATE, (gp + 2) * SSD_STATE)
        b0 = bm[:, gp * SSD_STATE:(gp + 1) * SSD_STATE].astype(BF16)
        b1 = bm[:, (gp + 1) * SSD_STATE:(gp + 2) * SSD_STATE].astype(BF16)
        b_bd = jnp.concatenate([jnp.concatenate([b0, zn], axis=1),
                                jnp.concatenate([zn, b1], axis=1)], axis=0)
        both = _dot_nt(cm[:, n2].astype(BF16), b_bd)
        cbs += [both[:, :CHUNK], both[:, CHUNK:]]
    for gi in range(SSD_GROUPS):
        ns = slice(gi * SSD_STATE, (gi + 1) * SSD_STATE)
        gs = slice(gi * SSD_GW, (gi + 1) * SSD_GW)
        bg = bm[:, ns].astype(BF16)
        cg = cm[:, ns].astype(BF16)
        cb = cbs[gi]
        st = st_ref[gi]
        y_int = _dot(cg, st.astype(BF16))
        e_last = []
        for qi in range(SSD_HPG // 2):
            h0 = gi * SSD_HPG + 2 * qi
            ps = slice(h0 * SSD_HEADDIM, h0 * SSD_HEADDIM + pair)
            ms, eas = [], []
            for e in range(2):
                h = h0 + e
                col = jnp.broadcast_to(ac[:, h:h + 1], (CHUNK, CHUNK))
                seg = col - ac_t[h:h + 1, :]
                ms.append(jnp.where(causal, cb * jnp.exp(seg) * dt_t[h:h + 1, :], 0.0).astype(BF16))
                eas.append(jnp.exp(col))
            xp = xs[:, ps]
            rhs = jnp.concatenate([jnp.where(lo_half, xp, 0.0).astype(BF16),
                                   jnp.where(lo_half, 0.0, xp).astype(BF16)], axis=0)
            ea = jnp.where(lo_half, eas[0], eas[1])
            e_last.append(ea[CHUNK - 1:CHUNK, :])
            y_ref[row0:row0 + CHUNK, ps] = (_dot(jnp.concatenate(ms, axis=1), rhs)
                                            + y_int[:, qi * pair:(qi + 1) * pair] * ea)
        st_ref[gi] = st * jnp.concatenate(e_last, axis=1) + _dot_tn(bg, xw[:, gs])


def _ssd_tail(y, z, wn, wo, postw, x):
    return _by_row_blocks(lambda y, z, x: _ssd_tail_rows(y, z, wn, wo, postw, x), y, z, x)


def _ssd_tail_rows(y, z, wn, wo, postw, x):
    y = y * _silu(z)
    parts = []
    for gi in range(SSD_GROUPS):
        gs = slice(gi * SSD_GW, (gi + 1) * SSD_GW)
        parts.append(_rms(y[:, gs], wn[:, gs]))
    yn = jnp.concatenate(parts, axis=1)
    out = _dot(yn.astype(BF16), wo)
    return x + _rms(out, postw)


def _ssd_prompt_kernel(*refs, n_pad, tile, has_prev):
    x_refs, refs = refs[:D_SLABS], refs[D_SLABS:]
    (s0_ref, c0_ref, prew_ref, postw_ref, win_ref, cw_ref, cb_ref, dtb_ref, alog_ref, dskip_ref,
     wn_ref, wo_ref, expand_ref) = refs[:13]
    y_ref, sout_ref, cout_ref, st_ref, cbuf_ref, ys_ref, slab_ref = refs[13 + (2 if has_prev else 0):]
    t = pl.program_id(1)
    first_buffered = SUBLANES - (SSD_CONV - 1)

    @pl.when(t == 0)
    def _():
        for gi in range(SSD_GROUPS):
            s0 = s0_ref[0, gi * SSD_HPG:(gi + 1) * SSD_HPG].reshape(SSD_GW, SSD_STATE)
            st_ref[gi] = s0.T
        cbuf_ref[...] = jnp.zeros(cbuf_ref.shape, F32)
        for v in range(SSD_CONV - 1):
            r = (v + 1) * SUBLANES - 1
            cbuf_ref[r:r + 1, :] = c0_ref[0, first_buffered + v:first_buffered + v + 1, :]

    x = _load_interleaved(x_refs, tile)
    hn = _rms(x, prew_ref[...])
    if n_pad:
        row = t * tile + _position_time(tile)[:, None]
        hn = jnp.where(row >= n_pad, hn, 0.0)
    hb = hn.astype(BF16)
    xbc = _dot(hb, win_ref[:, SSD_X0:SSD_DT0])
    dtr = _narrow_proj(hb, win_ref, SSD_DT0, SSD_HEADS)
    xc = _ssd_conv(xbc, cbuf_ref, cw_ref[...], cb_ref[...], tile)
    z = _dot(hb, win_ref[:, SSD_Z0:SSD_X0])
    xs = xc[:, :SSD_DINNER]
    bm = xc[:, SSD_DINNER:SSD_DINNER + SSD_GROUPS * SSD_STATE]
    cm = xc[:, SSD_DINNER + SSD_GROUPS * SSD_STATE:]
    dt = _softplus(dtr + dtb_ref[...])
    if n_pad:
        dt = jnp.where(row >= n_pad, dt, 0.0)
    a = dt * (-jnp.exp(alog_ref[...]))

    tm = _position_time(CHUNK)
    causal = tm[:, None] >= tm[None, :]
    tril = causal.astype(BF16)
    lo_half = lax.broadcasted_iota(jnp.int32, (CHUNK, 2 * SSD_HEADDIM), 1) < SSD_HEADDIM
    expand = expand_ref[...]
    for c in range(tile // CHUNK):
        rows = slice(c * CHUNK, (c + 1) * CHUNK)
        _ssd_scan_chunk(xs[rows], bm[rows], cm[rows], dt[rows], a[rows], expand,
                        st_ref, ys_ref, c * CHUNK, causal, tril, lo_half)
    y = ys_ref[...] + xs * dskip_ref[...]
    out = _ssd_tail(y, z, wn_ref[...], wo_ref[...], postw_ref[...], x)
    _store_interleaved(y_ref, slab_ref, out, tile)

    @pl.when(t == pl.num_programs(1) - 1)
    def _():
        for gi in range(SSD_GROUPS):
            sout_ref[0, gi * SSD_HPG:(gi + 1) * SSD_HPG] = (
                st_ref[gi].T.reshape(SSD_HPG, SSD_HEADDIM, SSD_STATE))
        cout_ref[0] = jnp.zeros(cout_ref.shape[1:], F32)
        for v in range(SSD_CONV - 1):
            r = (v + 1) * SUBLANES - 1
            cout_ref[0, first_buffered + v:first_buffered + v + 1, :] = cbuf_ref[r:r + 1, :]


def _ssd_weight_list(w, step):
    names = ("pre", "post", "win", "cw", "cb", "dtb", "alog", "dskip", "wn", "wo")
    consts = ("expand", "expand_n") if step else ("expand",)
    return [(w[k], True) for k in names] + [(w[k], False) for k in consts]


def _ssd_prompt_layer(x, s0, c0, prev_out, layer, w, *, n_pad, tile):
    bsz, seq, _ = x.shape
    s_tail = (SSD_HEADS, SSD_HEADDIM, SSD_STATE)
    c_tail = (SUBLANES, SSD_CONV_DIM)
    own = s0.shape[1] == bsz
    weights = _ssd_weight_list(w, step=False)
    carried = [] if prev_out is None else list(prev_out)
    n_in = D_SLABS + 2 + len(weights)
    x_slabs = [pl.BlockSpec((1, tile, LANES), lambda b, t, j=j: (b, t, j)) for j in range(D_SLABS)]
    return pl.pallas_call(
        functools.partial(_ssd_prompt_kernel, n_pad=n_pad, tile=tile, has_prev=bool(carried)),
        grid=(bsz, seq // tile),
        in_specs=x_slabs
                 + [_batch_block(layer, 1, s_tail, own), _batch_block(layer, 1, c_tail, own)]
                 + _weight_specs(weights, layer)
                 + [pl.BlockSpec(memory_space=pl.ANY)] * len(carried),
        out_specs=[pl.BlockSpec((1, tile, D_MODEL), lambda b, t: (b, t, 0)),
                   _batch_block(layer, 1, s_tail), _batch_block(layer, 1, c_tail)],
        out_shape=[jax.ShapeDtypeStruct(x.shape, F32),
                   jax.ShapeDtypeStruct((N_LAYERS_PER_MIXER, bsz) + s_tail, F32),
                   jax.ShapeDtypeStruct((N_LAYERS_PER_MIXER, bsz) + c_tail, F32)],
        scratch_shapes=[pltpu.VMEM((SSD_GROUPS, SSD_STATE, SSD_GW), F32),
                        pltpu.VMEM((WRAP_ROWS, SSD_CONV_DIM), F32),
                        pltpu.VMEM((tile, SSD_DINNER), F32),
                        pltpu.VMEM((D_SLABS, tile, LANES), F32)],
        input_output_aliases={n_in: 1, n_in + 1: 2} if carried else {},
        compiler_params=pltpu.CompilerParams(
            dimension_semantics=("arbitrary", "arbitrary"), vmem_limit_bytes=VMEM_LIMIT),
        name="ssd_prompt",
    )(*([x] * D_SLABS), s0, c0, *[a for a, _ in weights], *carried)


def _column(row, width, lane0=0):
    seg = row[:, lane0:lane0 + width]
    r = lax.broadcasted_iota(jnp.int32, (width, width), 0)
    c = lax.broadcasted_iota(jnp.int32, (width, width), 1)
    return jnp.sum(jnp.where(r == c, seg, 0.0), axis=1, keepdims=True)


def _gla_step_kernel(x_ref, s_ref, prew_ref, postw_ref, win_ref, wgu_ref, bg_ref, wn_ref, wo_ref,
                     *rest, bb, has_prev):
    y_ref, sout_ref, q_s, kl_s, v_s, e_s, o_s = rest[1:] if has_prev else rest
    i = pl.program_id(0)
    bsz = x_ref.shape[0]

    @pl.when(i == 0)
    def _():
        hb = _rms(x_ref[...], prew_ref[...]).astype(BF16)
        q_s[...] = _dot(hb, win_ref[:, GLA_Q0:GLA_K0]) * (GLA_HK ** -0.5)
        o_s[:, 0:GLA_DK] = _dot(hb, win_ref[:, GLA_K0:GLA_V0])
        k_t = o_s[:, 0:GLA_DK].T
        for h in range(GLA_HEADS):
            kl_s[h // 2, :, (h % 2) * bsz:(h % 2 + 1) * bsz] = (
                k_t[h * GLA_HK:(h + 1) * GLA_HK].astype(BF16))
        v_s[...] = _dot(hb, win_ref[:, GLA_V0:GLA_R0])
        e_s[...] = jnp.exp(_gla_gates(hb, win_ref, wgu_ref, bg_ref))

    sub = lax.broadcasted_iota(jnp.int32, (bsz, GLA_HV), 0)
    zero_bv = jnp.zeros((bsz, GLA_HV), BF16)

    for j in range(bb):
        row = i * bb + j
        qr = q_s[pl.ds(row, 1), :]
        er = e_s[pl.ds(row, 1), :]
        for p in range(GLA_HEADS // 2):
            vsel = [jnp.where(sub == row, v_s[:, (2 * p + e) * GLA_HV:(2 * p + e + 1) * GLA_HV],
                              0.0).astype(BF16) for e in range(2)]
            rhs = jnp.concatenate([jnp.concatenate([vsel[0], zero_bv], axis=1),
                                   jnp.concatenate([zero_bv, vsel[1]], axis=1)], axis=0)
            kv = _dot(kl_s[p], rhs)
            for e in range(2):
                h = 2 * p + e
                vs = slice(h * GLA_HV, (h + 1) * GLA_HV)
                ecol = _column(er, GLA_HK, h * GLA_HK)
                qcol = _column(qr, GLA_HK, h * GLA_HK)
                s_new = s_ref[j, h] * ecol + kv[:, e * GLA_HV:(e + 1) * GLA_HV]
                sout_ref[j, h] = s_new
                o_s[pl.ds(row, 1), vs] = jnp.sum(qcol * s_new, axis=0, keepdims=True)

    @pl.when(i == pl.num_programs(0) - 1)
    def _():
        x = x_ref[...]
        hb = _rms(x, prew_ref[...]).astype(BF16)
        r = _dot(hb, win_ref[:, GLA_R0:GLA_G0])
        y_ref[...] = _gla_tail(o_s[...], r, wn_ref[...], wo_ref[...], postw_ref[...], x)


def _gla_step_layer(x, states, prev_out, layer, w, *, bb=8):
    bsz = x.shape[0]
    weights = _gla_weight_list(w)
    sblk = _batch_block(layer, bb, (GLA_HEADS, GLA_HK, GLA_HV))
    carried = [] if prev_out is None else [prev_out]
    return pl.pallas_call(
        functools.partial(_gla_step_kernel, bb=bb, has_prev=bool(carried)),
        grid=(bsz // bb,),
        in_specs=[_full(x.shape), sblk] + _weight_specs(weights, layer)
                 + [pl.BlockSpec(memory_space=pl.ANY)] * len(carried),
        out_specs=[pl.BlockSpec(x.shape, lambda i: (0, 0)), sblk],
        out_shape=[jax.ShapeDtypeStruct(x.shape, F32), jax.ShapeDtypeStruct(states.shape, F32)],
        scratch_shapes=[pltpu.VMEM((bsz, GLA_DK), F32),
                        pltpu.VMEM((GLA_HEADS // 2, GLA_HK, 2 * bsz), BF16),
                        pltpu.VMEM((bsz, GLA_DV), F32),
                        pltpu.VMEM((bsz, GLA_DK), F32),
                        pltpu.VMEM((bsz, GLA_DV), F32)],
        input_output_aliases={2 + len(weights): 1} if carried else {},
        compiler_params=pltpu.CompilerParams(
            dimension_semantics=("arbitrary",), vmem_limit_bytes=VMEM_LIMIT),
        name="gla_step",
    )(x, states, *[a for a, _ in weights], *carried)


def _ssd_step_kernel(x_ref, s_ref, cv_ref, prew_ref, postw_ref, win_ref, cw_ref, cb_ref, dtb_ref,
                     alog_ref, dskip_ref, wn_ref, wo_ref, expand_ref, expand_n_ref, *rest,
                     bb, has_prev):
    (y_ref, sout_ref, cvout_ref, xs_s, xl_s, ct_s, b_s, e_s, yt_s) = rest[2:] if has_prev else rest
    i = pl.program_id(0)
    bsz = x_ref.shape[0]
    cd = SSD_CONV_DIM

    @pl.when(i == 0)
    def _():
        hb = _rms(x_ref[...], prew_ref[...]).astype(BF16)
        xbc = _dot(hb, win_ref[:, SSD_X0:SSD_DT0])
        dtr = _narrow_proj(hb, win_ref, SSD_DT0, SSD_HEADS)
        cw = cw_ref[...]
        conv = cb_ref[...] + xbc * cw[3:4, :]
        for t in range(SSD_CONV - 1):
            conv = conv + cv_ref[:, t * cd:(t + 1) * cd] * cw[t:t + 1, :]
        cvout_ref[:, 0:cd] = cv_ref[:, cd:2 * cd]
        cvout_ref[:, cd:2 * cd] = cv_ref[:, 2 * cd:3 * cd]
        cvout_ref[:, 2 * cd:3 * cd] = xbc
        xc = _silu(conv)
        xs = xc[:, :SSD_DINNER]
        dt = _softplus(dtr + dtb_ref[...])
        ea = jnp.exp(dt * (-jnp.exp(alog_ref[...])))
        xs_s[...] = xs
        xd_t = (xs * _dot_exact_rhs(dt, expand_ref[...])).T
        c_t = xc[:, SSD_DINNER + SSD_GROUPS * SSD_STATE:].T
        for p in range(SSD_GROUPS // 2):
            for e in range(2):
                gi = 2 * p + e
                xl_s[p, :, e * bsz:(e + 1) * bsz] = (
                    xd_t[gi * SSD_GW:(gi + 1) * SSD_GW].astype(BF16))
        ct_s[...] = c_t
        b_s[...] = xc[:, SSD_DINNER:SSD_DINNER + SSD_GROUPS * SSD_STATE]
        e_s[...] = _dot_exact_rhs(ea, expand_n_ref[...])
        yt_s[...] = jnp.zeros(yt_s.shape, F32)

    sub = lax.broadcasted_iota(jnp.int32, (bsz, SSD_STATE), 0)
    lane = lax.broadcasted_iota(jnp.int32, (SSD_STATE, bsz), 1)
    zero_bn = jnp.zeros((bsz, SSD_STATE), BF16)
    zero_nb = jnp.zeros((SSD_STATE, bsz), BF16)

    for p in range(SSD_GROUPS // 2):
        y_acc = None
        for j in range(bb):
            row = i * bb + j
            er = e_s[pl.ds(row, 1), :]
            bsel, csel = [], []
            for e in range(2):
                ns = slice((2 * p + e) * SSD_STATE, (2 * p + e + 1) * SSD_STATE)
                bsel.append(jnp.where(sub == row, b_s[:, ns], 0.0).astype(BF16))
                csel.append(jnp.where(lane == row, ct_s[ns, :], 0.0).astype(BF16))
            rhs_b = jnp.concatenate([jnp.concatenate([bsel[0], zero_bn], axis=1),
                                     jnp.concatenate([zero_bn, bsel[1]], axis=1)], axis=0)
            rhs_c = jnp.concatenate([jnp.concatenate([csel[0], zero_nb], axis=1),
                                     jnp.concatenate([zero_nb, csel[1]], axis=1)], axis=0)
            outer = _dot(xl_s[p], rhs_b)
            halves = []
            for e in range(2):
                pieces = []
                for r in range(SSD_HPG):
                    h = (2 * p + e) * SSD_HPG + r
                    s_new = (s_ref[j, h] * er[:, h * SSD_STATE:(h + 1) * SSD_STATE]
                             + outer[r * SSD_HEADDIM:(r + 1) * SSD_HEADDIM,
                                     e * SSD_STATE:(e + 1) * SSD_STATE])
                    sout_ref[j, h] = s_new
                    pieces.append(s_new.astype(BF16))
                halves.append(jnp.concatenate(pieces, axis=0))
            y_j = _dot(jnp.concatenate(halves, axis=1), rhs_c)
            y_acc = y_j if y_acc is None else y_acc + y_j
        yt_s[p] += y_acc

    @pl.when(i == pl.num_programs(0) - 1)
    def _():
        x = x_ref[...]
        hb = _rms(x, prew_ref[...]).astype(BF16)
        z = _dot(hb, win_ref[:, SSD_Z0:SSD_X0])
        y = jnp.concatenate([yt_s[p, :, e * bsz:(e + 1) * bsz].T
                             for p in range(SSD_GROUPS // 2) for e in range(2)], axis=1)
        y = y + xs_s[...] * dskip_ref[...]
        y_ref[...] = _ssd_tail(y, z, wn_ref[...], wo_ref[...], postw_ref[...], x)


def _ssd_step_layer(x, states, convs, prev_out, layer, w, *, bb=4):
    bsz = x.shape[0]
    weights = _ssd_weight_list(w, step=True)
    sblk = _batch_block(layer, bb, (SSD_HEADS, SSD_HEADDIM, SSD_STATE))
    cblk = pl.BlockSpec((None,) + convs.shape[1:], lambda i: (layer, 0, 0))
    carried = [] if prev_out is None else list(prev_out)
    n_in = 3 + len(weights)
    return pl.pallas_call(
        functools.partial(_ssd_step_kernel, bb=bb, has_prev=bool(carried)),
        grid=(bsz // bb,),
        in_specs=[_full(x.shape), sblk,
                  pl.BlockSpec(cblk.block_shape, cblk.index_map, pipeline_mode=pl.Buffered(1))]
                 + _weight_specs(weights, layer)
                 + [pl.BlockSpec(memory_space=pl.ANY)] * len(carried),
        out_specs=[pl.BlockSpec(x.shape, lambda i: (0, 0)), sblk, cblk],
        out_shape=[jax.ShapeDtypeStruct(x.shape, F32), jax.ShapeDtypeStruct(states.shape, F32),
                   jax.ShapeDtypeStruct(convs.shape, F32)],
        input_output_aliases={n_in: 1, n_in + 1: 2} if carried else {},
        scratch_shapes=[pltpu.VMEM((bsz, SSD_DINNER), F32),
                        pltpu.VMEM((SSD_GROUPS // 2, SSD_GW, 2 * bsz), BF16),
                        pltpu.VMEM((SSD_GROUPS * SSD_STATE, bsz), F32),
                        pltpu.VMEM((bsz, SSD_GROUPS * SSD_STATE), F32),
                        pltpu.VMEM((bsz, SSD_HEADS * SSD_STATE), F32),
                        pltpu.VMEM((SSD_GROUPS // 2, SSD_GW, 2 * bsz), F32)],
        compiler_params=pltpu.CompilerParams(
            dimension_semantics=("arbitrary",), vmem_limit_bytes=VMEM_LIMIT),
        name="ssd_step",
    )(x, states, convs, *[a for a, _ in weights], *carried)


def _pad_last(a, n):
    return jnp.pad(a, [(0, 0)] * (a.ndim - 1) + [(0, n - a.shape[-1])])


def _rows(a):
    return a[:, None, :]


def _gla_weights(pre, post, w_in, w_gate_up, b_gate, w_norm, w_out):
    return {
        "pre": _rows(pre), "post": _rows(post),
        "win": w_in.astype(BF16),
        "wgu": jnp.pad(w_gate_up, ((0, 0), (0, LANES - GLA_RANK), (0, 0))).astype(BF16),
        "bg": _rows(b_gate), "wn": _rows(w_norm), "wo": w_out.astype(BF16),
    }


def _ssd_weights(pre, post, w_in, conv_w, conv_b, dt_bias, a_log, d_skip, w_norm, w_out):
    head_of_lane = jnp.arange(SSD_DINNER) // SSD_HEADDIM
    head_of_lane_n = jnp.arange(SSD_HEADS * SSD_STATE) // SSD_STATE
    heads = jnp.arange(LANES)[:, None]
    return {
        "pre": _rows(pre), "post": _rows(post),
        "win": w_in.astype(BF16),
        "cw": jnp.pad(conv_w, ((0, 0), (0, SUBLANES - SSD_CONV), (0, 0))), "cb": _rows(conv_b),
        "dtb": _rows(_pad_last(dt_bias, LANES)), "alog": _rows(_pad_last(a_log, LANES)),
        "dskip": _rows(jnp.repeat(d_skip, SSD_HEADDIM, axis=1)),
        "wn": _rows(w_norm), "wo": w_out.astype(BF16),
        "expand": (heads == head_of_lane[None, :]).astype(BF16),
        "expand_n": (heads == head_of_lane_n[None, :]).astype(BF16),
    }


def _prompt_trunk(x, gla0, ssm0, conv0, gw, sw, *, n_pad, gla_tile, ssd_tile):
    gla_s = ssm_conv_s = None
    for j in range(N_LAYERS_PER_MIXER):
        x, gla_s = _gla_prompt_layer(x, gla0, gla_s, j, gw, n_pad=n_pad, tile=gla_tile)
        x, *ssm_conv_s = _ssd_prompt_layer(x, ssm0, conv0, ssm_conv_s, j, sw,
                                           n_pad=n_pad, tile=ssd_tile)
    return x, gla_s, ssm_conv_s[0], ssm_conv_s[1]


def kernel(x_prompt, x_sample, state_gla, state_ssm, state_conv, meta_tokens, pre_norm, post_norm,
           gla_w_in, gla_w_gate_up, gla_b_gate, gla_w_norm, gla_w_out,
           ssd_w_in, ssd_conv_w, ssd_conv_b, ssd_dt_bias, ssd_a_log, ssd_d_skip, ssd_w_norm, ssd_w_out):
    gw = _gla_weights(pre_norm[0::2], post_norm[0::2], gla_w_in, gla_w_gate_up, gla_b_gate,
                      gla_w_norm, gla_w_out)
    sw = _ssd_weights(pre_norm[1::2], post_norm[1::2], ssd_w_in, ssd_conv_w, ssd_conv_b,
                      ssd_dt_bias, ssd_a_log, ssd_d_skip, ssd_w_norm, ssd_w_out)

    n_pad = CHUNK - N_META
    x_meta = jnp.pad(meta_tokens.astype(F32), ((n_pad, 0), (0, 0)))[None]
    zg = jnp.zeros((N_LAYERS_PER_MIXER, 1, GLA_HEADS, GLA_HK, GLA_HV), F32)
    zs = jnp.zeros((N_LAYERS_PER_MIXER, 1, SSD_HEADS, SSD_HEADDIM, SSD_STATE), F32)
    zc = jnp.zeros((N_LAYERS_PER_MIXER, 1, SUBLANES, SSD_CONV_DIM), F32)
    _, mg, ms, mc = _prompt_trunk(x_meta, zg, zs, zc, gw, sw,
                                  n_pad=n_pad, gla_tile=CHUNK, ssd_tile=CHUNK)
    y_prompt, gla_p, ssm_p, conv_p = _prompt_trunk(x_prompt, mg, ms, mc, gw, sw, n_pad=0,
                                                   gla_tile=GLA_TILE, ssd_tile=SSD_TILE)
    conv_p = conv_p[:, :, SUBLANES - (SSD_CONV - 1):, :]

    xs = x_sample[:, 0, :]
    sbsz = xs.shape[0]
    convs = state_conv.reshape(N_LAYERS_PER_MIXER, sbsz, (SSD_CONV - 1) * SSD_CONV_DIM)
    gla_s = ssm_conv_s = None
    for j in range(N_LAYERS_PER_MIXER):
        xs, gla_s = _gla_step_layer(xs, state_gla, gla_s, j, gw)
        xs, *ssm_conv_s = _ssd_step_layer(xs, state_ssm, convs, ssm_conv_s, j, sw)
    ssm_s, conv_s = ssm_conv_s
    y_sample = xs[:, None, :]
    return (y_prompt, y_sample, gla_p, ssm_p, conv_p, gla_s, ssm_s,
            conv_s.reshape(N_LAYERS_PER_MIXER, sbsz, SSD_CONV - 1, SSD_CONV_DIM))
```
